```python
import math, functools
import jax, jax.numpy as jnp
from jax import lax
import numpy as np

D_MODEL = 1024
BATCH = 16
SEQ = 4096
DEPTH = 4
DEC_BATCH = 8
DEC_SEQ = 32
PAST_LEN = 1024

CHUNK = 64
N_MIXERS = 4
DEEPNORM_ALPHA = (2.0 * DEPTH) ** 0.25
DEEPNORM_BETA = (8.0 * DEPTH) ** -0.25
NORM_EPS = 1e-5
D_FF = 2816

A_HEADS = 8
A_KV_HEADS = 2
A_HEAD_DIM = 128
IDX_HEADS = 8
IDX_DIM = 64
IDX_TOPK_MAX = 256
Q_BLOCK = 128
T5_BUCKETS = 32
T5_MAX_DIST = 128
A_SPLITS = (A_HEADS * A_HEAD_DIM, A_KV_HEADS * A_HEAD_DIM, A_KV_HEADS * A_HEAD_DIM,
            IDX_HEADS * IDX_DIM, IDX_DIM, IDX_HEADS)

BAND_HEADS = 16
BAND_HEAD_DIM = 64
BAND_LEFT_CHUNKS = 8
BAND_WINDOW = BAND_LEFT_CHUNKS * CHUNK
BAND_MAX_REL = 128

SSD_D_INNER = 2 * D_MODEL
SSD_HEAD_DIM = 64
SSD_HEADS = SSD_D_INNER // SSD_HEAD_DIM
SSD_GROUPS = 8
SSD_D_STATE = 128
SSD_CONV = 4
SSD_CONV_DIM = SSD_D_INNER + 2 * SSD_GROUPS * SSD_D_STATE

MLSTM_D_INNER = 2 * D_MODEL
MLSTM_HEADS = 4
MLSTM_HEAD_DIM = MLSTM_D_INNER // MLSTM_HEADS
MLSTM_CONV = 4
MLSTM_QK_BLOCK = 4

kernel_name = "hybrid_streaming_encoder_step"

F32 = jnp.float32


def split_cols(z, sizes):
    cuts = [int(c) for c in np.cumsum(sizes)[:-1]]
    return jnp.split(z, cuts, axis=-1)


def layer_norm(x, g, b):
    xf = x.astype(F32)
    mu = jnp.mean(xf, axis=-1, keepdims=True)
    var = jnp.mean(jnp.square(xf - mu), axis=-1, keepdims=True)
    return ((xf - mu) * lax.rsqrt(var + NORM_EPS) * g + b).astype(x.dtype)


def post_norm(x, sub, g, b):
    return layer_norm(DEEPNORM_ALPHA * x + sub, g, b)


def swiglu_ffn(x, wg, wu, wd):
    return (jax.nn.silu(x @ wg) * (x @ wu)) @ wd


def group_norm(x, g, n_groups, center):
    xf = x.astype(F32)
    shp = xf.shape
    xg = xf.reshape(shp[:-1] + (n_groups, shp[-1] // n_groups))
    if center:
        xg = xg - jnp.mean(xg, axis=-1, keepdims=True)
    xg = xg * lax.rsqrt(jnp.mean(jnp.square(xg), axis=-1, keepdims=True) + NORM_EPS)
    return xg.reshape(shp) * g


def causal_conv(x, buf, w, b):
    width, length = w.shape[0], x.shape[1]
    xp = jnp.concatenate([buf.astype(x.dtype), x], axis=1)
    out = b + xp[:, 0:length] * w[0]
    for j in range(1, width):
        out = out + xp[:, j:j + length] * w[j]
    return out, xp[:, length:]


def chunk_scan(step, state, xs):
    length = xs[0].shape[1]
    if length <= CHUNK:
        return step(state, xs)
    n_chunks = length // CHUNK
    xs_c = tuple(jnp.swapaxes(a.reshape((a.shape[0], n_chunks, CHUNK) + a.shape[2:]), 0, 1) for a in xs)
    state, ys = lax.scan(step, state, xs_c)
    ys = jnp.swapaxes(ys, 0, 1)
    return state, ys.reshape((ys.shape[0], length) + ys.shape[3:])


def t5_bucket(rel):
    half = T5_BUCKETS // 2
    max_exact = half // 2
    ret = jnp.where(rel > 0, half, 0)
    n = jnp.abs(rel)
    nf = jnp.maximum(n, 1).astype(F32)
    large = max_exact + (jnp.log(nf / max_exact) / math.log(T5_MAX_DIST / max_exact)
                         * (half - max_exact)).astype(jnp.int32)
    large = jnp.minimum(large, half - 1)
    return ret + jnp.where(n < max_exact, n, large)


def project_a(x, w_in):
    bsz, length, _ = x.shape
    q, k, v, qi, ki, wi = split_cols(x @ w_in, A_SPLITS)
    return (q.reshape(bsz, length, A_HEADS, A_HEAD_DIM),
            k.reshape(bsz, length, A_KV_HEADS, A_HEAD_DIM),
            v.reshape(bsz, length, A_KV_HEADS, A_HEAD_DIM),
            qi.reshape(bsz, length, IDX_HEADS, IDX_DIM), ki, wi * IDX_HEADS ** -0.5)


def dsa_core(q, qi, wi, k, v, ki, q_pos, k_pos, n_sel, t5_table):
    bsz, t_len = q.shape[0], q.shape[1]
    group = A_HEADS // A_KV_HEADS
    q_chunk = q_pos // CHUNK
    admissible = (k_pos[None, :] // CHUNK) <= q_chunk[:, None]
    idx_rel = jax.nn.relu(jnp.einsum('bthd,bsd->bths', qi, ki).astype(F32) * IDX_DIM ** -0.5)
    score = jnp.einsum('bths,bth->bts', idx_rel, wi.astype(F32))
    score = jnp.where(admissible[None], score, -jnp.inf)
    _, sel = lax.top_k(score, n_sel)
    gather = jax.vmap(lambda rows, ids: rows[ids])
    k_sel = gather(k, sel)
    v_sel = gather(v, sel)
    sel_pos = k_pos[sel]
    valid = (sel_pos // CHUNK) <= q_chunk[None, :, None]
    qg = q.reshape(bsz, t_len, A_KV_HEADS, group, A_HEAD_DIM)
    logits = jnp.einsum('btkgd,btnkd->btkgn', qg, k_sel).astype(F32) * A_HEAD_DIM ** -0.5
    bias = t5_table[t5_bucket(sel_pos - q_pos[None, :, None])]
    bias = jnp.transpose(bias.reshape(bsz, t_len, n_sel, A_KV_HEADS, group), (0, 1, 3, 4, 2))
    logits = jnp.where(valid[:, :, None, None, :], logits + bias, -jnp.inf)
    p = jax.nn.softmax(logits, axis=-1).astype(v.dtype)
    out = jnp.einsum('btkgn,btnkd->btkgd', p, v_sel)
    return out.reshape(bsz, t_len, A_HEADS * A_HEAD_DIM)


def mixer_a_prompt(x, w_in, w_out, t5_table):
    bsz, length, _ = x.shape
    q, k, v, qi, ki, wi = project_a(x, w_in)
    k_pos = jnp.arange(length)
    n_sel = min(IDX_TOPK_MAX, length // 4)

    def block(b):
        s0 = b * Q_BLOCK
        sl = lambda a: lax.dynamic_slice_in_dim(a, s0, Q_BLOCK, axis=1)
        return dsa_core(sl(q), sl(qi), sl(wi), k, v, ki, s0 + jnp.arange(Q_BLOCK), k_pos, n_sel, t5_table)

    out = lax.map(block, jnp.arange(length // Q_BLOCK))
    out = jnp.swapaxes(out, 0, 1).reshape(bsz, length, A_HEADS * A_HEAD_DIM)
    return out @ w_out, k, v, ki


def mixer_a_sample(x, cache_k, cache_v, cache_ki, w_in, w_out, t5_table):
    t_len = x.shape[1]
    q, k, v, qi, ki, wi = project_a(x, w_in)
    past = cache_k.shape[1]
    k_all = jnp.concatenate([cache_k.astype(k.dtype), k], axis=1)
    v_all = jnp.concatenate([cache_v.astype(v.dtype), v], axis=1)
    ki_all = jnp.concatenate([cache_ki.astype(ki.dtype), ki], axis=1)
    n_sel = min(IDX_TOPK_MAX, (past + t_len) // 4)
    out = dsa_core(q, qi, wi, k_all, v_all, ki_all, past + jnp.arange(t_len), jnp.arange(past + t_len),
                   n_sel, t5_table)
    return out @ w_out, k, v, ki


def project_b(x, w_in):
    bsz, length, _ = x.shape
    q, k, v = split_cols(x @ w_in, (BAND_HEADS * BAND_HEAD_DIM,) * 3)
    shp = (bsz, length, BAND_HEADS, BAND_HEAD_DIM)
    return q.reshape(shp), k.reshape(shp), v.reshape(shp)


def band_core(q, k, v, q_pos, k_pos, valid, rel_table):
    logits = jnp.einsum('bthd,bshd->bhts', q, k).astype(F32) * BAND_HEAD_DIM ** -0.5
    rel = jnp.clip(q_pos[:, None] - k_pos[None, :], -BAND_MAX_REL, BAND_MAX_REL) + BAND_MAX_REL
    logits = jnp.where(valid[None, None], logits + rel_table[:, rel][None], -jnp.inf)
    p = jax.nn.softmax(logits, axis=-1).astype(v.dtype)
    return jnp.einsum('bhts,bshd->bthd', p, v)


def mixer_b_prompt(x, w_in, w_out, rel_table):
    bsz, length, _ = x.shape
    q, k, v = project_b(x, w_in)
    pad = ((0, 0), (BAND_WINDOW, 0), (0, 0), (0, 0))
    k_pad, v_pad = jnp.pad(k, pad), jnp.pad(v, pad)
    band = BAND_WINDOW + CHUNK

    def block(c):
        s0 = c * CHUNK
        k_pos = s0 - BAND_WINDOW + jnp.arange(band)
        valid = jnp.broadcast_to(k_pos[None, :] >= 0, (CHUNK, band))
        return band_core(lax.dynamic_slice_in_dim(q, s0, CHUNK, axis=1),
                         lax.dynamic_slice_in_dim(k_pad, s0, band, axis=1),
                         lax.dynamic_slice_in_dim(v_pad, s0, band, axis=1),
                         s0 + jnp.arange(CHUNK), k_pos, valid, rel_table)

    out = lax.map(block, jnp.arange(length // CHUNK))
    out = jnp.swapaxes(out, 0, 1).reshape(bsz, length, BAND_HEADS * BAND_HEAD_DIM)
    keep = min(BAND_WINDOW, length)
    return out @ w_out, k[:, length - keep:], v[:, length - keep:]


def mixer_b_sample(x, cache_k, cache_v, w_in, w_out, rel_table):
    bsz, t_len, _ = x.shape
    q, k, v = project_b(x, w_in)
    past = cache_k.shape[1]
    k_all = jnp.concatenate([cache_k.astype(k.dtype), k], axis=1)
    v_all = jnp.concatenate([cache_v.astype(v.dtype), v], axis=1)
    valid = jnp.ones((t_len, past + t_len), bool)
    out = band_core(q, k_all, v_all, past + jnp.arange(t_len), jnp.arange(past + t_len), valid, rel_table)
    out = out.reshape(bsz, t_len, BAND_HEADS * BAND_HEAD_DIM)
    return out @ w_out, k_all[:, t_len:], v_all[:, t_len:]


def ssd_step(a_head, h0, xs):
    xh, dt, bm, cm = xs
    length = xh.shape[1]
    cum = jnp.cumsum(dt * a_head, axis=1)
    causal = jnp.tril(jnp.ones((length, length), bool))
    seg = cum[:, :, None] - cum[:, None, :]
    decay = jnp.exp(jnp.where(causal[None, :, :, None, None], seg, -jnp.inf))
    xdt = xh * dt[..., None]
    cb = jnp.einsum('btgn,bsgn->btsg', cm, bm)
    y = jnp.einsum('btsg,btsgh,bsghp->btghp', cb, decay, xdt)
    y = y + jnp.exp(cum)[..., None] * jnp.einsum('btgn,bghpn->btghp', cm, h0)
    dec_end = jnp.exp(cum[:, -1:] - cum)
    h_new = (jnp.exp(cum[:, -1])[..., None, None] * h0
             + jnp.einsum('bsgh,bsghp,bsgn->bghpn', dec_end, xdt, bm))
    return h_new, y


def mixer_c(x, ssm0, conv0, w_in, conv_w, conv_b, dt_bias, a_log, d_skip, norm_g, w_out):
    bsz, length, _ = x.shape
    hg = SSD_HEADS // SSD_GROUPS
    z, xbc, dt = split_cols(x @ w_in, (SSD_D_INNER, SSD_CONV_DIM, SSD_HEADS))
    xbc, conv_new = causal_conv(xbc, conv0, conv_w, conv_b)
    xbc = jax.nn.silu(xbc)
    xh, bm, cm = split_cols(xbc, (SSD_D_INNER, SSD_GROUPS * SSD_D_STATE, SSD_GROUPS * SSD_D_STATE))
    xh = xh.reshape(bsz, length, SSD_GROUPS, hg, SSD_HEAD_DIM)
    bm = bm.reshape(bsz, length, SSD_GROUPS, SSD_D_STATE)
    cm = cm.reshape(bsz, length, SSD_GROUPS, SSD_D_STATE)
    dt = jax.nn.softplus(dt.astype(F32) + dt_bias).reshape(bsz, length, SSD_GROUPS, hg)
    a_head = -jnp.exp(a_log.astype(F32)).reshape(SSD_GROUPS, hg)
    h0 = ssm0.astype(F32).reshape(bsz, SSD_GROUPS, hg, SSD_HEAD_DIM, SSD_D_STATE)
    h_new, y = chunk_scan(functools.partial(ssd_step, a_head), h0, (xh, dt, bm, cm))
    y = y + d_skip.reshape(SSD_GROUPS, hg)[..., None] * xh
    y = y.reshape(bsz, length, SSD_D_INNER) * jax.nn.silu(z)
    y = group_norm(y, norm_g, SSD_GROUPS, False)
    out = y.astype(x.dtype) @ w_out
    return out, h_new.reshape(bsz, SSD_HEADS, SSD_HEAD_DIM, SSD_D_STATE), conv_new


def mlstm_step(state, xs):
    c0, n0, m0 = state
    q, k, v, i_pre, log_f = xs
    length = q.shape[1]
    f_cum = jnp.swapaxes(jnp.cumsum(log_f, axis=1), 1, 2)
    i_t = jnp.swapaxes(i_pre, 1, 2)
    causal = jnp.tril(jnp.ones((length, length), bool))
    d_log = jnp.where(causal, f_cum[..., :, None] - f_cum[..., None, :] + i_t[..., None, :], -jnp.inf)
    inter = f_cum + m0[..., None]
    m = jnp.maximum(jnp.max(d_log, axis=-1), inter)
    s = jnp.einsum('bthd,bshd->bhts', q, k).astype(F32) * jnp.exp(d_log - m[..., None])
    w_inter = jnp.exp(inter - m)
    num = (jnp.einsum('bhts,bshv->bthv', s, v)
           + jnp.swapaxes(w_inter, 1, 2)[..., None] * jnp.einsum('bthk,bhkv->bthv', q, c0))
    den = jnp.sum(s, axis=-1) + w_inter * jnp.einsum('bthk,bhk->bht', q, n0)
    h = num / jnp.swapaxes(jnp.maximum(jnp.abs(den), jnp.exp(-m)), 1, 2)[..., None]
    m_end = m[..., -1]
    w_end = jnp.exp(f_cum[..., -1:] - f_cum + i_t - m_end[..., None])
    decay = jnp.exp(f_cum[..., -1] + m0 - m_end)
    c_new = decay[..., None, None] * c0 + jnp.einsum('bhs,bshk,bshv->bhkv', w_end, k, v)
    n_new = decay[..., None] * n0 + jnp.einsum('bhs,bshk->bhk', w_end, k)
    return (c_new, n_new, m_end), h


def mixer_d(x, c0, n0, m0, conv0, w_in, conv_w, conv_b, wq_blk, wk_blk, gate_b, norm_g, w_out):
    bsz, length, _ = x.shape
    xc, v, o, gates = split_cols(x @ w_in, (MLSTM_D_INNER, MLSTM_D_INNER, MLSTM_D_INNER, 2 * MLSTM_HEADS))
    xa, conv_new = causal_conv(xc, conv0, conv_w, conv_b)
    xa = jax.nn.silu(xa).reshape(bsz, length, MLSTM_D_INNER // MLSTM_QK_BLOCK, MLSTM_QK_BLOCK)
    heads = (bsz, length, MLSTM_HEADS, MLSTM_HEAD_DIM)
    q = jnp.einsum('blnc,ncd->blnd', xa, wq_blk).reshape(heads)
    k = jnp.einsum('blnc,ncd->blnd', xa, wk_blk).reshape(heads) * MLSTM_HEAD_DIM ** -0.5
    v = v.reshape(heads)
    gates = gates.astype(F32) + gate_b
    i_pre = gates[..., :MLSTM_HEADS]
    log_f = jax.nn.log_sigmoid(gates[..., MLSTM_HEADS:])
    state = (c0.astype(F32), n0.astype(F32), m0.astype(F32))
    (c_new, n_new, m_new), h = chunk_scan(mlstm_step, state, (q, k, v, i_pre, log_f))
    h = jax.nn.sigmoid(o.astype(F32)).reshape(heads) * h
    h = group_norm(h.reshape(bsz, length, MLSTM_D_INNER), norm_g, MLSTM_HEADS, True)
    return h.astype(x.dtype) @ w_out, c_new, n_new, m_new, conv_new


def setup_inputs(seed: int = 0) -> dict:
    key = jax.random.key(seed)
    ks = iter(jax.random.split(key, 64))

    def nrm(shape, scale):
        return jax.random.normal(next(ks), shape, F32) * scale

    band_rows = min(BAND_WINDOW, PAST_LEN)
    dt0 = jnp.exp(jax.random.uniform(next(ks), (SSD_HEADS,), F32, math.log(1e-3), math.log(1e-1)))
    a_init = jax.random.uniform(next(ks), (SSD_HEADS,), F32, 1.0, 16.0)
    gate_b = jnp.concatenate([nrm((MLSTM_HEADS,), 0.1),
                              jnp.linspace(3.0, 6.0, MLSTM_HEADS, dtype=F32) + nrm((MLSTM_HEADS,), 0.1)])
    a_in = sum(A_SPLITS)
    c_in = SSD_D_INNER + SSD_CONV_DIM + SSD_HEADS
    d_in = 3 * MLSTM_D_INNER + 2 * MLSTM_HEADS
    return {
        "x_prompt": nrm((BATCH, SEQ, D_MODEL), 1.0),
        "x_sample": nrm((DEC_BATCH, DEC_SEQ, D_MODEL), 1.0),
        "cache_a_k": nrm((DEC_BATCH, PAST_LEN, A_KV_HEADS, A_HEAD_DIM), 1.0),
        "cache_a_v": nrm((DEC_BATCH, PAST_LEN, A_KV_HEADS, A_HEAD_DIM), 1.0),
        "cache_a_kidx": nrm((DEC_BATCH, PAST_LEN, IDX_DIM), 1.0),
        "cache_b_k": nrm((DEC_BATCH, band_rows, BAND_HEADS, BAND_HEAD_DIM), 1.0),
        "cache_b_v": nrm((DEC_BATCH, band_rows, BAND_HEADS, BAND_HEAD_DIM), 1.0),
        "state_c_ssm": nrm((DEC_BATCH, SSD_HEADS, SSD_HEAD_DIM, SSD_D_STATE), 0.5),
        "state_c_conv": nrm((DEC_BATCH, SSD_CONV - 1, SSD_CONV_DIM), 1.0),
        "state_d_c": nrm((DEC_BATCH, MLSTM_HEADS, MLSTM_HEAD_DIM, MLSTM_HEAD_DIM), 0.05),
        "state_d_n": nrm((DEC_BATCH, MLSTM_HEADS, MLSTM_HEAD_DIM), 0.05),
        "state_d_m": nrm((DEC_BATCH, MLSTM_HEADS), 1.0),
        "state_d_conv": nrm((DEC_BATCH, MLSTM_CONV - 1, MLSTM_D_INNER), 1.0),
        "a_w_in": nrm((D_MODEL, a_in), D_MODEL ** -0.5),
        "a_w_out": nrm((A_HEADS * A_HEAD_DIM, D_MODEL), (A_HEADS * A_HEAD_DIM) ** -0.5 * DEEPNORM_BETA),
        "t5_table": nrm((T5_BUCKETS, A_HEADS), 0.5),
        "b_w_in": nrm((D_MODEL, 3 * BAND_HEADS * BAND_HEAD_DIM), D_MODEL ** -0.5),
        "b_w_out": nrm((BAND_HEADS * BAND_HEAD_DIM, D_MODEL), (BAND_HEADS * BAND_HEAD_DIM) ** -0.5 * DEEPNORM_BETA),
        "b_rel_table": nrm((BAND_HEADS, 2 * BAND_MAX_REL + 1), 0.5),
        "c_w_in": nrm((D_MODEL, c_in), D_MODEL ** -0.5),
        "c_conv_w": nrm((SSD_CONV, SSD_CONV_DIM), SSD_CONV ** -0.5),
        "c_conv_b": nrm((SSD_CONV_DIM,), 0.02),
        "c_dt_bias": dt0 + jnp.log(-jnp.expm1(-dt0)),
        "c_a_log": jnp.log(a_init),
        "c_d_skip": 1.0 + nrm((SSD_HEADS,), 0.1),
        "c_norm_g": 1.0 + nrm((SSD_D_INNER,), 0.1),
        "c_w_out": nrm((SSD_D_INNER, D_MODEL), SSD_D_INNER ** -0.5 * DEEPNORM_BETA),
        "d_w_in": nrm((D_MODEL, d_in), D_MODEL ** -0.5),
        "d_conv_w": nrm((MLSTM_CONV, MLSTM_D_INNER), MLSTM_CONV ** -0.5),
        "d_conv_b": nrm((MLSTM_D_INNER,), 0.02),
        "d_wq_blk": nrm((MLSTM_D_INNER // MLSTM_QK_BLOCK, MLSTM_QK_BLOCK, MLSTM_QK_BLOCK), MLSTM_QK_BLOCK ** -0.5),
        "d_wk_blk": nrm((MLSTM_D_INNER // MLSTM_QK_BLOCK, MLSTM_QK_BLOCK, MLSTM_QK_BLOCK), MLSTM_QK_BLOCK ** -0.5),
        "d_gate_b": gate_b,
        "d_norm_g": 1.0 + nrm((MLSTM_D_INNER,), 0.1),
        "d_w_out": nrm((MLSTM_D_INNER, D_MODEL), MLSTM_D_INNER ** -0.5 * DEEPNORM_BETA),
        "ffn1_wg": nrm((DEPTH, D_MODEL, D_FF), D_MODEL ** -0.5),
        "ffn1_wu": nrm((DEPTH, D_MODEL, D_FF), D_MODEL ** -0.5),
        "ffn1_wd": nrm((DEPTH, D_FF, D_MODEL), D_FF ** -0.5 * DEEPNORM_BETA),
        "ffn2_wg": nrm((DEPTH, D_MODEL, D_FF), D_MODEL ** -0.5),
        "ffn2_wu": nrm((DEPTH, D_MODEL, D_FF), D_MODEL ** -0.5),
        "ffn2_wd": nrm((DEPTH, D_FF, D_MODEL), D_FF ** -0.5 * DEEPNORM_BETA),
        "ln_g": 1.0 + nrm((DEPTH, 3, D_MODEL), 0.05),
        "ln_b": nrm((DEPTH, 3, D_MODEL), 0.02),
    }


def reference(x_prompt, x_sample, cache_a_k, cache_a_v, cache_a_kidx, cache_b_k, cache_b_v,
              state_c_ssm, state_c_conv, state_d_c, state_d_n, state_d_m, state_d_conv,
              a_w_in, a_w_out, t5_table, b_w_in, b_w_out, b_rel_table,
              c_w_in, c_conv_w, c_conv_b, c_dt_bias, c_a_log, c_d_skip, c_norm_g, c_w_out,
              d_w_in, d_conv_w, d_conv_b, d_wq_blk, d_wk_blk, d_gate_b, d_norm_g, d_w_out,
              ffn1_wg, ffn1_wu, ffn1_wd, ffn2_wg, ffn2_wu, ffn2_wd, ln_g, ln_b):
    bp = x_prompt.shape[0]
    xp, xs = x_prompt, x_sample
    for i in range(DEPTH):
        xp = post_norm(xp, 0.5 * swiglu_ffn(xp, ffn1_wg[i], ffn1_wu[i], ffn1_wd[i]), ln_g[i, 0], ln_b[i, 0])
        xs = post_norm(xs, 0.5 * swiglu_ffn(xs, ffn1_wg[i], ffn1_wu[i], ffn1_wd[i]), ln_g[i, 0], ln_b[i, 0])
        kind = i % N_MIXERS
        if kind == 0:
            mp, a_k_p, a_v_p, a_kidx_p = mixer_a_prompt(xp, a_w_in, a_w_out, t5_table)
            ms, a_k_s, a_v_s, a_kidx_s = mixer_a_sample(xs, cache_a_k, cache_a_v, cache_a_kidx,
                                                        a_w_in, a_w_out, t5_table)
        elif kind == 1:
            mp, b_k_p, b_v_p = mixer_b_prompt(xp, b_w_in, b_w_out, b_rel_table)
            ms, b_k_s, b_v_s = mixer_b_sample(xs, cache_b_k, cache_b_v, b_w_in, b_w_out, b_rel_table)
        elif kind == 2:
            mp, c_ssm_p, c_conv_p = mixer_c(
                xp, jnp.zeros((bp, SSD_HEADS, SSD_HEAD_DIM, SSD_D_STATE), F32),
                jnp.zeros((bp, SSD_CONV - 1, SSD_CONV_DIM), xp.dtype),
                c_w_in, c_conv_w, c_conv_b, c_dt_bias, c_a_log, c_d_skip, c_norm_g, c_w_out)
            ms, c_ssm_s, c_conv_s = mixer_c(
                xs, state_c_ssm, state_c_conv,
                c_w_in, c_conv_w, c_conv_b, c_dt_bias, c_a_log, c_d_skip, c_norm_g, c_w_out)
        else:
            mp, d_c_p, d_n_p, d_m_p, d_conv_p = mixer_d(
                xp, jnp.zeros((bp, MLSTM_HEADS, MLSTM_HEAD_DIM, MLSTM_HEAD_DIM), F32),
                jnp.zeros((bp, MLSTM_HEADS, MLSTM_HEAD_DIM), F32), jnp.zeros((bp, MLSTM_HEADS), F32),
                jnp.zeros((bp, MLSTM_CONV - 1, MLSTM_D_INNER), xp.dtype),
                d_w_in, d_conv_w, d_conv_b, d_wq_blk, d_wk_blk, d_gate_b, d_norm_g, d_w_out)
            ms, d_c_s, d_n_s, d_m_s, d_conv_s = mixer_d(
                xs, state_d_c, state_d_n, state_d_m, state_d_conv,
                d_w_in, d_conv_w, d_conv_b, d_wq_blk, d_wk_blk, d_gate_b, d_norm_g, d_w_out)
        xp = post_norm(xp, mp, ln_g[i, 1], ln_b[i, 1])
        xs = post_norm(xs, ms, ln_g[i, 1], ln_b[i, 1])
        xp = post_norm(xp, 0.5 * swiglu_ffn(xp, ffn2_wg[i], ffn2_wu[i], ffn2_wd[i]), ln_g[i, 2], ln_b[i, 2])
        xs = post_norm(xs, 0.5 * swiglu_ffn(xs, ffn2_wg[i], ffn2_wu[i], ffn2_wd[i]), ln_g[i, 2], ln_b[i, 2])
    return (xp, xs,
            a_k_p, a_v_p, a_kidx_p, b_k_p, b_v_p, c_ssm_p, c_conv_p, d_c_p, d_n_p, d_m_p, d_conv_p,
            a_k_s, a_v_s, a_kidx_s, b_k_s, b_v_s, c_ssm_s, c_conv_s, d_c_s, d_n_s, d_m_s, d_conv_s)
```

```python
import functools
import math

import numpy as np
import jax
import jax.numpy as jnp
from jax import lax
from jax.experimental import pallas as pl
from jax.experimental.pallas import tpu as pltpu

F32 = jnp.float32
BF16 = jnp.bfloat16
I32 = jnp.int32

CHUNK = 64
NORM_EPS = 1e-5
A_HEADS, A_KV_HEADS, A_HEAD_DIM = 8, 2, 128
IDX_HEADS, IDX_DIM, IDX_TOPK_MAX = 8, 64, 256
Q_BLOCK = 128
T5_BUCKETS, T5_MAX_DIST = 32, 128
BAND_HEADS, BAND_HEAD_DIM, BAND_LEFT_CHUNKS, BAND_MAX_REL = 16, 64, 8, 128
BAND_WINDOW = BAND_LEFT_CHUNKS * CHUNK
SSD_HEAD_DIM, SSD_GROUPS, SSD_D_STATE, SSD_CONV = 64, 8, 128, 4
MLSTM_HEADS, MLSTM_CONV, MLSTM_QK_BLOCK = 4, 4, 4

LANES = 128
NEG_BIG = -1e30
VMEM_LIMIT = 56 * 1024 * 1024

NT_DIMS = (((1,), (1,)), ((), ()))
TN_DIMS = (((0,), (0,)), ((), ()))


def _params(*sem):
    return pltpu.CompilerParams(dimension_semantics=sem, vmem_limit_bytes=VMEM_LIMIT)


def _resident(shape, index_map):
    return pl.BlockSpec(shape, index_map, pipeline_mode=pl.Buffered(1))


def _layer_norm(y, g, b):
    mu = jnp.mean(y, axis=-1, keepdims=True)
    yc = y - mu
    var = jnp.mean(yc * yc, axis=-1, keepdims=True)
    return yc * lax.rsqrt(var + NORM_EPS) * g + b


def _silu(x):
    return x * jax.nn.sigmoid(x)


def _row_tile(n, want):
    t = min(n, want)
    assert n % t == 0, (n, t)
    return t


def _ffn_kernel(x_ref, wg_ref, wu_ref, wd_ref, g_ref, b_ref, o_ref, *, alpha, f_chunk):
    x = x_ref[...]
    xb = x.astype(BF16)
    d_ff = wg_ref.shape[1]
    acc = jnp.zeros(x.shape, F32)
    for c in range(d_ff // f_chunk):
        sl = slice(c * f_chunk, (c + 1) * f_chunk)
        gate = jnp.dot(xb, wg_ref[:, sl], preferred_element_type=F32)
        up = jnp.dot(xb, wu_ref[:, sl], preferred_element_type=F32)
        h = (_silu(gate) * up).astype(BF16)
        acc = acc + jnp.dot(h, wd_ref[sl, :], preferred_element_type=F32)
    o_ref[...] = _layer_norm(alpha * x + 0.5 * acc, g_ref[...], b_ref[...])


def ffn_postnorm(x, wg, wu, wd, layer, g, b, alpha, tm=512):
    n, d = x.shape
    d_ff = wg.shape[2]
    tm = _row_tile(n, tm)
    f_chunk = d_ff // 2 if (d_ff // 2) % LANES == 0 else d_ff
    return pl.pallas_call(
        functools.partial(_ffn_kernel, alpha=alpha, f_chunk=f_chunk),
        grid=(n // tm,),
        in_specs=[
            pl.BlockSpec((tm, d), lambda i: (i, 0)),
            _resident((None, d, d_ff), lambda i: (layer, 0, 0)),
            _resident((None, d, d_ff), lambda i: (layer, 0, 0)),
            _resident((None, d_ff, d), lambda i: (layer, 0, 0)),
            _resident((1, d), lambda i: (0, 0)),
            _resident((1, d), lambda i: (0, 0)),
        ],
        out_specs=pl.BlockSpec((tm, d), lambda i: (i, 0)),
        out_shape=jax.ShapeDtypeStruct((n, d), F32),
        compiler_params=_params("parallel"),
        name="ffn_postnorm",
    )(x, wg, wu, wd, g, b)


def _proj_kernel(x_ref, w_ref, *o_refs, cols):
    xb = x_ref[...].astype(BF16)
    done = {}
    for (off, pad_w, true_w), o_ref in zip(cols, o_refs):
        if (off, pad_w) not in done:
            done[(off, pad_w)] = jnp.dot(xb, w_ref[:, off:off + pad_w], preferred_element_type=F32)
        o_ref[...] = done[(off, pad_w)][:, :true_w].astype(o_ref.dtype)


def _pad_cols(w, widths):
    pieces, offs, off, src = [], [], 0, 0
    for wd in widths:
        pad_w = -(-wd // LANES) * LANES
        pieces.append(w[:, src:src + wd])
        if pad_w != wd:
            pieces.append(jnp.zeros((w.shape[0], pad_w - wd), w.dtype))
        offs.append((off, pad_w, wd))
        off += pad_w
        src += wd
    return jnp.concatenate(pieces, axis=1), offs


def project(x, w_pad, outs, tm=512, name="project"):
    n, d = x.shape
    tm = _row_tile(n, tm)
    cols = tuple(c for c, _ in outs)
    return pl.pallas_call(
        functools.partial(_proj_kernel, cols=cols),
        grid=(n // tm,),
        in_specs=[pl.BlockSpec((tm, d), lambda i: (i, 0)),
                  _resident(w_pad.shape, lambda i: (0, 0))],
        out_specs=[pl.BlockSpec((tm, c[2]), lambda i: (i, 0)) for c in cols],
        out_shape=[jax.ShapeDtypeStruct((n, c[2]), dt) for c, dt in outs],
        compiler_params=_params("parallel"),
        name=name,
    )(x, w_pad)


def _proj_postnorm_kernel(x_ref, m_ref, w_ref, g_ref, b_ref, o_ref, *, alpha):
    sub = jnp.dot(m_ref[...], w_ref[...], preferred_element_type=F32)
    o_ref[...] = _layer_norm(alpha * x_ref[...] + sub, g_ref[...], b_ref[...])


def proj_postnorm(x, m, w, g, b, alpha, tm=512):
    n, d = x.shape
    k = m.shape[1]
    tm = _row_tile(n, tm)
    return pl.pallas_call(
        functools.partial(_proj_postnorm_kernel, alpha=alpha),
        grid=(n // tm,),
        in_specs=[pl.BlockSpec((tm, d), lambda i: (i, 0)),
                  pl.BlockSpec((tm, k), lambda i: (i, 0)),
                  _resident((k, d), lambda i: (0, 0)),
                  _resident((1, d), lambda i: (0, 0)),
                  _resident((1, d), lambda i: (0, 0))],
        out_specs=pl.BlockSpec((tm, d), lambda i: (i, 0)),
        out_shape=jax.ShapeDtypeStruct((n, d), F32),
        compiler_params=_params("parallel"),
        name="proj_postnorm",
    )(x, m, w, g, b)


def _t5_bucket_np(rel):
    half = T5_BUCKETS // 2
    max_exact = half // 2
    n = np.abs(rel)
    nf = np.maximum(n, 1).astype(np.float32)
    large = max_exact + (np.log(nf / np.float32(max_exact)) / np.float32(math.log(T5_MAX_DIST / max_exact))
                         * np.float32(half - max_exact)).astype(np.int32)
    large = np.minimum(large, half - 1)
    return np.where(rel > 0, half, 0) + np.where(n < max_exact, n, large)


def _t5_bias_tiles(t5_table):
    t = np.arange(LANES)[:, None]
    s = np.arange(LANES)[None, :]
    rel = np.stack([s - t, s - t - LANES, np.full((LANES, LANES), -2 * LANES)])
    assert T5_MAX_DIST <= LANES + 1
    idx = _t5_bucket_np(rel)
    return jnp.transpose(t5_table[idx], (0, 3, 1, 2))


def _dsa_kernel(q_ref, qi_ref, wi_ref, k_ref, v_ref, ki_ref, bias_ref, o_ref, key_scr, *,
                tq, q_off, l_true, n_sel, lp):
    j = pl.program_id(1)
    q0 = q_off + j * tq
    kend = jnp.minimum(((q0 + tq - 1) // CHUNK + 1) * CHUNK, l_true)
    n_tiles = (kend + LANES - 1) // LANES
    jd = q0 // LANES
    qpos = q0 + lax.broadcasted_iota(I32, (tq, LANES), 0)
    lane = lax.broadcasted_iota(I32, (tq, LANES), 1)
    q_chunk = qpos // CHUNK
    wi = wi_ref[...] * (IDX_HEADS ** -0.5)
    int_min = jnp.int32(-2 ** 31)

    def tile_off(kt):
        return pl.multiple_of(kt * LANES, LANES)

    def admissible(kt):
        kpos = kt * LANES + lane
        return ((kpos // CHUNK) <= q_chunk) & (kpos < l_true), kpos

    def score_body(kt, carry):
        ki_t = ki_ref[pl.ds(tile_off(kt), LANES), :]
        acc = jnp.zeros((tq, LANES), F32)
        for h in range(IDX_HEADS):
            s = lax.dot_general(qi_ref[:, h * IDX_DIM:(h + 1) * IDX_DIM], ki_t, NT_DIMS,
                                preferred_element_type=F32)
            acc = acc + jnp.maximum(s * IDX_DIM ** -0.5, 0.0) * wi[:, h:h + 1]
        adm, _ = admissible(kt)
        sc = jnp.where(adm, jnp.where(acc == 0.0, 0.0, acc), -jnp.inf)
        bits = lax.bitcast_convert_type(sc, I32)
        key_scr[:, pl.ds(tile_off(kt), LANES)] = jnp.where(bits < 0, bits ^ jnp.int32(0x7FFFFFFF), bits)
        return carry

    lax.fori_loop(0, n_tiles, score_body, 0)

    def count(pred):
        def body(kt, acc):
            _, kpos = admissible(kt)
            return acc + pred(key_scr[:, pl.ds(tile_off(kt), LANES)], kpos).astype(I32)
        acc = lax.fori_loop(0, n_tiles, body, jnp.zeros((tq, LANES), I32))
        return jnp.sum(acc, axis=1, keepdims=True)

    def thr_body(i, thr_u):
        cand_u = thr_u | lax.shift_left(jnp.int32(1), 31 - i)
        cand = cand_u ^ int_min
        cnt = count(lambda key, kpos: key >= cand)
        return jnp.where(cnt >= n_sel, cand_u, thr_u)

    thr = lax.fori_loop(0, 32, thr_body, jnp.zeros((tq, 1), I32)) ^ int_min
    need = n_sel - count(lambda key, kpos: key > thr)
    n_eq = count(lambda key, kpos: key == thr)
    has_tie = jnp.max((n_eq - need).astype(F32)) > 0.0

    nbits = int(lp).bit_length()

    def tie_cut():
        def cut_body(i, cut):
            cand = cut | lax.shift_left(jnp.int32(1), nbits - 1 - i)
            cnt = count(lambda key, kpos: (key == thr) & (kpos < cand))
            return jnp.where(cnt <= need, cand, cut)
        return lax.fori_loop(0, nbits, cut_body, jnp.zeros((tq, 1), I32))

    cut = lax.cond(has_tie, tie_cut, lambda: jnp.full((tq, 1), 2 ** 30, I32))

    grp = A_HEADS // A_KV_HEADS
    scale = A_HEAD_DIM ** -0.5
    for g in range(A_KV_HEADS):
        qg = jnp.concatenate(
            [q_ref[:, (g * grp + i) * A_HEAD_DIM:(g * grp + i + 1) * A_HEAD_DIM] for i in range(grp)], axis=0)
        kv_cols = slice(g * A_HEAD_DIM, (g + 1) * A_HEAD_DIM)

        def att_body(kt, carry, qg=qg, kv_cols=kv_cols, g=g):
            m, l, acc = carry
            off = tile_off(kt)
            k_t = k_ref[pl.ds(off, LANES), kv_cols]
            v_t = v_ref[pl.ds(off, LANES), kv_cols]
            lg = lax.dot_general(qg, k_t, NT_DIMS, preferred_element_type=F32) * scale
            bias = bias_ref[jnp.clip(jd - kt, 0, 2), g * grp:(g + 1) * grp, 0:tq, :]
            lg = lg.reshape(grp, tq, LANES) + bias
            key = key_scr[:, pl.ds(off, LANES)]
            adm, kpos = admissible(kt)
            sel = (((key > thr) | ((key == thr) & (kpos < cut))) & adm)[None]
            lg = jnp.where(sel, lg, NEG_BIG)
            m_new = jnp.maximum(m, jnp.max(lg, axis=-1, keepdims=True))
            p = jnp.where(sel, jnp.exp(lg - m_new), 0.0)
            a = jnp.exp(m - m_new)
            l_new = a * l + jnp.sum(p, axis=-1, keepdims=True)
            pv = jnp.dot(p.reshape(grp * tq, LANES).astype(BF16), v_t, preferred_element_type=F32)
            return m_new, l_new, a * acc + pv.reshape(grp, tq, A_HEAD_DIM)

        init = (jnp.full((grp, tq, 1), NEG_BIG, F32), jnp.zeros((grp, tq, 1), F32),
                jnp.zeros((grp, tq, A_HEAD_DIM), F32))
        _, l, acc = lax.fori_loop(0, n_tiles, att_body, init)
        out = acc / l
        for i in range(grp):
            o_ref[:, (g * grp + i) * A_HEAD_DIM:(g * grp + i + 1) * A_HEAD_DIM] = out[i].astype(o_ref.dtype)


def dsa_attention(q, qi, wi, k, v, ki, bias, *, tq, q_off, l_true, n_sel):
    bsz, t_len, _ = q.shape
    lp = k.shape[1]
    assert lp % LANES == 0 and t_len % tq == 0 and q_off % LANES == 0 and tq <= LANES
    assert tq == LANES or t_len == tq
    qspec = lambda w: pl.BlockSpec((None, tq, w), lambda b, j: (b, j, 0))
    kspec = lambda w: pl.BlockSpec((None, lp, w), lambda b, j: (b, 0, 0))
    return pl.pallas_call(
        functools.partial(_dsa_kernel, tq=tq, q_off=q_off, l_true=l_true, n_sel=n_sel, lp=lp),
        grid=(bsz, t_len // tq),
        in_specs=[qspec(q.shape[2]), qspec(qi.shape[2]), qspec(wi.shape[2]),
                  kspec(k.shape[2]), kspec(v.shape[2]), kspec(ki.shape[2]),
                  _resident(bias.shape, lambda b, j: (0, 0, 0, 0))],
        out_specs=qspec(q.shape[2]),
        out_shape=jax.ShapeDtypeStruct(q.shape, BF16),
        scratch_shapes=[pltpu.VMEM((tq, lp), I32)],
        compiler_params=_params("parallel", "arbitrary"),
        name="dsa_attention",
    )(q, qi, wi, k, v, ki, bias)


def _band_bias(rel_table, chunk, grp):
    win = BAND_WINDOW + grp
    t = np.arange(grp)[:, None]
    s = np.arange(win)[None, :]
    idx = np.clip(t - s + BAND_WINDOW, -BAND_MAX_REL, BAND_MAX_REL) + BAND_MAX_REL
    lo = (t // chunk) * chunk
    allowed = (s >= lo) & (s < lo + BAND_WINDOW + chunk)
    return jnp.where(allowed[None], rel_table[:, idx], NEG_BIG)


def _band_kernel(q_ref, kp_ref, kc_ref, vp_ref, vc_ref, bias_ref, o_ref, kcat, vcat, *, grp, first_has_no_past):
    i = pl.program_id(1)
    prev = kp_ref.shape[0]
    tq = q_ref.shape[0]
    win = prev + grp
    kcat[0:prev, :] = kp_ref[...]
    kcat[prev:prev + tq, :] = kc_ref[...]
    vcat[0:prev, :] = vp_ref[...]
    vcat[prev:prev + tq, :] = vc_ref[...]
    scale = BAND_HEAD_DIM ** -0.5
    lane = lax.broadcasted_iota(I32, (grp, win), 1)
    for r0 in range(0, tq, grp):
        if first_has_no_past:
            kpos_ok = (i * tq - prev + r0 + lane) >= 0
        for h in range(BAND_HEADS):
            cols = slice(h * BAND_HEAD_DIM, (h + 1) * BAND_HEAD_DIM)
            lg = lax.dot_general(q_ref[r0:r0 + grp, cols], kcat[r0:r0 + win, cols], NT_DIMS,
                                 preferred_element_type=F32) * scale + bias_ref[h]
            if first_has_no_past:
                lg = jnp.where(kpos_ok, lg, NEG_BIG)
            p = jnp.exp(lg - jnp.max(lg, axis=-1, keepdims=True))
            den = jnp.sum(p, axis=-1, keepdims=True)
            pv = jnp.dot(p.astype(BF16), vcat[r0:r0 + win, cols], preferred_element_type=F32)
            o_ref[r0:r0 + grp, cols] = (pv / den).astype(o_ref.dtype)


def band_attention_prompt(q, k, v, bias, *, tq=512, grp=128):
    bsz, length, w = q.shape
    assert tq == BAND_WINDOW and length % tq == 0
    cur = pl.BlockSpec((None, tq, w), lambda b, i: (b, i, 0))
    prv = pl.BlockSpec((None, tq, w), lambda b, i: (b, jnp.maximum(i - 1, 0), 0))
    return pl.pallas_call(
        functools.partial(_band_kernel, grp=grp, first_has_no_past=True),
        grid=(bsz, length // tq),
        in_specs=[cur, prv, cur, prv, cur, _resident(bias.shape, lambda b, i: (0, 0, 0))],
        out_specs=cur,
        out_shape=jax.ShapeDtypeStruct(q.shape, BF16),
        scratch_shapes=[pltpu.VMEM((2 * tq, w), BF16), pltpu.VMEM((2 * tq, w), BF16)],
        compiler_params=_params("parallel", "arbitrary"),
        name="band_attention",
    )(q, k, k, v, v, bias)


def band_attention_sample(q, k_past, k_new, v_past, v_new, bias):
    bsz, t_len, w = q.shape
    past = k_past.shape[1]
    assert past == BAND_WINDOW
    new = pl.BlockSpec((None, t_len, w), lambda b, i: (b, 0, 0))
    old = pl.BlockSpec((None, past, w), lambda b, i: (b, 0, 0))
    return pl.pallas_call(
        functools.partial(_band_kernel, grp=t_len, first_has_no_past=False),
        grid=(bsz, 1),
        in_specs=[new, old, new, old, new, _resident(bias.shape, lambda b, i: (0, 0, 0))],
        out_specs=new,
        out_shape=jax.ShapeDtypeStruct(q.shape, BF16),
        scratch_shapes=[pltpu.VMEM((past + t_len, w), BF16), pltpu.VMEM((past + t_len, w), BF16)],
        compiler_params=_params("parallel", "arbitrary"),
        name="band_attention_sample",
    )(q, k_past, k_new, v_past, v_new, bias)


HIST = 8


def _causal_conv_chunk(x_ref, hist_scr, w_ref, b_ref, tc, width):
    hist_scr[HIST:HIST + tc, :] = x_ref[...]
    base = HIST - (width - 1)
    out = b_ref[...] + hist_scr[base:base + tc, :] * w_ref[0:1, :]
    for j in range(1, width):
        out = out + hist_scr[base + j:base + j + tc, :] * w_ref[j:j + 1, :]
    tail = hist_scr[HIST + tc - (width - 1):HIST + tc, :]
    hist_scr[base:HIST, :] = tail
    return out


def _tri(tc):
    r = lax.broadcasted_iota(I32, (tc, tc), 0)
    c = lax.broadcasted_iota(I32, (tc, tc), 1)
    return r >= c


def _eye(n):
    r = lax.broadcasted_iota(I32, (n, n), 0)
    c = lax.broadcasted_iota(I32, (n, n), 1)
    return (r == c).astype(F32)


def _cumsum_rows(x, causal):
    return jnp.dot(causal.astype(F32), x, precision=lax.Precision.HIGHEST, preferred_element_type=F32)


def _transpose_f32(x):
    return lax.dot_general(_eye(x.shape[1]), x, NT_DIMS, precision=lax.Precision.HIGHEST,
                           preferred_element_type=F32)


def _ssd_kernel(xbc_ref, z_ref, dt_ref, conv0_ref, h0_ref, cw_ref, cb_ref, dtb_ref, alog_ref, dskip_ref,
                ng_ref, y_ref, hs_ref, hist_scr, y_scr, *, tc, d_inner, n_heads):
    c = pl.program_id(1)

    @pl.when(c == 0)
    def _():
        hs_ref[...] = h0_ref[...]
        hist_scr[HIST - (SSD_CONV - 1):HIST, :] = conv0_ref[...]

    xs = _silu(_causal_conv_chunk(xbc_ref, hist_scr, cw_ref, cb_ref, tc, SSD_CONV))
    gn = SSD_GROUPS * SSD_D_STATE
    hg = n_heads // SSD_GROUPS
    causal = _tri(tc)
    x_dt = dt_ref[...] + dtb_ref[...]
    dt = jnp.maximum(x_dt, 0.0) + jnp.log1p(jnp.exp(-jnp.abs(x_dt)))
    a_head = -jnp.exp(alog_ref[...])
    cum = _cumsum_rows(dt * a_head, causal)
    cum_t = _transpose_f32(cum)
    cum_last = cum[tc - 1:tc, :]
    for g in range(SSD_GROUPS):
        bm = xs[:, d_inner + g * SSD_D_STATE:d_inner + (g + 1) * SSD_D_STATE].astype(BF16)
        cm = xs[:, d_inner + gn + g * SSD_D_STATE:d_inner + gn + (g + 1) * SSD_D_STATE].astype(BF16)
        cb = lax.dot_general(cm, bm, NT_DIMS, preferred_element_type=F32)
        for hh in range(hg):
            h = g * hg + hh
            cols = slice(h * SSD_HEAD_DIM, (h + 1) * SSD_HEAD_DIM)
            col = cum[:, h:h + 1]
            seg = col - cum_t[h:h + 1, :]
            decay = jnp.exp(jnp.where(causal, seg, -jnp.inf))
            xh = xs[:, cols]
            xdt = xh * dt[:, h:h + 1]
            h0 = hs_ref[h]
            y = jnp.dot((cb * decay).astype(BF16), xdt.astype(BF16), preferred_element_type=F32)
            y = y + jnp.exp(col) * lax.dot_general(cm, h0.astype(BF16), NT_DIMS, preferred_element_type=F32)
            y_scr[:, cols] = y + dskip_ref[:, cols] * xh
            last = cum_last[:, h:h + 1]
            w = (jnp.exp(last - col) * xdt).astype(BF16)
            hs_ref[h] = jnp.exp(last) * h0 + lax.dot_general(w, bm, TN_DIMS, preferred_element_type=F32)
    yz = y_scr[...] * _silu(z_ref[...])
    gw = d_inner // SSD_GROUPS
    for g in range(SSD_GROUPS):
        cols = slice(g * gw, (g + 1) * gw)
        seg = yz[:, cols]
        ms = jnp.mean(seg * seg, axis=-1, keepdims=True)
        y_ref[:, cols] = (seg * lax.rsqrt(ms + NORM_EPS) * ng_ref[:, cols]).astype(y_ref.dtype)


def ssd_scan(xbc, z, dt, conv0, h0, conv_w, conv_b, dt_bias, a_log, d_skip_cols, norm_g, *, tc):
    bsz, length, cc = xbc.shape
    d_inner = z.shape[2]
    n_heads = dt.shape[2]
    assert length % tc == 0
    seq = lambda w: pl.BlockSpec((None, tc, w), lambda b, c: (b, c, 0))
    per_b3 = lambda s: pl.BlockSpec((None,) + s, lambda b, c: (b,) + (0,) * len(s))
    row = lambda w: _resident((1, w), lambda b, c: (0, 0))
    return pl.pallas_call(
        functools.partial(_ssd_kernel, tc=tc, d_inner=d_inner, n_heads=n_heads),
        grid=(bsz, length // tc),
        in_specs=[seq(cc), seq(d_inner), seq(n_heads), per_b3(conv0.shape[1:]), per_b3(h0.shape[1:]),
                  _resident(conv_w.shape, lambda b, c: (0, 0)), row(cc), row(n_heads), row(n_heads),
                  row(d_inner), row(d_inner)],
        out_specs=[seq(d_inner), per_b3(h0.shape[1:])],
        out_shape=[jax.ShapeDtypeStruct((bsz, length, d_inner), BF16), jax.ShapeDtypeStruct(h0.shape, F32)],
        scratch_shapes=[pltpu.VMEM((HIST + tc, cc), F32), pltpu.VMEM((tc, d_inner), F32)],
        compiler_params=_params("parallel", "arbitrary"),
        name="ssd_scan",
    )(xbc, z, dt, conv0, h0, conv_w, conv_b, dt_bias, a_log, d_skip_cols, norm_g)


def _mlstm_kernel(xc_ref, v_ref, o_ref, gates_ref, conv0_ref, c0_ref, n0_ref, m0_ref, cw_ref, cb_ref, wq_ref,
                  wk_ref, gb_ref, ng_ref, h_ref, cs_ref, ns_ref, ms_ref, hist_scr, *, tc, d_inner):
    c = pl.program_id(1)

    @pl.when(c == 0)
    def _():
        cs_ref[...] = c0_ref[...]
        ns_ref[...] = n0_ref[...]
        ms_ref[...] = m0_ref[...]
        hist_scr[HIST - (MLSTM_CONV - 1):HIST, :] = conv0_ref[...]

    xa = _silu(_causal_conv_chunk(xc_ref, hist_scr, cw_ref, cb_ref, tc, MLSTM_CONV)).astype(BF16)
    blk = wq_ref.shape[1]
    dh = d_inner // MLSTM_HEADS
    q = jnp.concatenate([jnp.dot(xa[:, j * blk:(j + 1) * blk], wq_ref[j], preferred_element_type=F32)
                         for j in range(d_inner // blk)], axis=1)
    k = jnp.concatenate([jnp.dot(xa[:, j * blk:(j + 1) * blk], wk_ref[j], preferred_element_type=F32)
                         for j in range(d_inner // blk)], axis=1) * dh ** -0.5
    causal = _tri(tc)
    gates = gates_ref[...] + gb_ref[...]
    log_f = jnp.minimum(gates, 0.0) - jnp.log1p(jnp.exp(-jnp.abs(gates)))
    f_cum = _cumsum_rows(log_f, causal)
    f_cum_t = _transpose_f32(f_cum)
    gates_t = _transpose_f32(gates)
    for hh in range(MLSTM_HEADS):
        cols = slice(hh * dh, (hh + 1) * dh)
        fh = MLSTM_HEADS + hh
        qh = q[:, cols].astype(BF16)
        kh = k[:, cols]
        vh = v_ref[:, cols]
        fc = f_cum[:, fh:fh + 1]
        i_col = gates[:, hh:hh + 1]
        m0 = ms_ref[0:1, hh:hh + 1]
        d_log = jnp.where(causal, fc - f_cum_t[fh:fh + 1, :] + gates_t[hh:hh + 1, :], -jnp.inf)
        inter = fc + m0
        m = jnp.maximum(jnp.max(d_log, axis=-1, keepdims=True), inter)
        s = lax.dot_general(qh, kh.astype(BF16), NT_DIMS, preferred_element_type=F32) * jnp.exp(d_log - m)
        w_inter = jnp.exp(inter - m)
        c0 = cs_ref[hh]
        n0 = ns_ref[hh:hh + 1, :]
        num = (jnp.dot(s.astype(BF16), vh, preferred_element_type=F32)
               + w_inter * jnp.dot(qh, c0.astype(BF16), preferred_element_type=F32))
        den = (jnp.sum(s, axis=-1, keepdims=True)
               + w_inter * jnp.sum(q[:, cols] * n0, axis=-1, keepdims=True))
        h = num / jnp.maximum(jnp.abs(den), jnp.exp(-m))
        m_end = m[tc - 1:tc, :]
        f_last = fc[tc - 1:tc, :]
        w_end = jnp.exp(f_last - fc + i_col - m_end)
        decay = jnp.exp(f_last + m0 - m_end)
        wk = w_end * kh
        cs_ref[hh] = decay * c0 + lax.dot_general(wk.astype(BF16), vh, TN_DIMS, preferred_element_type=F32)
        ns_ref[hh:hh + 1, :] = decay * n0 + jnp.sum(wk, axis=0, keepdims=True)
        ms_ref[0:1, hh:hh + 1] = m_end
        h = jax.nn.sigmoid(o_ref[:, cols]) * h
        h = h - jnp.mean(h, axis=-1, keepdims=True)
        h = h * lax.rsqrt(jnp.mean(h * h, axis=-1, keepdims=True) + NORM_EPS)
        h_ref[:, cols] = (h * ng_ref[:, cols]).astype(h_ref.dtype)


def mlstm_scan(xc, v, o, gates, conv0, c0, n0, m0, conv_w, conv_b, wq_bd, wk_bd, gate_b, norm_g, *, tc):
    bsz, length, d_inner = xc.shape
    assert length % tc == 0
    seq = lambda w: pl.BlockSpec((None, tc, w), lambda b, c: (b, c, 0))
    per_b = lambda s: pl.BlockSpec((None,) + s, lambda b, c: (b,) + (0,) * len(s))
    row = lambda w: _resident((1, w), lambda b, c: (0, 0))
    return pl.pallas_call(
        functools.partial(_mlstm_kernel, tc=tc, d_inner=d_inner),
        grid=(bsz, length // tc),
        in_specs=[seq(d_inner), seq(d_inner), seq(d_inner), seq(gates.shape[2]), per_b(conv0.shape[1:]),
                  per_b(c0.shape[1:]), per_b(n0.shape[1:]), per_b(m0.shape[1:]),
                  _resident(conv_w.shape, lambda b, c: (0, 0)), row(d_inner),
                  _resident(wq_bd.shape, lambda b, c: (0, 0, 0)), _resident(wk_bd.shape, lambda b, c: (0, 0, 0)),
                  row(gates.shape[2]), row(d_inner)],
        out_specs=[seq(d_inner), per_b(c0.shape[1:]), per_b(n0.shape[1:]), per_b(m0.shape[1:])],
        out_shape=[jax.ShapeDtypeStruct((bsz, length, d_inner), BF16), jax.ShapeDtypeStruct(c0.shape, F32),
                   jax.ShapeDtypeStruct(n0.shape, F32), jax.ShapeDtypeStruct(m0.shape, F32)],
        scratch_shapes=[pltpu.VMEM((HIST + tc, d_inner), F32)],
        compiler_params=_params("parallel", "arbitrary"),
        name="mlstm_scan",
    )(xc, v, o, gates, conv0, c0, n0, m0, conv_w, conv_b, wq_bd, wk_bd, gate_b, norm_g)


def _block_diag(w_blk, blk):
    n, c, d = w_blk.shape
    per = blk // c
    eye = jnp.eye(per, dtype=w_blk.dtype)
    tiles = w_blk.reshape(n // per, per, c, d)
    return jnp.einsum("jpcd,pq->jpcqd", tiles, eye).reshape(n // per, per * c, per * d)


def _mixer_a(xp, xs, shapes, cache_k, cache_v, cache_ki, w_in, t5_table):
    (bp, lp_), (bs, ls) = shapes
    hd, kvd, idd = A_HEADS * A_HEAD_DIM, A_KV_HEADS * A_HEAD_DIM, IDX_HEADS * IDX_DIM
    w_pad, cols = _pad_cols(w_in, (hd, kvd, kvd, idd, IDX_DIM, IDX_HEADS))
    outs = [(cols[0], BF16), (cols[1], F32), (cols[1], BF16), (cols[2], F32), (cols[2], BF16),
            (cols[3], BF16), (cols[4], F32), (cols[4], BF16), (cols[5], F32)]
    w_pad = w_pad.astype(BF16)
    bias = _t5_bias_tiles(t5_table)

    q, k, kb, v, vb, qi, ki, kib, wi = project(xp, w_pad, outs, name="project_a")
    r3 = lambda a, b_, l: a.reshape(b_, l, a.shape[-1])
    att_p = dsa_attention(r3(q, bp, lp_), r3(qi, bp, lp_), r3(wi, bp, lp_), r3(kb, bp, lp_), r3(vb, bp, lp_),
                          r3(kib, bp, lp_), bias, tq=Q_BLOCK, q_off=0, l_true=lp_,
                          n_sel=min(IDX_TOPK_MAX, lp_ // 4))
    outs_p = (k.reshape(bp, lp_, A_KV_HEADS, A_HEAD_DIM), v.reshape(bp, lp_, A_KV_HEADS, A_HEAD_DIM),
              ki.reshape(bp, lp_, IDX_DIM))

    q, k, kb, v, vb, qi, ki, kib, wi = project(xs, w_pad, outs, name="project_a")
    past = cache_k.shape[1]
    total = past + ls
    lpad = -(-total // LANES) * LANES

    def with_past(cache, new):
        parts = [cache.reshape(bs, past, -1).astype(BF16), r3(new, bs, ls)]
        if lpad != total:
            parts.append(jnp.zeros((bs, lpad - total, new.shape[-1]), BF16))
        return jnp.concatenate(parts, axis=1)

    att_s = dsa_attention(r3(q, bs, ls), r3(qi, bs, ls), r3(wi, bs, ls), with_past(cache_k, kb),
                          with_past(cache_v, vb), with_past(cache_ki, kib), bias, tq=ls, q_off=past,
                          l_true=total, n_sel=min(IDX_TOPK_MAX, total // 4))
    outs_s = (k.reshape(bs, ls, A_KV_HEADS, A_HEAD_DIM), v.reshape(bs, ls, A_KV_HEADS, A_HEAD_DIM),
              ki.reshape(bs, ls, IDX_DIM))
    return att_p.reshape(bp * lp_, hd), att_s.reshape(bs * ls, hd), outs_p, outs_s


def _mixer_b(xp, xs, shapes, cache_k, cache_v, w_in, rel_table):
    (bp, lp_), (bs, ls) = shapes
    hd = BAND_HEADS * BAND_HEAD_DIM
    w_pad, cols = _pad_cols(w_in, (hd, hd, hd))
    outs = [(cols[0], BF16), (cols[1], F32), (cols[1], BF16), (cols[2], F32), (cols[2], BF16)]
    w_pad = w_pad.astype(BF16)

    q, k, kb, v, vb = project(xp, w_pad, outs, name="project_b")
    r3 = lambda a, b_, l: a.reshape(b_, l, hd)
    att_p = band_attention_prompt(r3(q, bp, lp_), r3(kb, bp, lp_), r3(vb, bp, lp_),
                                  _band_bias(rel_table, CHUNK, 2 * CHUNK), grp=2 * CHUNK)
    keep = min(BAND_WINDOW, lp_)
    heads = lambda a, b_, l: a.reshape(b_, l, BAND_HEADS, BAND_HEAD_DIM)
    outs_p = (heads(k, bp, lp_)[:, lp_ - keep:], heads(v, bp, lp_)[:, lp_ - keep:])

    q, k, kb, v, vb = project(xs, w_pad, outs, name="project_b")
    past = cache_k.shape[1]
    att_s = band_attention_sample(r3(q, bs, ls), cache_k.reshape(bs, past, hd).astype(BF16), r3(kb, bs, ls),
                                  cache_v.reshape(bs, past, hd).astype(BF16), r3(vb, bs, ls),
                                  _band_bias(rel_table, ls, ls))
    outs_s = (jnp.concatenate([cache_k, heads(k, bs, ls)], axis=1)[:, ls:],
              jnp.concatenate([cache_v, heads(v, bs, ls)], axis=1)[:, ls:])
    return att_p.reshape(bp * lp_, hd), att_s.reshape(bs * ls, hd), outs_p, outs_s


def _mixer_c(xp, xs, shapes, ssm0, conv0, w_in, conv_w, conv_b, dt_bias, a_log, d_skip, norm_g):
    n_heads = a_log.shape[0]
    d_inner = n_heads * SSD_HEAD_DIM
    conv_dim = conv_w.shape[1]
    w_pad, cols = _pad_cols(w_in, (d_inner, conv_dim, n_heads))
    outs = [(cols[0], F32), (cols[1], F32), (cols[2], F32)]
    w_pad = w_pad.astype(BF16)
    row = lambda a: a.reshape(1, -1)
    d_skip_cols = jnp.repeat(d_skip, SSD_HEAD_DIM).reshape(1, d_inner)
    res = []
    for x, (b_, l), h0, c0 in ((xp, shapes[0], None, None), (xs, shapes[1], ssm0, conv0)):
        if h0 is None:
            h0 = jnp.zeros((b_, n_heads, SSD_HEAD_DIM, SSD_D_STATE), F32)
            c0 = jnp.zeros((b_, SSD_CONV - 1, conv_dim), F32)
        z, xbc, dt = project(x, w_pad, outs, name="project_c")
        xbc3 = xbc.reshape(b_, l, conv_dim)
        y, h_new = ssd_scan(xbc3, z.reshape(b_, l, d_inner), dt.reshape(b_, l, n_heads), c0, h0, conv_w,
                            row(conv_b), row(dt_bias), row(a_log), d_skip_cols, row(norm_g), tc=min(CHUNK, l))
        conv_new = jnp.concatenate([c0, xbc3], axis=1)[:, l:]
        res.append((y.reshape(b_ * l, d_inner), (h_new, conv_new)))
    return res[0][0], res[1][0], res[0][1], res[1][1]


def _mixer_d(xp, xs, shapes, c0s, n0s, m0s, conv0s, w_in, conv_w, conv_b, wq_blk, wk_blk, gate_b, norm_g):
    d_inner = conv_w.shape[1]
    dh = d_inner // MLSTM_HEADS
    w_pad, cols = _pad_cols(w_in, (d_inner, d_inner, d_inner, 2 * MLSTM_HEADS))
    gcol = (cols[3][0], cols[3][1], cols[3][1])
    outs = [(cols[0], F32), (cols[1], BF16), (cols[2], F32), (gcol, F32)]
    w_pad = w_pad.astype(BF16)
    blk = 2 * LANES
    wq_bd = _block_diag(wq_blk, blk).astype(BF16)
    wk_bd = _block_diag(wk_blk, blk).astype(BF16)
    gate_b_pad = jnp.zeros((1, gcol[1]), F32).at[0, :2 * MLSTM_HEADS].set(gate_b)
    row = lambda a: a.reshape(1, -1)
    res = []
    for x, (b_, l), st in ((xp, shapes[0], None), (xs, shapes[1], (c0s, n0s, m0s, conv0s))):
        if st is None:
            st = (jnp.zeros((b_, MLSTM_HEADS, dh, dh), F32), jnp.zeros((b_, MLSTM_HEADS, dh), F32),
                  jnp.zeros((b_, MLSTM_HEADS), F32), jnp.zeros((b_, MLSTM_CONV - 1, d_inner), F32))
        c0, n0, m0, conv0 = st
        xc, v, o, gates = project(x, w_pad, outs, name="project_d")
        r3 = lambda a: a.reshape(b_, l, a.shape[-1])
        h, c_new, n_new, m_new = mlstm_scan(r3(xc), r3(v), r3(o), r3(gates), conv0, c0, n0,
                                            m0.reshape(b_, 1, MLSTM_HEADS), conv_w, row(conv_b), wq_bd, wk_bd,
                                            gate_b_pad, row(norm_g), tc=min(CHUNK, l))
        conv_new = jnp.concatenate([conv0, r3(xc)], axis=1)[:, l:]
        res.append((h.reshape(b_ * l, d_inner), (c_new, n_new, m_new.reshape(b_, MLSTM_HEADS), conv_new)))
    return res[0][0], res[1][0], res[0][1], res[1][1]


def kernel(x_prompt, x_sample, cache_a_k, cache_a_v, cache_a_kidx, cache_b_k, cache_b_v, state_c_ssm, state_c_conv, state_d_c, state_d_n, state_d_m, state_d_conv, a_w_in, a_w_out, t5_table, b_w_in, b_w_out, b_rel_table, c_w_in, c_conv_w, c_conv_b, c_dt_bias, c_a_log, c_d_skip, c_norm_g, c_w_out, d_w_in, d_conv_w, d_conv_b, d_wq_blk, d_wk_blk, d_gate_b, d_norm_g, d_w_out, ffn1_wg, ffn1_wu, ffn1_wd, ffn2_wg, ffn2_wu, ffn2_wd, ln_g, ln_b):
    bp, lp_, d = x_prompt.shape
    bs, ls, _ = x_sample.shape
    depth = ffn1_wg.shape[0]
    alpha = (2.0 * depth) ** 0.25
    shapes = ((bp, lp_), (bs, ls))
    xp = x_prompt.reshape(bp * lp_, d)
    xs = x_sample.reshape(bs * ls, d)
    ffn_w = [[w.astype(BF16) for w in ws] for ws in ((ffn1_wg, ffn1_wu, ffn1_wd), (ffn2_wg, ffn2_wu, ffn2_wd))]
    w_out = [w.astype(BF16) for w in (a_w_out, b_w_out, c_w_out, d_w_out)]
    row = lambda a: a.reshape(1, -1)
    extra = {}
    for i in range(depth):
        g, b = ln_g[i], ln_b[i]
        xp = ffn_postnorm(xp, *ffn_w[0], i, row(g[0]), row(b[0]), alpha)
        xs = ffn_postnorm(xs, *ffn_w[0], i, row(g[0]), row(b[0]), alpha)
        kind = i % 4
        if kind == 0:
            mp, ms, op, os_ = _mixer_a(xp, xs, shapes, cache_a_k, cache_a_v, cache_a_kidx, a_w_in, t5_table)
        elif kind == 1:
            mp, ms, op, os_ = _mixer_b(xp, xs, shapes, cache_b_k, cache_b_v, b_w_in, b_rel_table)
        elif kind == 2:
            mp, ms, op, os_ = _mixer_c(xp, xs, shapes, state_c_ssm, state_c_conv, c_w_in, c_conv_w, c_conv_b,
                                       c_dt_bias, c_a_log, c_d_skip, c_norm_g)
        else:
            mp, ms, op, os_ = _mixer_d(xp, xs, shapes, state_d_c, state_d_n, state_d_m, state_d_conv, d_w_in,
                                       d_conv_w, d_conv_b, d_wq_blk, d_wk_blk, d_gate_b, d_norm_g)
        extra[kind] = (op, os_)
        xp = proj_postnorm(xp, mp, w_out[kind], row(g[1]), row(b[1]), alpha)
        xs = proj_postnorm(xs, ms, w_out[kind], row(g[1]), row(b[1]), alpha)
        xp = ffn_postnorm(xp, *ffn_w[1], i, row(g[2]), row(b[2]), alpha)
        xs = ffn_postnorm(xs, *ffn_w[1], i, row(g[2]), row(b[2]), alpha)
    prompt_side = tuple(t for kind in range(4) for t in extra[kind][0])
    sample_side = tuple(t for kind in range(4) for t in extra[kind][1])
    return (xp.reshape(bp, lp_, d), xs.reshape(bs, ls, d)) + prompt_side + sample_side
```

```python
import functools
import math

import numpy as np
import jax
import jax.numpy as jnp
from jax import lax
from jax.experimental import pallas as pl
from jax.experimental.pallas import tpu as pltpu

F32 = jnp.float32
BF16 = jnp.bfloat16
I32 = jnp.int32

CHUNK = 64
NORM_EPS = 1e-5
A_HEADS, A_KV_HEADS, A_HEAD_DIM = 8, 2, 128
IDX_HEADS, IDX_DIM, IDX_TOPK_MAX = 8, 64, 256
Q_BLOCK = 128
T5_BUCKETS, T5_MAX_DIST = 32, 128
BAND_HEADS, BAND_HEAD_DIM, BAND_LEFT_CHUNKS, BAND_MAX_REL = 16, 64, 8, 128
BAND_WINDOW = BAND_LEFT_CHUNKS * CHUNK
SSD_HEAD_DIM, SSD_GROUPS, SSD_D_STATE, SSD_CONV = 64, 8, 128, 4
MLSTM_HEADS, MLSTM_CONV, MLSTM_QK_BLOCK = 4, 4, 4

LANES = 128
NEG_BIG = -1e30
VMEM_LIMIT = 56 * 1024 * 1024

NT_DIMS = (((1,), (1,)), ((), ()))
TN_DIMS = (((0,), (0,)), ((), ()))


def _params(*sem):
    return pltpu.CompilerParams(dimension_semantics=sem, vmem_limit_bytes=VMEM_LIMIT)


def _resident(shape, index_map):
    return pl.BlockSpec(shape, index_map, pipeline_mode=pl.Buffered(1))


def _layer_norm(y, g, b):
    mu = jnp.mean(y, axis=-1, keepdims=True)
    yc = y - mu
    var = jnp.mean(yc * yc, axis=-1, keepdims=True)
    return yc * lax.rsqrt(var + NORM_EPS) * g + b


def _silu(x):
    return x * jax.nn.sigmoid(x)


def _row_tile(n, want):
    t = min(n, want)
    assert n % t == 0, (n, t)
    return t


def _ffn_kernel(x_ref, wg_ref, wu_ref, wd_ref, g_ref, b_ref, o_ref, *, alpha, f_chunk):
    x = x_ref[...]
    xb = x.astype(BF16)
    d_ff = wg_ref.shape[1]
    acc = jnp.zeros(x.shape, F32)
    for c in range(d_ff // f_chunk):
        sl = slice(c * f_chunk, (c + 1) * f_chunk)
        gate = jnp.dot(xb, wg_ref[:, sl], preferred_element_type=F32)
        up = jnp.dot(xb, wu_ref[:, sl], preferred_element_type=F32)
        h = (_silu(gate) * up).astype(BF16)
        acc = acc + jnp.dot(h, wd_ref[sl, :], preferred_element_type=F32)
    o_ref[...] = _layer_norm(alpha * x + 0.5 * acc, g_ref[...], b_ref[...])


def ffn_postnorm(x, wg, wu, wd, layer, g, b, alpha, tm=512):
    n, d = x.shape
    d_ff = wg.shape[2]
    tm = _row_tile(n, tm)
    f_chunk = d_ff // 2 if (d_ff // 2) % LANES == 0 else d_ff
    return pl.pallas_call(
        functools.partial(_ffn_kernel, alpha=alpha, f_chunk=f_chunk),
        grid=(n // tm,),
        in_specs=[
            pl.BlockSpec((tm, d), lambda i: (i, 0)),
            _resident((None, d, d_ff), lambda i: (layer, 0, 0)),
            _resident((None, d, d_ff), lambda i: (layer, 0, 0)),
            _resident((None, d_ff, d), lambda i: (layer, 0, 0)),
            _resident((1, d), lambda i: (0, 0)),
            _resident((1, d), lambda i: (0, 0)),
        ],
        out_specs=pl.BlockSpec((tm, d), lambda i: (i, 0)),
        out_shape=jax.ShapeDtypeStruct((n, d), F32),
        compiler_params=_params("parallel"),
        name="ffn_postnorm",
    )(x, wg, wu, wd, g, b)


def _proj_kernel(x_ref, w_ref, *o_refs, cols, scales):
    xb = x_ref[...].astype(BF16)
    done = {}
    for (off, pad_w, true_w), scale, o_ref in zip(cols, scales, o_refs):
        if (off, pad_w) not in done:
            done[(off, pad_w)] = jnp.dot(xb, w_ref[:, off:off + pad_w], preferred_element_type=F32)
        y = done[(off, pad_w)][:, :true_w]
        o_ref[...] = (y if scale is None else y * scale).astype(o_ref.dtype)


def _pad_cols(w, widths):
    pieces, offs, off, src = [], [], 0, 0
    for wd in widths:
        pad_w = -(-wd // LANES) * LANES
        pieces.append(w[:, src:src + wd])
        if pad_w != wd:
            pieces.append(jnp.zeros((w.shape[0], pad_w - wd), w.dtype))
        offs.append((off, pad_w, wd))
        off += pad_w
        src += wd
    return jnp.concatenate(pieces, axis=1), offs


def project(x, w_pad, outs, tm=512, name="project"):
    n, d = x.shape
    tm = _row_tile(n, tm)
    cols = tuple(o[0] for o in outs)
    scales = tuple(o[2] if len(o) > 2 else None for o in outs)
    outs = [o[:2] for o in outs]
    return pl.pallas_call(
        functools.partial(_proj_kernel, cols=cols, scales=scales),
        grid=(n // tm,),
        in_specs=[pl.BlockSpec((tm, d), lambda i: (i, 0)),
                  _resident(w_pad.shape, lambda i: (0, 0))],
        out_specs=[pl.BlockSpec((tm, c[2]), lambda i: (i, 0)) for c in cols],
        out_shape=[jax.ShapeDtypeStruct((n, c[2]), dt) for c, dt in outs],
        compiler_params=_params("parallel"),
        name=name,
    )(x, w_pad)


def _proj_postnorm_kernel(x_ref, m_ref, w_ref, g_ref, b_ref, o_ref, *, alpha):
    sub = jnp.dot(m_ref[...], w_ref[...], preferred_element_type=F32)
    o_ref[...] = _layer_norm(alpha * x_ref[...] + sub, g_ref[...], b_ref[...])


def proj_postnorm(x, m, w, g, b, alpha, tm=512):
    n, d = x.shape
    k = m.shape[1]
    tm = _row_tile(n, tm)
    return pl.pallas_call(
        functools.partial(_proj_postnorm_kernel, alpha=alpha),
        grid=(n // tm,),
        in_specs=[pl.BlockSpec((tm, d), lambda i: (i, 0)),
                  pl.BlockSpec((tm, k), lambda i: (i, 0)),
                  _resident((k, d), lambda i: (0, 0)),
                  _resident((1, d), lambda i: (0, 0)),
                  _resident((1, d), lambda i: (0, 0))],
        out_specs=pl.BlockSpec((tm, d), lambda i: (i, 0)),
        out_shape=jax.ShapeDtypeStruct((n, d), F32),
        compiler_params=_params("parallel"),
        name="proj_postnorm",
    )(x, m, w, g, b)


def _t5_bucket_np(rel):
    half = T5_BUCKETS // 2
    max_exact = half // 2
    n = np.abs(rel)
    nf = np.maximum(n, 1).astype(np.float32)
    large = max_exact + (np.log(nf / np.float32(max_exact)) / np.float32(math.log(T5_MAX_DIST / max_exact))
                         * np.float32(half - max_exact)).astype(np.int32)
    large = np.minimum(large, half - 1)
    return np.where(rel > 0, half, 0) + np.where(n < max_exact, n, large)


KEY_TILE = 2 * LANES


def _t5_bias_tiles(t5_table):
    t = np.arange(LANES)[:, None]
    s = np.arange(KEY_TILE)[None, :]
    rel = np.stack([s - t - o * LANES for o in range(3)])
    assert T5_MAX_DIST <= LANES
    near = t5_table[_t5_bucket_np(rel)]
    far = t5_table[int(_t5_bucket_np(np.array(-T5_MAX_DIST)))]
    return jnp.transpose(near - far, (0, 3, 1, 2))


def _dsa_kernel(q_ref, qi_ref, wi_ref, kt_ref, v_ref, kit_ref, bias_ref, o_ref,
                key_scr, mask_scr, lg_scr, wib_scr, m_scr, l_scr, acc_scr, *, tq, q_off, l_true, n_sel, lp):
    j = pl.program_id(1)
    q0 = q_off + j * tq
    kend = jnp.minimum(((q0 + tq - 1) // CHUNK + 1) * CHUNK, l_true)
    n_tiles = (kend + KEY_TILE - 1) // KEY_TILE
    n_far = jnp.maximum((q0 - LANES) // KEY_TILE, 0)
    qpos = q0 + lax.broadcasted_iota(I32, (tq, KEY_TILE), 0)
    lane = lax.broadcasted_iota(I32, (tq, KEY_TILE), 1)
    q_chunk = qpos // CHUNK
    int_min = jnp.int32(-2 ** 31)

    def tile_off(kt):
        return pl.multiple_of(kt * KEY_TILE, KEY_TILE)

    def admissible(kt):
        kpos = kt * KEY_TILE + lane
        return ((kpos // CHUNK) <= q_chunk) & (kpos < l_true), kpos

    wi = wi_ref[...] * (IDX_HEADS ** -0.5) * (IDX_DIM ** -0.5)
    for h in range(IDX_HEADS):
        wib_scr[h] = jnp.broadcast_to(wi[:, h:h + 1], (tq, KEY_TILE))

    def score_body(kt, carry):
        ki_t = kit_ref[:, pl.ds(tile_off(kt), KEY_TILE)]
        accs = [jnp.zeros((tq, KEY_TILE), F32)] * 2
        for h in range(IDX_HEADS):
            s = jnp.dot(qi_ref[:, h * IDX_DIM:(h + 1) * IDX_DIM], ki_t, preferred_element_type=F32)
            accs[h % 2] = accs[h % 2] + jnp.maximum(s, 0.0) * wib_scr[h]
        acc = accs[0] + accs[1]
        adm, _ = admissible(kt)
        bits = lax.bitcast_convert_type(jnp.where(adm, acc, -jnp.inf), I32)
        key_scr[:, pl.ds(tile_off(kt), KEY_TILE)] = jnp.where(bits < 0, (bits ^ jnp.int32(0x7FFFFFFF)) + 1, bits)
        return carry

    lax.fori_loop(0, n_tiles, score_body, 0)

    def count(pred):
        def body(kt, acc):
            _, kpos = admissible(kt)
            hit = pred(key_scr[:, pl.ds(tile_off(kt), KEY_TILE)], kpos).astype(I32)
            return acc + (hit[:, :LANES] + hit[:, LANES:])
        acc = lax.fori_loop(0, n_tiles, body, jnp.zeros((tq, LANES), I32))
        return jnp.sum(acc, axis=1, keepdims=True)

    def thr_body(i, thr_u):
        cand_u = thr_u | lax.shift_left(jnp.int32(1), 31 - i)
        cand = cand_u ^ int_min
        cnt = count(lambda key, kpos: key >= cand)
        return jnp.where(cnt >= n_sel, cand_u, thr_u)

    thr = lax.fori_loop(0, 32, thr_body, jnp.zeros((tq, 1), I32)) ^ int_min
    need = n_sel - count(lambda key, kpos: key > thr)
    n_eq = count(lambda key, kpos: key == thr)
    has_tie = jnp.max((n_eq - need).astype(F32)) > 0.0

    nbits = int(lp).bit_length()

    def tie_cut():
        def cut_body(i, cut):
            cand = cut | lax.shift_left(jnp.int32(1), nbits - 1 - i)
            cnt = count(lambda key, kpos: (key == thr) & (kpos < cand))
            return jnp.where(cnt <= need, cand, cut)
        return lax.fori_loop(0, nbits, cut_body, jnp.zeros((tq, 1), I32))

    cut = lax.cond(has_tie, tie_cut, lambda: jnp.full((tq, 1), 2 ** 30, I32))

    def mask_body(kt, carry):
        key = key_scr[:, pl.ds(tile_off(kt), KEY_TILE)]
        adm, kpos = admissible(kt)
        sel = ((key > thr) | ((key == thr) & (kpos < cut))) & adm
        mask_scr[:, pl.ds(tile_off(kt), KEY_TILE)] = jnp.where(sel, 0.0, NEG_BIG)
        return carry

    lax.fori_loop(0, n_tiles, mask_body, 0)

    grp = A_HEADS // A_KV_HEADS
    rows = grp * tq
    head_cols = lambda h: slice(h * A_HEAD_DIM, (h + 1) * A_HEAD_DIM)
    group_rows = lambda g: slice(g * rows, (g + 1) * rows)
    qs = [jnp.concatenate([q_ref[:, head_cols(g * grp + i)] for i in range(grp)], axis=0)
          for g in range(A_KV_HEADS)]
    m_scr[...] = jnp.full(m_scr.shape, NEG_BIG, F32)

    def qk_body(kt, carry, near):
        off = tile_off(kt)
        mask = mask_scr[:, pl.ds(off, KEY_TILE)][None]
        for g in range(A_KV_HEADS):
            lg = jnp.dot(qs[g], kt_ref[head_cols(g), pl.ds(off, KEY_TILE)], preferred_element_type=F32)
            lg = lg.reshape(grp, tq, KEY_TILE) + mask
            if near:
                back = jnp.clip((q0 - kt * KEY_TILE) // LANES, 0, 2)
                lg = lg + bias_ref[back, g * grp:(g + 1) * grp, 0:tq, :]
            lg = lg.reshape(rows, KEY_TILE)
            lg_scr[group_rows(g), pl.ds(off, KEY_TILE)] = lg
            m_scr[group_rows(g), :] = jnp.maximum(m_scr[group_rows(g), :],
                                                  jnp.maximum(lg[:, :LANES], lg[:, LANES:]))
        return carry

    lax.fori_loop(0, n_far, functools.partial(qk_body, near=False), 0)
    lax.fori_loop(n_far, n_tiles, functools.partial(qk_body, near=True), 0)
    m_scr[...] = jnp.broadcast_to(jnp.max(m_scr[...], axis=-1, keepdims=True), m_scr.shape)
    l_scr[...] = jnp.zeros(l_scr.shape, F32)
    acc_scr[...] = jnp.zeros(acc_scr.shape, F32)

    def pv_body(kt, carry):
        off = tile_off(kt)
        for g in range(A_KV_HEADS):
            lg = lg_scr[group_rows(g), pl.ds(off, KEY_TILE)]
            m_b = m_scr[group_rows(g), :]
            p0 = jnp.exp(lg[:, :LANES] - m_b)
            p1 = jnp.exp(lg[:, LANES:] - m_b)
            l_scr[group_rows(g), :] += p0 + p1
            p = jnp.concatenate([p0, p1], axis=1).astype(BF16)
            acc_scr[group_rows(g), :] += jnp.dot(p, v_ref[pl.ds(off, KEY_TILE), head_cols(g)],
                                                 preferred_element_type=F32)
        return carry

    lax.fori_loop(0, n_tiles, pv_body, 0)
    out = acc_scr[...] / jnp.sum(l_scr[...], axis=-1, keepdims=True)
    for h in range(A_HEADS):
        o_ref[:, head_cols(h)] = out[h * tq:(h + 1) * tq].astype(o_ref.dtype)


def dsa_attention(q, qi, wi, k_t, v, ki_t, bias, *, tq, q_off, l_true, n_sel):
    bsz, t_len, _ = q.shape
    lp = v.shape[1]
    assert lp % KEY_TILE == 0 and t_len % tq == 0 and q_off % KEY_TILE == 0 and tq <= LANES
    assert tq == LANES or t_len == tq
    grp = A_HEADS // A_KV_HEADS
    qspec = lambda w: pl.BlockSpec((None, tq, w), lambda b, j: (b, j, 0))
    whole = lambda a: pl.BlockSpec((None,) + a.shape[1:], lambda b, j: (b, 0, 0))
    return pl.pallas_call(
        functools.partial(_dsa_kernel, tq=tq, q_off=q_off, l_true=l_true, n_sel=n_sel, lp=lp),
        grid=(bsz, t_len // tq),
        in_specs=[qspec(q.shape[2]), qspec(qi.shape[2]), qspec(wi.shape[2]),
                  whole(k_t), whole(v), whole(ki_t),
                  _resident(bias.shape, lambda b, j: (0, 0, 0, 0))],
        out_specs=qspec(q.shape[2]),
        out_shape=jax.ShapeDtypeStruct(q.shape, BF16),
        scratch_shapes=[pltpu.VMEM((tq, lp), I32), pltpu.VMEM((tq, lp), F32), pltpu.VMEM((A_HEADS * tq, lp), F32),
                        pltpu.VMEM((IDX_HEADS, tq, KEY_TILE), F32), pltpu.VMEM((A_HEADS * tq, LANES), F32),
                        pltpu.VMEM((A_HEADS * tq, LANES), F32), pltpu.VMEM((A_HEADS * tq, A_HEAD_DIM), F32)],
        compiler_params=_params("parallel", "arbitrary"),
        name="dsa_attention",
    )(q, qi, wi, k_t, v, ki_t, bias)


def _band_bias(rel_table, chunk, grp):
    win = BAND_WINDOW + grp
    t = np.arange(grp)[:, None]
    s = np.arange(win)[None, :]
    idx = np.clip(t - s + BAND_WINDOW, -BAND_MAX_REL, BAND_MAX_REL) + BAND_MAX_REL
    lo = (t // chunk) * chunk
    allowed = (s >= lo) & (s < lo + BAND_WINDOW + chunk)
    return jnp.where(allowed[None], rel_table[:, idx], NEG_BIG)


def _band_kernel(q_ref, kp_ref, kc_ref, vp_ref, vc_ref, bias_ref, o_ref, kcat, vcat, *, grp, first_has_no_past):
    i = pl.program_id(1)
    prev = kp_ref.shape[0]
    tq = q_ref.shape[0]
    win = prev + grp
    kcat[0:prev, :] = kp_ref[...]
    kcat[prev:prev + tq, :] = kc_ref[...]
    vcat[0:prev, :] = vp_ref[...]
    vcat[prev:prev + tq, :] = vc_ref[...]
    scale = BAND_HEAD_DIM ** -0.5
    lane = lax.broadcasted_iota(I32, (grp, win), 1)
    for r0 in range(0, tq, grp):
        if first_has_no_past:
            kpos_ok = (i * tq - prev + r0 + lane) >= 0
        for h in range(BAND_HEADS):
            cols = slice(h * BAND_HEAD_DIM, (h + 1) * BAND_HEAD_DIM)
            lg = lax.dot_general(q_ref[r0:r0 + grp, cols], kcat[r0:r0 + win, cols], NT_DIMS,
                                 preferred_element_type=F32) * scale + bias_ref[h]
            if first_has_no_past:
                lg = jnp.where(kpos_ok, lg, NEG_BIG)
            p = jnp.exp(lg - jnp.max(lg, axis=-1, keepdims=True))
            den = jnp.sum(p, axis=-1, keepdims=True)
            pv = jnp.dot(p.astype(BF16), vcat[r0:r0 + win, cols], preferred_element_type=F32)
            o_ref[r0:r0 + grp, cols] = (pv / den).astype(o_ref.dtype)


def band_attention_prompt(q, k, v, bias, *, tq=512, grp=128):
    bsz, length, w = q.shape
    assert tq == BAND_WINDOW and length % tq == 0
    cur = pl.BlockSpec((None, tq, w), lambda b, i: (b, i, 0))
    prv = pl.BlockSpec((None, tq, w), lambda b, i: (b, jnp.maximum(i - 1, 0), 0))
    return pl.pallas_call(
        functools.partial(_band_kernel, grp=grp, first_has_no_past=True),
        grid=(bsz, length // tq),
        in_specs=[cur, prv, cur, prv, cur, _resident(bias.shape, lambda b, i: (0, 0, 0))],
        out_specs=cur,
        out_shape=jax.ShapeDtypeStruct(q.shape, BF16),
        scratch_shapes=[pltpu.VMEM((2 * tq, w), BF16), pltpu.VMEM((2 * tq, w), BF16)],
        compiler_params=_params("parallel", "arbitrary"),
        name="band_attention",
    )(q, k, k, v, v, bias)


def band_attention_sample(q, k_past, k_new, v_past, v_new, bias):
    bsz, t_len, w = q.shape
    past = k_past.shape[1]
    assert past == BAND_WINDOW
    new = pl.BlockSpec((None, t_len, w), lambda b, i: (b, 0, 0))
    old = pl.BlockSpec((None, past, w), lambda b, i: (b, 0, 0))
    return pl.pallas_call(
        functools.partial(_band_kernel, grp=t_len, first_has_no_past=False),
        grid=(bsz, 1),
        in_specs=[new, old, new, old, new, _resident(bias.shape, lambda b, i: (0, 0, 0))],
        out_specs=new,
        out_shape=jax.ShapeDtypeStruct(q.shape, BF16),
        scratch_shapes=[pltpu.VMEM((past + t_len, w), BF16), pltpu.VMEM((past + t_len, w), BF16)],
        compiler_params=_params("parallel", "arbitrary"),
        name="band_attention_sample",
    )(q, k_past, k_new, v_past, v_new, bias)


HIST = 8
SSD_ROWS = 256
MLSTM_ROWS = 256


def _causal_conv_chunk(x_ref, hist_scr, w_ref, b_ref, tc, width):
    hist_scr[HIST:HIST + tc, :] = x_ref[...]
    base = HIST - (width - 1)
    out = b_ref[...] + hist_scr[base:base + tc, :] * w_ref[0:1, :]
    for j in range(1, width):
        out = out + hist_scr[base + j:base + j + tc, :] * w_ref[j:j + 1, :]
    tail = hist_scr[HIST + tc - (width - 1):HIST + tc, :]
    hist_scr[base:HIST, :] = tail
    return out


def _tri(tc):
    r = lax.broadcasted_iota(I32, (tc, tc), 0)
    c = lax.broadcasted_iota(I32, (tc, tc), 1)
    return r >= c


def _eye(n):
    r = lax.broadcasted_iota(I32, (n, n), 0)
    c = lax.broadcasted_iota(I32, (n, n), 1)
    return (r == c).astype(F32)


def _cumsum_rows(x, causal):
    return jnp.dot(causal.astype(F32), x, precision=lax.Precision.HIGHEST, preferred_element_type=F32)


def _transpose_f32(x):
    return lax.dot_general(_eye(x.shape[1]), x, NT_DIMS, precision=lax.Precision.HIGHEST,
                           preferred_element_type=F32)


def _ssd_kernel(xbc_ref, z_ref, dt_ref, conv0_ref, h0_ref, cw_ref, cb_ref, dtb_ref, alog_ref, dskip_ref,
                ng_ref, y_ref, hs_ref, hist_scr, y_scr, *, tc, d_inner, n_heads):
    c = pl.program_id(1)

    @pl.when(c == 0)
    def _():
        hs_ref[...] = h0_ref[...]
        hist_scr[HIST - (SSD_CONV - 1):HIST, :] = conv0_ref[...]

    xs = _silu(_causal_conv_chunk(xbc_ref, hist_scr, cw_ref, cb_ref, tc, SSD_CONV))
    gn = SSD_GROUPS * SSD_D_STATE
    hg = n_heads // SSD_GROUPS
    causal = _tri(tc)
    x_dt = dt_ref[...] + dtb_ref[...]
    dt = jnp.maximum(x_dt, 0.0) + jnp.log1p(jnp.exp(-jnp.abs(x_dt)))
    a_head = -jnp.exp(alog_ref[...])
    cum = _cumsum_rows(dt * a_head, causal)
    cum_t = _transpose_f32(cum)
    cum_last = cum[tc - 1:tc, :]
    for g in range(SSD_GROUPS):
        bm = xs[:, d_inner + g * SSD_D_STATE:d_inner + (g + 1) * SSD_D_STATE].astype(BF16)
        cm = xs[:, d_inner + gn + g * SSD_D_STATE:d_inner + gn + (g + 1) * SSD_D_STATE].astype(BF16)
        cb = lax.dot_general(cm, bm, NT_DIMS, preferred_element_type=F32)
        for hh in range(hg):
            h = g * hg + hh
            cols = slice(h * SSD_HEAD_DIM, (h + 1) * SSD_HEAD_DIM)
            col = cum[:, h:h + 1]
            seg = col - cum_t[h:h + 1, :]
            decay = jnp.exp(jnp.where(causal, seg, -jnp.inf))
            xh = xs[:, cols]
            xdt = xh * dt[:, h:h + 1]
            h0 = hs_ref[h]
            y = jnp.dot((cb * decay).astype(BF16), xdt.astype(BF16), preferred_element_type=F32)
            y = y + jnp.exp(col) * lax.dot_general(cm, h0.astype(BF16), NT_DIMS, preferred_element_type=F32)
            y_scr[:, cols] = y + dskip_ref[:, cols] * xh
            last = cum_last[:, h:h + 1]
            w = (jnp.exp(last - col) * xdt).astype(BF16)
            hs_ref[h] = jnp.exp(last) * h0 + lax.dot_general(w, bm, TN_DIMS, preferred_element_type=F32)
    yz = y_scr[...] * _silu(z_ref[...])
    gw = d_inner // SSD_GROUPS
    for g in range(SSD_GROUPS):
        cols = slice(g * gw, (g + 1) * gw)
        seg = yz[:, cols]
        ms = jnp.mean(seg * seg, axis=-1, keepdims=True)
        y_ref[:, cols] = (seg * lax.rsqrt(ms + NORM_EPS) * ng_ref[:, cols]).astype(y_ref.dtype)


def ssd_scan(xbc, z, dt, conv0, h0, conv_w, conv_b, dt_bias, a_log, d_skip_cols, norm_g, *, tc):
    bsz, length, cc = xbc.shape
    d_inner = z.shape[2]
    n_heads = dt.shape[2]
    assert length % tc == 0
    seq = lambda w: pl.BlockSpec((None, tc, w), lambda b, c: (b, c, 0))
    per_b3 = lambda s: pl.BlockSpec((None,) + s, lambda b, c: (b,) + (0,) * len(s))
    row = lambda w: _resident((1, w), lambda b, c: (0, 0))
    return pl.pallas_call(
        functools.partial(_ssd_kernel, tc=tc, d_inner=d_inner, n_heads=n_heads),
        grid=(bsz, length // tc),
        in_specs=[seq(cc), seq(d_inner), seq(n_heads), per_b3(conv0.shape[1:]), per_b3(h0.shape[1:]),
                  _resident(conv_w.shape, lambda b, c: (0, 0)), row(cc), row(n_heads), row(n_heads),
                  row(d_inner), row(d_inner)],
        out_specs=[seq(d_inner), per_b3(h0.shape[1:])],
        out_shape=[jax.ShapeDtypeStruct((bsz, length, d_inner), BF16), jax.ShapeDtypeStruct(h0.shape, F32)],
        scratch_shapes=[pltpu.VMEM((HIST + tc, cc), F32), pltpu.VMEM((tc, d_inner), F32)],
        compiler_params=_params("parallel", "arbitrary"),
        name="ssd_scan",
    )(xbc, z, dt, conv0, h0, conv_w, conv_b, dt_bias, a_log, d_skip_cols, norm_g)


def _mlstm_kernel(xc_ref, v_ref, o_ref, gates_ref, conv0_ref, c0_ref, n0_ref, m0_ref, cw_ref, cb_ref, wq_ref,
                  wk_ref, gb_ref, ng_ref, h_ref, cs_ref, ns_ref, ms_ref, hist_scr, *, tc, d_inner):
    c = pl.program_id(1)

    @pl.when(c == 0)
    def _():
        cs_ref[...] = c0_ref[...]
        ns_ref[...] = n0_ref[...]
        ms_ref[...] = m0_ref[...]
        hist_scr[HIST - (MLSTM_CONV - 1):HIST, :] = conv0_ref[...]

    xa = _silu(_causal_conv_chunk(xc_ref, hist_scr, cw_ref, cb_ref, tc, MLSTM_CONV)).astype(BF16)
    blk = wq_ref.shape[1]
    dh = d_inner // MLSTM_HEADS
    q = jnp.concatenate([jnp.dot(xa[:, j * blk:(j + 1) * blk], wq_ref[j], preferred_element_type=F32)
                         for j in range(d_inner // blk)], axis=1)
    k = jnp.concatenate([jnp.dot(xa[:, j * blk:(j + 1) * blk], wk_ref[j], preferred_element_type=F32)
                         for j in range(d_inner // blk)], axis=1) * dh ** -0.5
    causal = _tri(tc)
    gates = gates_ref[...] + gb_ref[...]
    log_f = jnp.minimum(gates, 0.0) - jnp.log1p(jnp.exp(-jnp.abs(gates)))
    f_cum = _cumsum_rows(log_f, causal)
    f_cum_t = _transpose_f32(f_cum)
    gates_t = _transpose_f32(gates)
    for hh in range(MLSTM_HEADS):
        cols = slice(hh * dh, (hh + 1) * dh)
        fh = MLSTM_HEADS + hh
        qh = q[:, cols].astype(BF16)
        kh = k[:, cols]
        vh = v_ref[:, cols]
        fc = f_cum[:, fh:fh + 1]
        i_col = gates[:, hh:hh + 1]
        m0 = ms_ref[0:1, hh:hh + 1]
        d_log = jnp.where(causal, fc - f_cum_t[fh:fh + 1, :] + gates_t[hh:hh + 1, :], -jnp.inf)
        inter = fc + m0
        m = jnp.maximum(jnp.max(d_log, axis=-1, keepdims=True), inter)
        s = lax.dot_general(qh, kh.astype(BF16), NT_DIMS, preferred_element_type=F32) * jnp.exp(d_log - m)
        w_inter = jnp.exp(inter - m)
        c0 = cs_ref[hh]
        n0 = ns_ref[hh:hh + 1, :]
        num = (jnp.dot(s.astype(BF16), vh, preferred_element_type=F32)
               + w_inter * jnp.dot(qh, c0.astype(BF16), preferred_element_type=F32))
        den = (jnp.sum(s, axis=-1, keepdims=True)
               + w_inter * jnp.sum(q[:, cols] * n0, axis=-1, keepdims=True))
        h = num / jnp.maximum(jnp.abs(den), jnp.exp(-m))
        m_end = m[tc - 1:tc, :]
        f_last = fc[tc - 1:tc, :]
        w_end = jnp.exp(f_last - fc + i_col - m_end)
        decay = jnp.exp(f_last + m0 - m_end)
        wk = w_end * kh
        cs_ref[hh] = decay * c0 + lax.dot_general(wk.astype(BF16), vh, TN_DIMS, preferred_element_type=F32)
        ns_ref[hh:hh + 1, :] = decay * n0 + jnp.sum(wk, axis=0, keepdims=True)
        ms_ref[0:1, hh:hh + 1] = m_end
        h = jax.nn.sigmoid(o_ref[:, cols]) * h
        h = h - jnp.mean(h, axis=-1, keepdims=True)
        h = h * lax.rsqrt(jnp.mean(h * h, axis=-1, keepdims=True) + NORM_EPS)
        h_ref[:, cols] = (h * ng_ref[:, cols]).astype(h_ref.dtype)


def mlstm_scan(xc, v, o, gates, conv0, c0, n0, m0, conv_w, conv_b, wq_bd, wk_bd, gate_b, norm_g, *, tc):
    bsz, length, d_inner = xc.shape
    assert length % tc == 0
    seq = lambda w: pl.BlockSpec((None, tc, w), lambda b, c: (b, c, 0))
    per_b = lambda s: pl.BlockSpec((None,) + s, lambda b, c: (b,) + (0,) * len(s))
    row = lambda w: _resident((1, w), lambda b, c: (0, 0))
    return pl.pallas_call(
        functools.partial(_mlstm_kernel, tc=tc, d_inner=d_inner),
        grid=(bsz, length // tc),
        in_specs=[seq(d_inner), seq(d_inner), seq(d_inner), seq(gates.shape[2]), per_b(conv0.shape[1:]),
                  per_b(c0.shape[1:]), per_b(n0.shape[1:]), per_b(m0.shape[1:]),
                  _resident(conv_w.shape, lambda b, c: (0, 0)), row(d_inner),
                  _resident(wq_bd.shape, lambda b, c: (0, 0, 0)), _resident(wk_bd.shape, lambda b, c: (0, 0, 0)),
                  row(gates.shape[2]), row(d_inner)],
        out_specs=[seq(d_inner), per_b(c0.shape[1:]), per_b(n0.shape[1:]), per_b(m0.shape[1:])],
        out_shape=[jax.ShapeDtypeStruct((bsz, length, d_inner), BF16), jax.ShapeDtypeStruct(c0.shape, F32),
                   jax.ShapeDtypeStruct(n0.shape, F32), jax.ShapeDtypeStruct(m0.shape, F32)],
        scratch_shapes=[pltpu.VMEM((HIST + tc, d_inner), F32)],
        compiler_params=_params("parallel", "arbitrary"),
        name="mlstm_scan",
    )(xc, v, o, gates, conv0, c0, n0, m0, conv_w, conv_b, wq_bd, wk_bd, gate_b, norm_g)


def _block_diag(w_blk, blk):
    n, c, d = w_blk.shape
    per = blk // c
    eye = jnp.eye(per, dtype=w_blk.dtype)
    tiles = w_blk.reshape(n // per, per, c, d)
    return jnp.einsum("jpcd,pq->jpcqd", tiles, eye).reshape(n // per, per * c, per * d)


def _mixer_a(xp, xs, shapes, cache_k, cache_v, cache_ki, w_in, t5_table):
    (bp, lp_), (bs, ls) = shapes
    hd, kvd, idd = A_HEADS * A_HEAD_DIM, A_KV_HEADS * A_HEAD_DIM, IDX_HEADS * IDX_DIM
    w_pad, cols = _pad_cols(w_in, (hd, kvd, kvd, idd, IDX_DIM, IDX_HEADS))
    outs = [(cols[0], BF16, A_HEAD_DIM ** -0.5), (cols[1], F32), (cols[1], BF16), (cols[2], F32), (cols[2], BF16),
            (cols[3], BF16), (cols[4], F32), (cols[4], BF16), (cols[5], F32)]
    w_pad = w_pad.astype(BF16)
    bias = _t5_bias_tiles(t5_table)
    keys_on_lanes = lambda a: jnp.swapaxes(a, 1, 2)

    q, k, kb, v, vb, qi, ki, kib, wi = project(xp, w_pad, outs, name="project_a")
    r3 = lambda a, b_, l: a.reshape(b_, l, a.shape[-1])
    att_p = dsa_attention(r3(q, bp, lp_), r3(qi, bp, lp_), r3(wi, bp, lp_), keys_on_lanes(r3(kb, bp, lp_)),
                          r3(vb, bp, lp_), keys_on_lanes(r3(kib, bp, lp_)), bias, tq=Q_BLOCK, q_off=0, l_true=lp_,
                          n_sel=min(IDX_TOPK_MAX, lp_ // 4))
    outs_p = (k.reshape(bp, lp_, A_KV_HEADS, A_HEAD_DIM), v.reshape(bp, lp_, A_KV_HEADS, A_HEAD_DIM),
              ki.reshape(bp, lp_, IDX_DIM))

    q, k, kb, v, vb, qi, ki, kib, wi = project(xs, w_pad, outs, name="project_a")
    past = cache_k.shape[1]
    total = past + ls
    lpad = -(-total // KEY_TILE) * KEY_TILE

    def with_past(cache, new):
        parts = [cache.reshape(bs, past, -1).astype(BF16), r3(new, bs, ls)]
        if lpad != total:
            parts.append(jnp.zeros((bs, lpad - total, new.shape[-1]), BF16))
        return jnp.concatenate(parts, axis=1)

    att_s = dsa_attention(r3(q, bs, ls), r3(qi, bs, ls), r3(wi, bs, ls), keys_on_lanes(with_past(cache_k, kb)),
                          with_past(cache_v, vb), keys_on_lanes(with_past(cache_ki, kib)), bias, tq=ls, q_off=past,
                          l_true=total, n_sel=min(IDX_TOPK_MAX, total // 4))
    outs_s = (k.reshape(bs, ls, A_KV_HEADS, A_HEAD_DIM), v.reshape(bs, ls, A_KV_HEADS, A_HEAD_DIM),
              ki.reshape(bs, ls, IDX_DIM))
    return att_p.reshape(bp * lp_, hd), att_s.reshape(bs * ls, hd), outs_p, outs_s


def _mixer_b(xp, xs, shapes, cache_k, cache_v, w_in, rel_table):
    (bp, lp_), (bs, ls) = shapes
    hd = BAND_HEADS * BAND_HEAD_DIM
    w_pad, cols = _pad_cols(w_in, (hd, hd, hd))
    outs = [(cols[0], BF16), (cols[1], F32), (cols[1], BF16), (cols[2], F32), (cols[2], BF16)]
    w_pad = w_pad.astype(BF16)

    q, k, kb, v, vb = project(xp, w_pad, outs, name="project_b")
    r3 = lambda a, b_, l: a.reshape(b_, l, hd)
    att_p = band_attention_prompt(r3(q, bp, lp_), r3(kb, bp, lp_), r3(vb, bp, lp_),
                                  _band_bias(rel_table, CHUNK, 2 * CHUNK), grp=2 * CHUNK)
    keep = min(BAND_WINDOW, lp_)
    heads = lambda a, b_, l: a.reshape(b_, l, BAND_HEADS, BAND_HEAD_DIM)
    outs_p = (heads(k, bp, lp_)[:, lp_ - keep:], heads(v, bp, lp_)[:, lp_ - keep:])

    q, k, kb, v, vb = project(xs, w_pad, outs, name="project_b")
    past = cache_k.shape[1]
    att_s = band_attention_sample(r3(q, bs, ls), cache_k.reshape(bs, past, hd).astype(BF16), r3(kb, bs, ls),
                                  cache_v.reshape(bs, past, hd).astype(BF16), r3(vb, bs, ls),
                                  _band_bias(rel_table, ls, ls))
    outs_s = (jnp.concatenate([cache_k, heads(k, bs, ls)], axis=1)[:, ls:],
              jnp.concatenate([cache_v, heads(v, bs, ls)], axis=1)[:, ls:])
    return att_p.reshape(bp * lp_, hd), att_s.reshape(bs * ls, hd), outs_p, outs_s


def _mixer_c(xp, xs, shapes, ssm0, conv0, w_in, conv_w, conv_b, dt_bias, a_log, d_skip, norm_g):
    n_heads = a_log.shape[0]
    d_inner = n_heads * SSD_HEAD_DIM
    conv_dim = conv_w.shape[1]
    w_pad, cols = _pad_cols(w_in, (d_inner, conv_dim, n_heads))
    outs = [(cols[0], F32), (cols[1], F32), (cols[2], F32)]
    w_pad = w_pad.astype(BF16)
    row = lambda a: a.reshape(1, -1)
    d_skip_cols = jnp.repeat(d_skip, SSD_HEAD_DIM).reshape(1, d_inner)
    res = []
    for x, (b_, l), h0, c0 in ((xp, shapes[0], None, None), (xs, shapes[1], ssm0, conv0)):
        if h0 is None:
            h0 = jnp.zeros((b_, n_heads, SSD_HEAD_DIM, SSD_D_STATE), F32)
            c0 = jnp.zeros((b_, SSD_CONV - 1, conv_dim), F32)
        z, xbc, dt = project(x, w_pad, outs, name="project_c")
        xbc3 = xbc.reshape(b_, l, conv_dim)
        y, h_new = ssd_scan(xbc3, z.reshape(b_, l, d_inner), dt.reshape(b_, l, n_heads), c0, h0, conv_w,
                            row(conv_b), row(dt_bias), row(a_log), d_skip_cols, row(norm_g), tc=min(SSD_ROWS, l))
        conv_new = jnp.concatenate([c0, xbc3], axis=1)[:, l:]
        res.append((y.reshape(b_ * l, d_inner), (h_new, conv_new)))
    return res[0][0], res[1][0], res[0][1], res[1][1]


def _mixer_d(xp, xs, shapes, c0s, n0s, m0s, conv0s, w_in, conv_w, conv_b, wq_blk, wk_blk, gate_b, norm_g):
    d_inner = conv_w.shape[1]
    dh = d_inner // MLSTM_HEADS
    w_pad, cols = _pad_cols(w_in, (d_inner, d_inner, d_inner, 2 * MLSTM_HEADS))
    gcol = (cols[3][0], cols[3][1], cols[3][1])
    outs = [(cols[0], F32), (cols[1], BF16), (cols[2], F32), (gcol, F32)]
    w_pad = w_pad.astype(BF16)
    blk = 2 * LANES
    wq_bd = _block_diag(wq_blk, blk).astype(BF16)
    wk_bd = _block_diag(wk_blk, blk).astype(BF16)
    gate_b_pad = jnp.zeros((1, gcol[1]), F32).at[0, :2 * MLSTM_HEADS].set(gate_b)
    row = lambda a: a.reshape(1, -1)
    res = []
    for x, (b_, l), st in ((xp, shapes[0], None), (xs, shapes[1], (c0s, n0s, m0s, conv0s))):
        if st is None:
            st = (jnp.zeros((b_, MLSTM_HEADS, dh, dh), F32), jnp.zeros((b_, MLSTM_HEADS, dh), F32),
                  jnp.zeros((b_, MLSTM_HEADS), F32), jnp.zeros((b_, MLSTM_CONV - 1, d_inner), F32))
        c0, n0, m0, conv0 = st
        xc, v, o, gates = project(x, w_pad, outs, name="project_d")
        r3 = lambda a: a.reshape(b_, l, a.shape[-1])
        h, c_new, n_new, m_new = mlstm_scan(r3(xc), r3(v), r3(o), r3(gates), conv0, c0, n0,
                                            m0.reshape(b_, 1, MLSTM_HEADS), conv_w, row(conv_b), wq_bd, wk_bd,
                                            gate_b_pad, row(norm_g), tc=min(MLSTM_ROWS, l))
        conv_new = jnp.concatenate([conv0, r3(xc)], axis=1)[:, l:]
        res.append((h.reshape(b_ * l, d_inner), (c_new, n_new, m_new.reshape(b_, MLSTM_HEADS), conv_new)))
    return res[0][0], res[1][0], res[0][1], res[1][1]


def kernel(x_prompt, x_sample, cache_a_k, cache_a_v, cache_a_kidx, cache_b_k, cache_b_v, state_c_ssm, state_c_conv, state_d_c, state_d_n, state_d_m, state_d_conv, a_w_in, a_w_out, t5_table, b_w_in, b_w_out, b_rel_table, c_w_in, c_conv_w, c_conv_b, c_dt_bias, c_a_log, c_d_skip, c_norm_g, c_w_out, d_w_in, d_conv_w, d_conv_b, d_wq_blk, d_wk_blk, d_gate_b, d_norm_g, d_w_out, ffn1_wg, ffn1_wu, ffn1_wd, ffn2_wg, ffn2_wu, ffn2_wd, ln_g, ln_b):
    bp, lp_, d = x_prompt.shape
    bs, ls, _ = x_sample.shape
    depth = ffn1_wg.shape[0]
    alpha = (2.0 * depth) ** 0.25
    shapes = ((bp, lp_), (bs, ls))
    xp = x_prompt.reshape(bp * lp_, d)
    xs = x_sample.reshape(bs * ls, d)
    ffn_w = [[w.astype(BF16) for w in ws] for ws in ((ffn1_wg, ffn1_wu, ffn1_wd), (ffn2_wg, ffn2_wu, ffn2_wd))]
    w_out = [w.astype(BF16) for w in (a_w_out, b_w_out, c_w_out, d_w_out)]
    row = lambda a: a.reshape(1, -1)
    extra = {}
    for i in range(depth):
        g, b = ln_g[i], ln_b[i]
        xp = ffn_postnorm(xp, *ffn_w[0], i, row(g[0]), row(b[0]), alpha)
        xs = ffn_postnorm(xs, *ffn_w[0], i, row(g[0]), row(b[0]), alpha)
        kind = i % 4
        if kind == 0:
            mp, ms, op, os_ = _mixer_a(xp, xs, shapes, cache_a_k, cache_a_v, cache_a_kidx, a_w_in, t5_table)
        elif kind == 1:
            mp, ms, op, os_ = _mixer_b(xp, xs, shapes, cache_b_k, cache_b_v, b_w_in, b_rel_table)
        elif kind == 2:
            mp, ms, op, os_ = _mixer_c(xp, xs, shapes, state_c_ssm, state_c_conv, c_w_in, c_conv_w, c_conv_b,
                                       c_dt_bias, c_a_log, c_d_skip, c_norm_g)
        else:
            mp, ms, op, os_ = _mixer_d(xp, xs, shapes, state_d_c, state_d_n, state_d_m, state_d_conv, d_w_in,
                                       d_conv_w, d_conv_b, d_wq_blk, d_wk_blk, d_gate_b, d_norm_g)
        extra[kind] = (op, os_)
        xp = proj_postnorm(xp, mp, w_out[kind], row(g[1]), row(b[1]), alpha)
        xs = proj_postnorm(xs, ms, w_out[kind], row(g[1]), row(b[1]), alpha)
        xp = ffn_postnorm(xp, *ffn_w[1], i, row(g[2]), row(b[2]), alpha)
        xs = ffn_postnorm(xs, *ffn_w[1], i, row(g[2]), row(b[2]), alpha)
    prompt_side = tuple(t for kind in range(4) for t in extra[kind][0])
    sample_side = tuple(t for kind in range(4) for t in extra[kind][1])
    return (xp.reshape(bp, lp_, d), xs.reshape(bs, ls, d)) + prompt_side + sample_side
```

```python
import functools
import math

import numpy as np
import jax
import jax.numpy as jnp
from jax import lax
from jax.experimental import pallas as pl
from jax.experimental.pallas import tpu as pltpu

F32 = jnp.float32
BF16 = jnp.bfloat16
I32 = jnp.int32

CHUNK = 64
NORM_EPS = 1e-5
A_HEADS, A_KV_HEADS, A_HEAD_DIM = 8, 2, 128
IDX_HEADS, IDX_DIM, IDX_TOPK_MAX = 8, 64, 256
T5_BUCKETS, T5_MAX_DIST = 32, 128
BAND_HEADS, BAND_HEAD_DIM, BAND_LEFT_CHUNKS, BAND_MAX_REL = 16, 64, 8, 128
BAND_WINDOW = BAND_LEFT_CHUNKS * CHUNK
SSD_HEAD_DIM, SSD_GROUPS, SSD_D_STATE, SSD_CONV = 64, 8, 128, 4
MLSTM_HEADS, MLSTM_CONV, MLSTM_QK_BLOCK = 4, 4, 4

LANES = 128
NEG_BIG = -1e30
VMEM_LIMIT = 56 * 1024 * 1024

NT_DIMS = (((1,), (1,)), ((), ()))
TN_DIMS = (((0,), (0,)), ((), ()))


def _params(*sem):
    return pltpu.CompilerParams(dimension_semantics=sem, vmem_limit_bytes=VMEM_LIMIT)


def _resident(shape, index_map):
    return pl.BlockSpec(shape, index_map, pipeline_mode=pl.Buffered(1))


def _layer_norm(y, g, b):
    mu = jnp.mean(y, axis=-1, keepdims=True)
    yc = y - mu
    var = jnp.mean(yc * yc, axis=-1, keepdims=True)
    return yc * lax.rsqrt(var + NORM_EPS) * g + b


def _silu(x):
    return x * jax.nn.sigmoid(x)


def _row_tile(n, want):
    t = min(n, want)
    assert n % t == 0, (n, t)
    return t


def _ffn_kernel(x_ref, wg_ref, wu_ref, wd_ref, g_ref, b_ref, o_ref, *, alpha, f_cuts):
    x = x_ref[...]
    xb = x.astype(BF16)
    acc = jnp.zeros(x.shape, F32)
    for lo, hi in zip(f_cuts[:-1], f_cuts[1:]):
        sl = slice(lo, hi)
        gate = jnp.dot(xb, wg_ref[:, sl], preferred_element_type=F32)
        up = jnp.dot(xb, wu_ref[:, sl], preferred_element_type=F32)
        h = (_silu(gate) * up).astype(BF16)
        acc = acc + jnp.dot(h, wd_ref[sl, :], preferred_element_type=F32)
    o_ref[...] = _layer_norm(alpha * x + 0.5 * acc, g_ref[...], b_ref[...])


MXU_WIDTH = 256


def _ffn_cuts(d_ff, n_chunks):
    tiles = -(-d_ff // MXU_WIDTH)
    cuts = [min(d_ff, MXU_WIDTH * (-(-tiles * c // n_chunks))) for c in range(n_chunks + 1)]
    return tuple(cuts)


def ffn_postnorm(x, wg, wu, wd, layer, g, b, alpha, tm=512, n_chunks=2):
    n, d = x.shape
    d_ff = wg.shape[2]
    tm = _row_tile(n, tm)
    return pl.pallas_call(
        functools.partial(_ffn_kernel, alpha=alpha, f_cuts=_ffn_cuts(d_ff, n_chunks)),
        grid=(n // tm,),
        in_specs=[
            pl.BlockSpec((tm, d), lambda i: (i, 0)),
            _resident((None, d, d_ff), lambda i: (layer, 0, 0)),
            _resident((None, d, d_ff), lambda i: (layer, 0, 0)),
            _resident((None, d_ff, d), lambda i: (layer, 0, 0)),
            _resident((1, d), lambda i: (0, 0)),
            _resident((1, d), lambda i: (0, 0)),
        ],
        out_specs=pl.BlockSpec((tm, d), lambda i: (i, 0)),
        out_shape=jax.ShapeDtypeStruct((n, d), F32),
        compiler_params=_params("parallel"),
        name="ffn_postnorm",
    )(x, wg, wu, wd, g, b)


def _proj_kernel(x_ref, w_ref, *o_refs, cols, scales):
    xb = x_ref[...].astype(BF16)
    done = {}
    for (off, pad_w, true_w), scale, o_ref in zip(cols, scales, o_refs):
        if (off, pad_w) not in done:
            done[(off, pad_w)] = jnp.dot(xb, w_ref[:, off:off + pad_w], preferred_element_type=F32)
        y = done[(off, pad_w)][:, :true_w]
        o_ref[...] = (y if scale is None else y * scale).astype(o_ref.dtype)


def _pad_cols(w, widths):
    pieces, offs, off, src = [], [], 0, 0
    for wd in widths:
        pad_w = -(-wd // LANES) * LANES
        pieces.append(w[:, src:src + wd])
        if pad_w != wd:
            pieces.append(jnp.zeros((w.shape[0], pad_w - wd), w.dtype))
        offs.append((off, pad_w, wd))
        off += pad_w
        src += wd
    return jnp.concatenate(pieces, axis=1), offs


def project(x, w_pad, outs, tm=512, name="project"):
    n, d = x.shape
    tm = _row_tile(n, tm)
    cols = tuple(o[0] for o in outs)
    scales = tuple(o[2] if len(o) > 2 else None for o in outs)
    outs = [o[:2] for o in outs]
    return pl.pallas_call(
        functools.partial(_proj_kernel, cols=cols, scales=scales),
        grid=(n // tm,),
        in_specs=[pl.BlockSpec((tm, d), lambda i: (i, 0)),
                  _resident(w_pad.shape, lambda i: (0, 0))],
        out_specs=[pl.BlockSpec((tm, c[2]), lambda i: (i, 0)) for c in cols],
        out_shape=[jax.ShapeDtypeStruct((n, c[2]), dt) for c, dt in outs],
        compiler_params=_params("parallel"),
        name=name,
    )(x, w_pad)


def _proj_postnorm_kernel(x_ref, m_ref, w_ref, g_ref, b_ref, o_ref, *, alpha):
    sub = jnp.dot(m_ref[...], w_ref[...], preferred_element_type=F32)
    o_ref[...] = _layer_norm(alpha * x_ref[...] + sub, g_ref[...], b_ref[...])


def proj_postnorm(x, m, w, g, b, alpha, tm=512):
    n, d = x.shape
    k = m.shape[1]
    tm = _row_tile(n, tm)
    return pl.pallas_call(
        functools.partial(_proj_postnorm_kernel, alpha=alpha),
        grid=(n // tm,),
        in_specs=[pl.BlockSpec((tm, d), lambda i: (i, 0)),
                  pl.BlockSpec((tm, k), lambda i: (i, 0)),
                  _resident((k, d), lambda i: (0, 0)),
                  _resident((1, d), lambda i: (0, 0)),
                  _resident((1, d), lambda i: (0, 0))],
        out_specs=pl.BlockSpec((tm, d), lambda i: (i, 0)),
        out_shape=jax.ShapeDtypeStruct((n, d), F32),
        compiler_params=_params("parallel"),
        name="proj_postnorm",
    )(x, m, w, g, b)


def _t5_bucket_np(rel):
    half = T5_BUCKETS // 2
    max_exact = half // 2
    n = np.abs(rel)
    nf = np.maximum(n, 1).astype(np.float32)
    large = max_exact + (np.log(nf / np.float32(max_exact)) / np.float32(math.log(T5_MAX_DIST / max_exact))
                         * np.float32(half - max_exact)).astype(np.int32)
    large = np.minimum(large, half - 1)
    return np.where(rel > 0, half, 0) + np.where(n < max_exact, n, large)


KEY_TILE = 2 * LANES


def _t5_bias_tiles(t5_table):
    t = np.arange(KEY_TILE)[:, None]
    s = np.arange(KEY_TILE)[None, :]
    rel = np.stack([s - t - o * KEY_TILE for o in range(2)])
    assert T5_MAX_DIST <= KEY_TILE
    near = t5_table[_t5_bucket_np(rel)]
    far = t5_table[int(_t5_bucket_np(np.array(-T5_MAX_DIST)))]
    return jnp.transpose(near - far, (0, 3, 1, 2))


def _dsa_kernel(q_ref, qi_ref, wi_ref, kt_ref, v_ref, kit_ref, bias_ref, o_ref,
                key_scr, mask_scr, lg_scr, wib_scr, m_scr, l_scr, acc_scr, *, tq, q_off, l_true, n_sel, lp):
    j = pl.program_id(1)
    q0 = q_off + j * tq
    kend = jnp.minimum(((q0 + tq - 1) // CHUNK + 1) * CHUNK, l_true)
    n_tiles = (kend + KEY_TILE - 1) // KEY_TILE
    n_far = jnp.maximum(q0 // KEY_TILE - 1, 0)
    qpos = q0 + lax.broadcasted_iota(I32, (tq, LANES), 0)
    lane = lax.broadcasted_iota(I32, (tq, LANES), 1)
    q_chunk = qpos // CHUNK
    int_min = jnp.int32(-2 ** 31)
    wide = 2 * KEY_TILE

    def tile_off(kt):
        return pl.multiple_of(kt * KEY_TILE, KEY_TILE)

    def lane_blocks(off, width):
        return [pl.multiple_of(off + u * LANES, LANES) for u in range(width // LANES)]

    def admissible(kpos):
        return ((kpos // CHUNK) <= q_chunk) & (kpos < l_true)

    def slabs(lo, hi, fn):
        def pair(i, carry):
            fn(tile_off(lo + 2 * i), wide)
            return carry
        lax.fori_loop(0, (hi - lo) // 2, pair, 0)

        @pl.when((hi - lo) % 2 == 1)
        def _():
            fn(tile_off(hi - 1), KEY_TILE)

    wi = wi_ref[...] * (IDX_HEADS ** -0.5) * (IDX_DIM ** -0.5)
    for h in range(IDX_HEADS):
        wib_scr[h] = jnp.broadcast_to(wi[:, h:h + 1], (tq, LANES))

    def to_key(x):
        bits = lax.bitcast_convert_type(x, I32)
        return jnp.where(bits < 0, (bits ^ jnp.int32(0x7FFFFFFF)) + 1, bits)

    def score_slab(off, width):
        ki_t = kit_ref[:, pl.ds(off, width)]
        accs = [jnp.zeros((tq, width), F32)] * 2
        for h in range(IDX_HEADS):
            s = jnp.dot(qi_ref[:, h * IDX_DIM:(h + 1) * IDX_DIM], ki_t, preferred_element_type=F32)
            w = jnp.concatenate([wib_scr[h]] * (width // LANES), axis=1)
            accs[h % 2] = accs[h % 2] + jnp.maximum(s, 0.0) * w
        acc = accs[0] + accs[1]
        for u, o in enumerate(lane_blocks(off, width)):
            sc = jnp.where(admissible(o + lane), acc[:, u * LANES:(u + 1) * LANES], -jnp.inf)
            key_scr[:, pl.ds(o, LANES)] = to_key(sc)

    slabs(0, n_tiles, score_slab)

    @pl.when(n_tiles % 2 == 1)
    def _():
        key_scr[:, pl.ds(tile_off(n_tiles), KEY_TILE)] = to_key(jnp.full((tq, KEY_TILE), -jnp.inf, F32))

    ones = jnp.ones((LANES, LANES), BF16)

    def count(pred):
        half = min(tq, LANES)
        lane_half = lax.broadcasted_iota(I32, (half, LANES), 1)

        def count_rows(r0):
            rs = slice(r0, r0 + half)

            def body(ct, acc):
                off = pl.multiple_of(ct * wide, wide)
                keys = key_scr[rs, pl.ds(off, wide)]
                for u in range(wide // LANES):
                    hit = pred(keys[:, u * LANES:(u + 1) * LANES], off + u * LANES + lane_half, rs)
                    acc = acc + hit.astype(I32)
                return acc
            return lax.fori_loop(0, (n_tiles + 1) // 2, body, jnp.zeros((half, LANES), I32))

        acc = jnp.concatenate([count_rows(r0) for r0 in range(0, tq, half)], axis=0)
        return jnp.dot(acc.astype(F32).astype(BF16), ones, preferred_element_type=F32)

    check_every = 4

    def search_body(state):
        i, thr_u, settled, _ = state

        def one_bit(b, st):
            thr_u, settled = st
            cand_u = thr_u | lax.shift_left(jnp.int32(1), 31 - (i + b))
            cand = cand_u ^ int_min
            cnt = count(lambda key, kpos, rs: key >= cand[rs])
            thr_u = jnp.where((cnt >= n_sel) & (settled == 0), cand_u, thr_u)
            return thr_u, jnp.where(cnt == n_sel, 1, settled)

        thr_u, settled = lax.fori_loop(0, check_every, one_bit, (thr_u, settled))
        return i + check_every, thr_u, settled, jnp.min(settled.astype(F32))

    zeros = jnp.zeros((tq, LANES), I32)
    _, thr_u, _, _ = lax.while_loop(lambda s: (s[0] < 32) & (s[3] == 0.0), search_body,
                                    (jnp.int32(0), zeros, zeros, jnp.float32(0.0)))
    thr = thr_u ^ int_min
    need = n_sel - count(lambda key, kpos, rs: key > thr[rs])
    n_eq = count(lambda key, kpos, rs: key == thr[rs])
    has_tie = jnp.max(n_eq - need) > 0.0

    nbits = int(lp).bit_length()

    def tie_cut():
        def cut_body(i, cut):
            cand = cut | lax.shift_left(jnp.int32(1), nbits - 1 - i)
            cnt = count(lambda key, kpos, rs: (key == thr[rs]) & (kpos < cand[rs]))
            return jnp.where(cnt <= need, cand, cut)
        return lax.fori_loop(0, nbits, cut_body, jnp.zeros((tq, LANES), I32))

    cut = lax.cond(has_tie, tie_cut, lambda: jnp.full((tq, LANES), 2 ** 30, I32))

    def mask_slab(off, width):
        for o in lane_blocks(off, width):
            key = key_scr[:, pl.ds(o, LANES)]
            kpos = o + lane
            sel = ((key > thr) | ((key == thr) & (kpos < cut))) & admissible(kpos)
            mask_scr[:, pl.ds(o, LANES)] = jnp.where(sel, 0.0, NEG_BIG)

    slabs(0, n_tiles, mask_slab)

    grp = A_HEADS // A_KV_HEADS
    rows = grp * tq
    head_cols = lambda h: slice(h * A_HEAD_DIM, (h + 1) * A_HEAD_DIM)
    for g in range(A_KV_HEADS):
        qg = jnp.concatenate([q_ref[:, head_cols(g * grp + i)] for i in range(grp)], axis=0)
        m_scr[...] = jnp.full(m_scr.shape, NEG_BIG, F32)

        def qk_slab(off, width, near=False, qg=qg, g=g):
            lg = jnp.dot(qg, kt_ref[head_cols(g), pl.ds(off, width)], preferred_element_type=F32)
            lg = lg.reshape(grp, tq, width) + mask_scr[:, pl.ds(off, width)][None]
            if near:
                back = jnp.clip((q0 - off) // KEY_TILE, 0, 1)
                lg = lg + bias_ref[back, g * grp:(g + 1) * grp, 0:tq, :]
            lg = lg.reshape(rows, width)
            lg_scr[:, pl.ds(off, width)] = lg
            mx = lg[:, 0:LANES]
            for u in range(1, width // LANES):
                mx = jnp.maximum(mx, lg[:, u * LANES:(u + 1) * LANES])
            m_scr[...] = jnp.maximum(m_scr[...], mx)

        def near_body(kt, carry, qk_slab=qk_slab):
            qk_slab(tile_off(kt), KEY_TILE, near=True)
            return carry

        slabs(0, n_far, qk_slab)
        lax.fori_loop(n_far, n_tiles, near_body, 0)
        m_scr[...] = jnp.broadcast_to(jnp.max(m_scr[...], axis=-1, keepdims=True), m_scr.shape)
        l_scr[...] = jnp.zeros(l_scr.shape, F32)
        acc_scr[...] = jnp.zeros(acc_scr.shape, F32)

        def pv_slab(off, width, g=g):
            lg = lg_scr[:, pl.ds(off, width)]
            m_b = m_scr[...]
            ps = [jnp.exp(lg[:, u * LANES:(u + 1) * LANES] - m_b) for u in range(width // LANES)]
            l_scr[...] += functools.reduce(lambda a, b: a + b, ps)
            p = jnp.concatenate(ps, axis=1).astype(BF16)
            acc_scr[...] += jnp.dot(p, v_ref[pl.ds(off, width), head_cols(g)], preferred_element_type=F32)

        slabs(0, n_tiles, pv_slab)
        out = acc_scr[...] / jnp.sum(l_scr[...], axis=-1, keepdims=True)
        for i in range(grp):
            o_ref[:, head_cols(g * grp + i)] = out[i * tq:(i + 1) * tq].astype(o_ref.dtype)


def dsa_attention(q, qi, wi, k_t, v, ki_t, bias, *, tq, q_off, l_true, n_sel):
    bsz, t_len, _ = q.shape
    lp = v.shape[1]
    assert lp % (2 * KEY_TILE) == 0 and t_len % tq == 0 and q_off % KEY_TILE == 0
    assert tq == KEY_TILE or (t_len == tq and tq < KEY_TILE)
    grp = A_HEADS // A_KV_HEADS
    qspec = lambda w: pl.BlockSpec((None, tq, w), lambda b, j: (b, j, 0))
    whole = lambda a: pl.BlockSpec((None,) + a.shape[1:], lambda b, j: (b, 0, 0))
    return pl.pallas_call(
        functools.partial(_dsa_kernel, tq=tq, q_off=q_off, l_true=l_true, n_sel=n_sel, lp=lp),
        grid=(bsz, t_len // tq),
        in_specs=[qspec(q.shape[2]), qspec(qi.shape[2]), qspec(wi.shape[2]),
                  whole(k_t), whole(v), whole(ki_t),
                  _resident(bias.shape, lambda b, j: (0, 0, 0, 0))],
        out_specs=qspec(q.shape[2]),
        out_shape=jax.ShapeDtypeStruct(q.shape, BF16),
        scratch_shapes=[pltpu.VMEM((tq, lp), I32), pltpu.VMEM((tq, lp), F32), pltpu.VMEM((grp * tq, lp), F32),
                        pltpu.VMEM((IDX_HEADS, tq, LANES), F32), pltpu.VMEM((grp * tq, LANES), F32),
                        pltpu.VMEM((grp * tq, LANES), F32), pltpu.VMEM((grp * tq, A_HEAD_DIM), F32)],
        compiler_params=_params("parallel", "arbitrary"),
        name="dsa_attention",
    )(q, qi, wi, k_t, v, ki_t, bias)


BAND_ROWS = 4 * CHUNK


def _band_bias(rel_table, chunk, grp):
    win = BAND_WINDOW + grp
    t = np.arange(grp)[:, None]
    s = np.arange(win)[None, :]
    idx = np.clip(t - s + BAND_WINDOW, -BAND_MAX_REL, BAND_MAX_REL) + BAND_MAX_REL
    lo = (t // chunk) * chunk
    allowed = (s >= lo) & (s < lo + BAND_WINDOW + chunk)
    return jnp.where(allowed[None], rel_table[:, idx], NEG_BIG)


def _band_kernel(q_ref, kp_ref, kc_ref, vp_ref, vc_ref, bias_ref, o_ref, kcat, vcat, *, grp, first_has_no_past):
    i = pl.program_id(1)
    prev = kp_ref.shape[0]
    tq = q_ref.shape[0]
    win = prev + grp
    kcat[0:prev, :] = kp_ref[...]
    kcat[prev:prev + tq, :] = kc_ref[...]
    vcat[0:prev, :] = vp_ref[...]
    vcat[prev:prev + tq, :] = vc_ref[...]
    scale = BAND_HEAD_DIM ** -0.5
    lane = lax.broadcasted_iota(I32, (grp, win), 1)
    pair_w = 2 * BAND_HEAD_DIM
    assert pair_w == LANES
    first_half = lax.broadcasted_iota(I32, (grp, pair_w), 1) < BAND_HEAD_DIM
    for r0 in range(0, tq, grp):
        if first_has_no_past:
            kpos_ok = (i * tq - prev + r0 + lane) >= 0
        for hp in range(BAND_HEADS // 2):
            cols = slice(hp * pair_w, (hp + 1) * pair_w)
            q_pair = q_ref[r0:r0 + grp, cols]
            k_pair = kcat[r0:r0 + win, cols]
            v_pair = vcat[r0:r0 + win, cols]
            outs = []
            for side in range(2):
                q_one = jnp.where(first_half == (side == 0), q_pair, jnp.zeros_like(q_pair))
                lg = lax.dot_general(q_one, k_pair, NT_DIMS, preferred_element_type=F32) * scale
                lg = lg + bias_ref[2 * hp + side]
                if first_has_no_past:
                    lg = jnp.where(kpos_ok, lg, NEG_BIG)
                p = jnp.exp(lg - jnp.max(lg, axis=-1, keepdims=True))
                den = jnp.sum(p, axis=-1, keepdims=True)
                outs.append(jnp.dot(p.astype(BF16), v_pair, preferred_element_type=F32) / den)
            o_ref[r0:r0 + grp, cols] = jnp.where(first_half, outs[0], outs[1]).astype(o_ref.dtype)


def band_attention_prompt(q, k, v, bias, *, tq=512, grp=128):
    bsz, length, w = q.shape
    assert tq == BAND_WINDOW and length % tq == 0
    cur = pl.BlockSpec((None, tq, w), lambda b, i: (b, i, 0))
    prv = pl.BlockSpec((None, tq, w), lambda b, i: (b, jnp.maximum(i - 1, 0), 0))
    return pl.pallas_call(
        functools.partial(_band_kernel, grp=grp, first_has_no_past=True),
        grid=(bsz, length // tq),
        in_specs=[cur, prv, cur, prv, cur, _resident(bias.shape, lambda b, i: (0, 0, 0))],
        out_specs=cur,
        out_shape=jax.ShapeDtypeStruct(q.shape, BF16),
        scratch_shapes=[pltpu.VMEM((2 * tq, w), BF16), pltpu.VMEM((2 * tq, w), BF16)],
        compiler_params=_params("parallel", "arbitrary"),
        name="band_attention",
    )(q, k, k, v, v, bias)


def band_attention_sample(q, k_past, k_new, v_past, v_new, bias):
    bsz, t_len, w = q.shape
    past = k_past.shape[1]
    assert past == BAND_WINDOW
    new = pl.BlockSpec((None, t_len, w), lambda b, i: (b, 0, 0))
    old = pl.BlockSpec((None, past, w), lambda b, i: (b, 0, 0))
    return pl.pallas_call(
        functools.partial(_band_kernel, grp=t_len, first_has_no_past=False),
        grid=(bsz, 1),
        in_specs=[new, old, new, old, new, _resident(bias.shape, lambda b, i: (0, 0, 0))],
        out_specs=new,
        out_shape=jax.ShapeDtypeStruct(q.shape, BF16),
        scratch_shapes=[pltpu.VMEM((past + t_len, w), BF16), pltpu.VMEM((past + t_len, w), BF16)],
        compiler_params=_params("parallel", "arbitrary"),
        name="band_attention_sample",
    )(q, k_past, k_new, v_past, v_new, bias)


HIST = 8
SSD_ROWS = 256
MLSTM_ROWS = 256


def _causal_conv_chunk(x_ref, hist_scr, w_ref, b_ref, tc, width):
    hist_scr[HIST:HIST + tc, :] = x_ref[...]
    base = HIST - (width - 1)
    out = b_ref[...] + hist_scr[base:base + tc, :] * w_ref[0:1, :]
    for j in range(1, width):
        out = out + hist_scr[base + j:base + j + tc, :] * w_ref[j:j + 1, :]
    tail = hist_scr[HIST + tc - (width - 1):HIST + tc, :]
    hist_scr[base:HIST, :] = tail
    return out


def _tri(tc):
    r = lax.broadcasted_iota(I32, (tc, tc), 0)
    c = lax.broadcasted_iota(I32, (tc, tc), 1)
    return r >= c


def _eye(n):
    r = lax.broadcasted_iota(I32, (n, n), 0)
    c = lax.broadcasted_iota(I32, (n, n), 1)
    return (r == c).astype(F32)


def _cumsum_rows(x, causal):
    return jnp.dot(causal.astype(F32), x, precision=lax.Precision.HIGHEST, preferred_element_type=F32)


def _transpose_f32(x):
    return lax.dot_general(_eye(x.shape[1]), x, NT_DIMS, precision=lax.Precision.HIGHEST,
                           preferred_element_type=F32)


def _ssd_kernel(xbc_ref, z_ref, dt_ref, conv0_ref, h0_ref, cw_ref, cb_ref, dtb_ref, alog_ref, dskip_ref,
                ng_ref, y_ref, hs_ref, hist_scr, y_scr, *, tc, d_inner, n_heads):
    c = pl.program_id(1)

    @pl.when(c == 0)
    def _():
        hs_ref[...] = h0_ref[...]
        hist_scr[HIST - (SSD_CONV - 1):HIST, :] = conv0_ref[...]

    xs = _silu(_causal_conv_chunk(xbc_ref, hist_scr, cw_ref, cb_ref, tc, SSD_CONV))
    gn = SSD_GROUPS * SSD_D_STATE
    hg = n_heads // SSD_GROUPS
    causal = _tri(tc)
    x_dt = dt_ref[...] + dtb_ref[...]
    dt = jnp.maximum(x_dt, 0.0) + jnp.log1p(jnp.exp(-jnp.abs(x_dt)))
    a_head = -jnp.exp(alog_ref[...])
    cum = _cumsum_rows(dt * a_head, causal)
    cum_t = _transpose_f32(cum)
    cum_last = cum[tc - 1:tc, :]
    for g in range(SSD_GROUPS):
        bm = xs[:, d_inner + g * SSD_D_STATE:d_inner + (g + 1) * SSD_D_STATE].astype(BF16)
        cm = xs[:, d_inner + gn + g * SSD_D_STATE:d_inner + gn + (g + 1) * SSD_D_STATE].astype(BF16)
        cb = lax.dot_general(cm, bm, NT_DIMS, preferred_element_type=F32)
        for hh in range(hg):
            h = g * hg + hh
            cols = slice(h * SSD_HEAD_DIM, (h + 1) * SSD_HEAD_DIM)
            col = cum[:, h:h + 1]
            seg = col - cum_t[h:h + 1, :]
            decay = jnp.exp(jnp.where(causal, seg, -jnp.inf))
            xh = xs[:, cols]
            xdt = xh * dt[:, h:h + 1]
            h0 = hs_ref[h]
            y = jnp.dot((cb * decay).astype(BF16), xdt.astype(BF16), preferred_element_type=F32)
            y = y + jnp.exp(col) * lax.dot_general(cm, h0.astype(BF16), NT_DIMS, preferred_element_type=F32)
            y_scr[:, cols] = y + dskip_ref[:, cols] * xh
            last = cum_last[:, h:h + 1]
            w = (jnp.exp(last - col) * xdt).astype(BF16)
            hs_ref[h] = jnp.exp(last) * h0 + lax.dot_general(w, bm, TN_DIMS, preferred_element_type=F32)
    yz = y_scr[...] * _silu(z_ref[...])
    gw = d_inner // SSD_GROUPS
    for g in range(SSD_GROUPS):
        cols = slice(g * gw, (g + 1) * gw)
        seg = yz[:, cols]
        ms = jnp.mean(seg * seg, axis=-1, keepdims=True)
        y_ref[:, cols] = (seg * lax.rsqrt(ms + NORM_EPS) * ng_ref[:, cols]).astype(y_ref.dtype)


def ssd_scan(xbc, z, dt, conv0, h0, conv_w, conv_b, dt_bias, a_log, d_skip_cols, norm_g, *, tc):
    bsz, length, cc = xbc.shape
    d_inner = z.shape[2]
    n_heads = dt.shape[2]
    assert length % tc == 0
    seq = lambda w: pl.BlockSpec((None, tc, w), lambda b, c: (b, c, 0))
    per_b3 = lambda s: pl.BlockSpec((None,) + s, lambda b, c: (b,) + (0,) * len(s))
    row = lambda w: _resident((1, w), lambda b, c: (0, 0))
    return pl.pallas_call(
        functools.partial(_ssd_kernel, tc=tc, d_inner=d_inner, n_heads=n_heads),
        grid=(bsz, length // tc),
        in_specs=[seq(cc), seq(d_inner), seq(n_heads), per_b3(conv0.shape[1:]), per_b3(h0.shape[1:]),
                  _resident(conv_w.shape, lambda b, c: (0, 0)), row(cc), row(n_heads), row(n_heads),
                  row(d_inner), row(d_inner)],
        out_specs=[seq(d_inner), per_b3(h0.shape[1:])],
        out_shape=[jax.ShapeDtypeStruct((bsz, length, d_inner), BF16), jax.ShapeDtypeStruct(h0.shape, F32)],
        scratch_shapes=[pltpu.VMEM((HIST + tc, cc), F32), pltpu.VMEM((tc, d_inner), F32)],
        compiler_params=_params("parallel", "arbitrary"),
        name="ssd_scan",
    )(xbc, z, dt, conv0, h0, conv_w, conv_b, dt_bias, a_log, d_skip_cols, norm_g)


def _mlstm_kernel(xc_ref, v_ref, o_ref, gates_ref, conv0_ref, c0_ref, n0_ref, m0_ref, cw_ref, cb_ref, wq_ref,
                  wk_ref, gb_ref, ng_ref, h_ref, cs_ref, ns_ref, ms_ref, hist_scr, *, tc, d_inner):
    c = pl.program_id(1)

    @pl.when(c == 0)
    def _():
        cs_ref[...] = c0_ref[...]
        ns_ref[...] = n0_ref[...]
        ms_ref[...] = m0_ref[...]
        hist_scr[HIST - (MLSTM_CONV - 1):HIST, :] = conv0_ref[...]

    xa = _silu(_causal_conv_chunk(xc_ref, hist_scr, cw_ref, cb_ref, tc, MLSTM_CONV)).astype(BF16)
    blk = wq_ref.shape[1]
    dh = d_inner // MLSTM_HEADS
    q = jnp.concatenate([jnp.dot(xa[:, j * blk:(j + 1) * blk], wq_ref[j], preferred_element_type=F32)
                         for j in range(d_inner // blk)], axis=1)
    k = jnp.concatenate([jnp.dot(xa[:, j * blk:(j + 1) * blk], wk_ref[j], preferred_element_type=F32)
                         for j in range(d_inner // blk)], axis=1) * dh ** -0.5
    causal = _tri(tc)
    gates = gates_ref[...] + gb_ref[...]
    log_f = jnp.minimum(gates, 0.0) - jnp.log1p(jnp.exp(-jnp.abs(gates)))
    f_cum = _cumsum_rows(log_f, causal)
    f_cum_t = _transpose_f32(f_cum)
    gates_t = _transpose_f32(gates)
    for hh in range(MLSTM_HEADS):
        cols = slice(hh * dh, (hh + 1) * dh)
        fh = MLSTM_HEADS + hh
        qh = q[:, cols].astype(BF16)
        kh = k[:, cols]
        vh = v_ref[:, cols]
        fc = f_cum[:, fh:fh + 1]
        i_col = gates[:, hh:hh + 1]
        m0 = ms_ref[0:1, hh:hh + 1]
        d_log = jnp.where(causal, fc - f_cum_t[fh:fh + 1, :] + gates_t[hh:hh + 1, :], -jnp.inf)
        inter = fc + m0
        m = jnp.maximum(jnp.max(d_log, axis=-1, keepdims=True), inter)
        s = lax.dot_general(qh, kh.astype(BF16), NT_DIMS, preferred_element_type=F32) * jnp.exp(d_log - m)
        w_inter = jnp.exp(inter - m)
        c0 = cs_ref[hh]
        n0 = ns_ref[hh:hh + 1, :]
        num = (jnp.dot(s.astype(BF16), vh, preferred_element_type=F32)
               + w_inter * jnp.dot(qh, c0.astype(BF16), preferred_element_type=F32))
        den = (jnp.sum(s, axis=-1, keepdims=True)
               + w_inter * jnp.sum(q[:, cols] * n0, axis=-1, keepdims=True))
        h = num / jnp.maximum(jnp.abs(den), jnp.exp(-m))
        m_end = m[tc - 1:tc, :]
        f_last = fc[tc - 1:tc, :]
        w_end = jnp.exp(f_last - fc + i_col - m_end)
        decay = jnp.exp(f_last + m0 - m_end)
        wk = w_end * kh
        cs_ref[hh] = decay * c0 + lax.dot_general(wk.astype(BF16), vh, TN_DIMS, preferred_element_type=F32)
        ns_ref[hh:hh + 1, :] = decay * n0 + jnp.sum(wk, axis=0, keepdims=True)
        ms_ref[0:1, hh:hh + 1] = m_end
        h = jax.nn.sigmoid(o_ref[:, cols]) * h
        h = h - jnp.mean(h, axis=-1, keepdims=True)
        h = h * lax.rsqrt(jnp.mean(h * h, axis=-1, keepdims=True) + NORM_EPS)
        h_ref[:, cols] = (h * ng_ref[:, cols]).astype(h_ref.dtype)


def mlstm_scan(xc, v, o, gates, conv0, c0, n0, m0, conv_w, conv_b, wq_bd, wk_bd, gate_b, norm_g, *, tc):
    bsz, length, d_inner = xc.shape
    assert length % tc == 0
    seq = lambda w: pl.BlockSpec((None, tc, w), lambda b, c: (b, c, 0))
    per_b = lambda s: pl.BlockSpec((None,) + s, lambda b, c: (b,) + (0,) * len(s))
    row = lambda w: _resident((1, w), lambda b, c: (0, 0))
    return pl.pallas_call(
        functools.partial(_mlstm_kernel, tc=tc, d_inner=d_inner),
        grid=(bsz, length // tc),
        in_specs=[seq(d_inner), seq(d_inner), seq(d_inner), seq(gates.shape[2]), per_b(conv0.shape[1:]),
                  per_b(c0.shape[1:]), per_b(n0.shape[1:]), per_b(m0.shape[1:]),
                  _resident(conv_w.shape, lambda b, c: (0, 0)), row(d_inner),
                  _resident(wq_bd.shape, lambda b, c: (0, 0, 0)), _resident(wk_bd.shape, lambda b, c: (0, 0, 0)),
                  row(gates.shape[2]), row(d_inner)],
        out_specs=[seq(d_inner), per_b(c0.shape[1:]), per_b(n0.shape[1:]), per_b(m0.shape[1:])],
        out_shape=[jax.ShapeDtypeStruct((bsz, length, d_inner), BF16), jax.ShapeDtypeStruct(c0.shape, F32),
                   jax.ShapeDtypeStruct(n0.shape, F32), jax.ShapeDtypeStruct(m0.shape, F32)],
        scratch_shapes=[pltpu.VMEM((HIST + tc, d_inner), F32)],
        compiler_params=_params("parallel", "arbitrary"),
        name="mlstm_scan",
    )(xc, v, o, gates, conv0, c0, n0, m0, conv_w, conv_b, wq_bd, wk_bd, gate_b, norm_g)


def _block_diag(w_blk, blk):
    n, c, d = w_blk.shape
    per = blk // c
    eye = jnp.eye(per, dtype=w_blk.dtype)
    tiles = w_blk.reshape(n // per, per, c, d)
    return jnp.einsum("jpcd,pq->jpcqd", tiles, eye).reshape(n // per, per * c, per * d)


def _mixer_a(xp, xs, shapes, cache_k, cache_v, cache_ki, w_in, t5_table):
    (bp, lp_), (bs, ls) = shapes
    hd, kvd, idd = A_HEADS * A_HEAD_DIM, A_KV_HEADS * A_HEAD_DIM, IDX_HEADS * IDX_DIM
    w_pad, cols = _pad_cols(w_in, (hd, kvd, kvd, idd, IDX_DIM, IDX_HEADS))
    outs = [(cols[0], BF16, A_HEAD_DIM ** -0.5), (cols[1], F32), (cols[1], BF16), (cols[2], F32), (cols[2], BF16),
            (cols[3], BF16), (cols[4], F32), (cols[4], BF16), (cols[5], F32)]
    w_pad = w_pad.astype(BF16)
    bias = _t5_bias_tiles(t5_table)
    keys_on_lanes = lambda a: jnp.swapaxes(a, 1, 2)

    q, k, kb, v, vb, qi, ki, kib, wi = project(xp, w_pad, outs, name="project_a")
    r3 = lambda a, b_, l: a.reshape(b_, l, a.shape[-1])
    att_p = dsa_attention(r3(q, bp, lp_), r3(qi, bp, lp_), r3(wi, bp, lp_), keys_on_lanes(r3(kb, bp, lp_)),
                          r3(vb, bp, lp_), keys_on_lanes(r3(kib, bp, lp_)), bias, tq=KEY_TILE, q_off=0, l_true=lp_,
                          n_sel=min(IDX_TOPK_MAX, lp_ // 4))
    outs_p = (k.reshape(bp, lp_, A_KV_HEADS, A_HEAD_DIM), v.reshape(bp, lp_, A_KV_HEADS, A_HEAD_DIM),
              ki.reshape(bp, lp_, IDX_DIM))

    q, k, kb, v, vb, qi, ki, kib, wi = project(xs, w_pad, outs, name="project_a")
    past = cache_k.shape[1]
    total = past + ls
    lpad = -(-total // (2 * KEY_TILE)) * (2 * KEY_TILE)

    def with_past(cache, new):
        parts = [cache.reshape(bs, past, -1).astype(BF16), r3(new, bs, ls)]
        if lpad != total:
            parts.append(jnp.zeros((bs, lpad - total, new.shape[-1]), BF16))
        return jnp.concatenate(parts, axis=1)

    att_s = dsa_attention(r3(q, bs, ls), r3(qi, bs, ls), r3(wi, bs, ls), keys_on_lanes(with_past(cache_k, kb)),
                          with_past(cache_v, vb), keys_on_lanes(with_past(cache_ki, kib)), bias, tq=ls, q_off=past,
                          l_true=total, n_sel=min(IDX_TOPK_MAX, total // 4))
    outs_s = (k.reshape(bs, ls, A_KV_HEADS, A_HEAD_DIM), v.reshape(bs, ls, A_KV_HEADS, A_HEAD_DIM),
              ki.reshape(bs, ls, IDX_DIM))
    return att_p.reshape(bp * lp_, hd), att_s.reshape(bs * ls, hd), outs_p, outs_s


def _mixer_b(xp, xs, shapes, cache_k, cache_v, w_in, rel_table):
    (bp, lp_), (bs, ls) = shapes
    hd = BAND_HEADS * BAND_HEAD_DIM
    w_pad, cols = _pad_cols(w_in, (hd, hd, hd))
    outs = [(cols[0], BF16), (cols[1], F32), (cols[1], BF16), (cols[2], F32), (cols[2], BF16)]
    w_pad = w_pad.astype(BF16)

    q, k, kb, v, vb = project(xp, w_pad, outs, name="project_b")
    r3 = lambda a, b_, l: a.reshape(b_, l, hd)
    att_p = band_attention_prompt(r3(q, bp, lp_), r3(kb, bp, lp_), r3(vb, bp, lp_),
                                  _band_bias(rel_table, CHUNK, BAND_ROWS), grp=BAND_ROWS)
    keep = min(BAND_WINDOW, lp_)
    heads = lambda a, b_, l: a.reshape(b_, l, BAND_HEADS, BAND_HEAD_DIM)
    outs_p = (heads(k, bp, lp_)[:, lp_ - keep:], heads(v, bp, lp_)[:, lp_ - keep:])

    q, k, kb, v, vb = project(xs, w_pad, outs, name="project_b")
    past = cache_k.shape[1]
    att_s = band_attention_sample(r3(q, bs, ls), cache_k.reshape(bs, past, hd).astype(BF16), r3(kb, bs, ls),
                                  cache_v.reshape(bs, past, hd).astype(BF16), r3(vb, bs, ls),
                                  _band_bias(rel_table, ls, ls))
    outs_s = (jnp.concatenate([cache_k, heads(k, bs, ls)], axis=1)[:, ls:],
              jnp.concatenate([cache_v, heads(v, bs, ls)], axis=1)[:, ls:])
    return att_p.reshape(bp * lp_, hd), att_s.reshape(bs * ls, hd), outs_p, outs_s


def _mixer_c(xp, xs, shapes, ssm0, conv0, w_in, conv_w, conv_b, dt_bias, a_log, d_skip, norm_g):
    n_heads = a_log.shape[0]
    d_inner = n_heads * SSD_HEAD_DIM
    conv_dim = conv_w.shape[1]
    w_pad, cols = _pad_cols(w_in, (d_inner, conv_dim, n_heads))
    outs = [(cols[0], F32), (cols[1], F32), (cols[2], F32)]
    w_pad = w_pad.astype(BF16)
    row = lambda a: a.reshape(1, -1)
    d_skip_cols = jnp.repeat(d_skip, SSD_HEAD_DIM).reshape(1, d_inner)
    res = []
    for x, (b_, l), h0, c0 in ((xp, shapes[0], None, None), (xs, shapes[1], ssm0, conv0)):
        if h0 is None:
            h0 = jnp.zeros((b_, n_heads, SSD_HEAD_DIM, SSD_D_STATE), F32)
            c0 = jnp.zeros((b_, SSD_CONV - 1, conv_dim), F32)
        z, xbc, dt = project(x, w_pad, outs, name="project_c")
        xbc3 = xbc.reshape(b_, l, conv_dim)
        y, h_new = ssd_scan(xbc3, z.reshape(b_, l, d_inner), dt.reshape(b_, l, n_heads), c0, h0, conv_w,
                            row(conv_b), row(dt_bias), row(a_log), d_skip_cols, row(norm_g), tc=min(SSD_ROWS, l))
        conv_new = jnp.concatenate([c0, xbc3], axis=1)[:, l:]
        res.append((y.reshape(b_ * l, d_inner), (h_new, conv_new)))
    return res[0][0], res[1][0], res[0][1], res[1][1]


def _mixer_d(xp, xs, shapes, c0s, n0s, m0s, conv0s, w_in, conv_w, conv_b, wq_blk, wk_blk, gate_b, norm_g):
    d_inner = conv_w.shape[1]
    dh = d_inner // MLSTM_HEADS
    w_pad, cols = _pad_cols(w_in, (d_inner, d_inner, d_inner, 2 * MLSTM_HEADS))
    gcol = (cols[3][0], cols[3][1], cols[3][1])
    outs = [(cols[0], F32), (cols[1], BF16), (cols[2], F32), (gcol, F32)]
    w_pad = w_pad.astype(BF16)
    blk = 2 * LANES
    wq_bd = _block_diag(wq_blk, blk).astype(BF16)
    wk_bd = _block_diag(wk_blk, blk).astype(BF16)
    gate_b_pad = jnp.zeros((1, gcol[1]), F32).at[0, :2 * MLSTM_HEADS].set(gate_b)
    row = lambda a: a.reshape(1, -1)
    res = []
    for x, (b_, l), st in ((xp, shapes[0], None), (xs, shapes[1], (c0s, n0s, m0s, conv0s))):
        if st is None:
            st = (jnp.zeros((b_, MLSTM_HEADS, dh, dh), F32), jnp.zeros((b_, MLSTM_HEADS, dh), F32),
                  jnp.zeros((b_, MLSTM_HEADS), F32), jnp.zeros((b_, MLSTM_CONV - 1, d_inner), F32))
        c0, n0, m0, conv0 = st
        xc, v, o, gates = project(x, w_pad, outs, name="project_d")
        r3 = lambda a: a.reshape(b_, l, a.shape[-1])
        h, c_new, n_new, m_new = mlstm_scan(r3(xc), r3(v), r3(o), r3(gates), conv0, c0, n0,
                                            m0.reshape(b_, 1, MLSTM_HEADS), conv_w, row(conv_b), wq_bd, wk_bd,
                                            gate_b_pad, row(norm_g), tc=min(MLSTM_ROWS, l))
        conv_new = jnp.concatenate([conv0, r3(xc)], axis=1)[:, l:]
        res.append((h.reshape(b_ * l, d_inner), (c_new, n_new, m_new.reshape(b_, MLSTM_HEADS), conv_new)))
    return res[0][0], res[1][0], res[0][1], res[1][1]


def kernel(x_prompt, x_sample, cache_a_k, cache_a_v, cache_a_kidx, cache_b_k, cache_b_v, state_c_ssm, state_c_conv, state_d_c, state_d_n, state_d_m, state_d_conv, a_w_in, a_w_out, t5_table, b_w_in, b_w_out, b_rel_table, c_w_in, c_conv_w, c_conv_b, c_dt_bias, c_a_log, c_d_skip, c_norm_g, c_w_out, d_w_in, d_conv_w, d_conv_b, d_wq_blk, d_wk_blk, d_gate_b, d_norm_g, d_w_out, ffn1_wg, ffn1_wu, ffn1_wd, ffn2_wg, ffn2_wu, ffn2_wd, ln_g, ln_b):
    bp, lp_, d = x_prompt.shape
    bs, ls, _ = x_sample.shape
    depth = ffn1_wg.shape[0]
    alpha = (2.0 * depth) ** 0.25
    shapes = ((bp, lp_), (bs, ls))
    xp = x_prompt.reshape(bp * lp_, d)
    xs = x_sample.reshape(bs * ls, d)
    ffn_w = [[w.astype(BF16) for w in ws] for ws in ((ffn1_wg, ffn1_wu, ffn1_wd), (ffn2_wg, ffn2_wu, ffn2_wd))]
    w_out = [w.astype(BF16) for w in (a_w_out, b_w_out, c_w_out, d_w_out)]
    row = lambda a: a.reshape(1, -1)
    extra = {}
    for i in range(depth):
        g, b = ln_g[i], ln_b[i]
        xp = ffn_postnorm(xp, *ffn_w[0], i, row(g[0]), row(b[0]), alpha)
        xs = ffn_postnorm(xs, *ffn_w[0], i, row(g[0]), row(b[0]), alpha)
        kind = i % 4
        if kind == 0:
            mp, ms, op, os_ = _mixer_a(xp, xs, shapes, cache_a_k, cache_a_v, cache_a_kidx, a_w_in, t5_table)
        elif kind == 1:
            mp, ms, op, os_ = _mixer_b(xp, xs, shapes, cache_b_k, cache_b_v, b_w_in, b_rel_table)
        elif kind == 2:
            mp, ms, op, os_ = _mixer_c(xp, xs, shapes, state_c_ssm, state_c_conv, c_w_in, c_conv_w, c_conv_b,
                                       c_dt_bias, c_a_log, c_d_skip, c_norm_g)
        else:
            mp, ms, op, os_ = _mixer_d(xp, xs, shapes, state_d_c, state_d_n, state_d_m, state_d_conv, d_w_in,
                                       d_conv_w, d_conv_b, d_wq_blk, d_wk_blk, d_gate_b, d_norm_g)
        extra[kind] = (op, os_)
        xp = proj_postnorm(xp, mp, w_out[kind], row(g[1]), row(b[1]), alpha)
        xs = proj_postnorm(xs, ms, w_out[kind], row(g[1]), row(b[1]), alpha)
        xp = ffn_postnorm(xp, *ffn_w[1], i, row(g[2]), row(b[2]), alpha)
        xs = ffn_postnorm(xs, *ffn_w[1], i, row(g[2]), row(b[2]), alpha)
    prompt_side = tuple(t for kind in range(4) for t in extra[kind][0])
    sample_side = tuple(t for kind in range(4) for t in extra[kind][1])
    return (xp.reshape(bp, lp_, d), xs.reshape(bs, ls, d)) + prompt_side + sample_side
```

```python
import functools
import math

import numpy as np
import jax
import jax.numpy as jnp
from jax import lax
from jax.experimental import pallas as pl
from jax.experimental.pallas import tpu as pltpu

F32 = jnp.float32
BF16 = jnp.bfloat16
I32 = jnp.int32

CHUNK = 64
NORM_EPS = 1e-5
A_HEADS, A_KV_HEADS, A_HEAD_DIM = 8, 2, 128
IDX_HEADS, IDX_DIM, IDX_TOPK_MAX = 8, 64, 256
T5_BUCKETS, T5_MAX_DIST = 32, 128
BAND_HEADS, BAND_HEAD_DIM, BAND_LEFT_CHUNKS, BAND_MAX_REL = 16, 64, 8, 128
BAND_WINDOW = BAND_LEFT_CHUNKS * CHUNK
SSD_HEAD_DIM, SSD_GROUPS, SSD_D_STATE, SSD_CONV = 64, 8, 128, 4
MLSTM_HEADS, MLSTM_CONV, MLSTM_QK_BLOCK = 4, 4, 4

LANES = 128
NEG_BIG = -1e30
VMEM_LIMIT = 56 * 1024 * 1024

NT_DIMS = (((1,), (1,)), ((), ()))
TN_DIMS = (((0,), (0,)), ((), ()))


def _params(*sem):
    return pltpu.CompilerParams(dimension_semantics=sem, vmem_limit_bytes=VMEM_LIMIT)


def _resident(shape, index_map):
    return pl.BlockSpec(shape, index_map, pipeline_mode=pl.Buffered(1))


def _layer_norm(y, g, b):
    mu = jnp.mean(y, axis=-1, keepdims=True)
    yc = y - mu
    var = jnp.mean(yc * yc, axis=-1, keepdims=True)
    return yc * lax.rsqrt(var + NORM_EPS) * g + b


def _silu(x):
    return x * jax.nn.sigmoid(x)


def _row_tile(n, want):
    t = min(n, want)
    assert n % t == 0, (n, t)
    return t


def _ffn_block(x, wg_ref, wu_ref, wd_ref, g_ref, b_ref, alpha, f_cuts):
    xb = x.astype(BF16)
    acc = jnp.zeros(x.shape, F32)
    for lo, hi in zip(f_cuts[:-1], f_cuts[1:]):
        sl = slice(lo, hi)
        gate = jnp.dot(xb, wg_ref[:, sl], preferred_element_type=F32)
        up = jnp.dot(xb, wu_ref[:, sl], preferred_element_type=F32)
        h = (_silu(gate) * up).astype(BF16)
        acc = acc + jnp.dot(h, wd_ref[sl, :], preferred_element_type=F32)
    return _layer_norm(alpha * x + 0.5 * acc, g_ref[...], b_ref[...])


def _ffn_kernel(x_ref, wg_ref, wu_ref, wd_ref, g_ref, b_ref, o_ref, *, alpha, f_cuts):
    o_ref[...] = _ffn_block(x_ref[...], wg_ref, wu_ref, wd_ref, g_ref, b_ref, alpha, f_cuts)


def _mixout_ffn_kernel(x_ref, m_ref, wo_ref, g1_ref, b1_ref, wg_ref, wu_ref, wd_ref, g2_ref, b2_ref, o_ref, *,
                       alpha, f_cuts):
    sub = jnp.dot(m_ref[...], wo_ref[...], preferred_element_type=F32)
    x1 = _layer_norm(alpha * x_ref[...] + sub, g1_ref[...], b1_ref[...])
    o_ref[...] = _ffn_block(x1, wg_ref, wu_ref, wd_ref, g2_ref, b2_ref, alpha, f_cuts)


MXU_WIDTH = 256


def _ffn_cuts(d_ff, n_chunks):
    tiles = -(-d_ff // MXU_WIDTH)
    cuts = [min(d_ff, MXU_WIDTH * (-(-tiles * c // n_chunks))) for c in range(n_chunks + 1)]
    return tuple(cuts)


def ffn_postnorm(x, wg, wu, wd, layer, g, b, alpha, tm=512, n_chunks=2):
    n, d = x.shape
    d_ff = wg.shape[2]
    tm = _row_tile(n, tm)
    return pl.pallas_call(
        functools.partial(_ffn_kernel, alpha=alpha, f_cuts=_ffn_cuts(d_ff, n_chunks)),
        grid=(n // tm,),
        in_specs=[
            pl.BlockSpec((tm, d), lambda i: (i, 0)),
            _resident((None, d, d_ff), lambda i: (layer, 0, 0)),
            _resident((None, d, d_ff), lambda i: (layer, 0, 0)),
            _resident((None, d_ff, d), lambda i: (layer, 0, 0)),
            _resident((1, d), lambda i: (0, 0)),
            _resident((1, d), lambda i: (0, 0)),
        ],
        out_specs=pl.BlockSpec((tm, d), lambda i: (i, 0)),
        out_shape=jax.ShapeDtypeStruct((n, d), F32),
        compiler_params=_params("parallel"),
        name="ffn_postnorm",
    )(x, wg, wu, wd, g, b)


def mixout_ffn(x, m, w_out, g1, b1, wg, wu, wd, layer, g2, b2, alpha, tm=512, n_chunks=2):
    n, d = x.shape
    k = m.shape[1]
    d_ff = wg.shape[2]
    tm = _row_tile(n, tm)
    vec = lambda: _resident((1, d), lambda i: (0, 0))
    return pl.pallas_call(
        functools.partial(_mixout_ffn_kernel, alpha=alpha, f_cuts=_ffn_cuts(d_ff, n_chunks)),
        grid=(n // tm,),
        in_specs=[
            pl.BlockSpec((tm, d), lambda i: (i, 0)),
            pl.BlockSpec((tm, k), lambda i: (i, 0)),
            _resident((k, d), lambda i: (0, 0)), vec(), vec(),
            _resident((None, d, d_ff), lambda i: (layer, 0, 0)),
            _resident((None, d, d_ff), lambda i: (layer, 0, 0)),
            _resident((None, d_ff, d), lambda i: (layer, 0, 0)), vec(), vec(),
        ],
        out_specs=pl.BlockSpec((tm, d), lambda i: (i, 0)),
        out_shape=jax.ShapeDtypeStruct((n, d), F32),
        compiler_params=_params("parallel"),
        name="mixout_ffn",
    )(x, m, w_out, g1, b1, wg, wu, wd, g2, b2)


def _proj_kernel(x_ref, w_ref, *o_refs, cols, scales):
    xb = x_ref[...].astype(BF16)
    done = {}
    for (off, pad_w, true_w), scale, o_ref in zip(cols, scales, o_refs):
        if (off, pad_w) not in done:
            done[(off, pad_w)] = jnp.dot(xb, w_ref[:, off:off + pad_w], preferred_element_type=F32)
        y = done[(off, pad_w)][:, :true_w]
        o_ref[...] = (y if scale is None else y * scale).astype(o_ref.dtype)


def _pad_cols(w, widths):
    pieces, offs, off, src = [], [], 0, 0
    for wd in widths:
        pad_w = -(-wd // LANES) * LANES
        pieces.append(w[:, src:src + wd])
        if pad_w != wd:
            pieces.append(jnp.zeros((w.shape[0], pad_w - wd), w.dtype))
        offs.append((off, pad_w, wd))
        off += pad_w
        src += wd
    return jnp.concatenate(pieces, axis=1), offs


def project(x, w_pad, outs, tm=512, name="project"):
    n, d = x.shape
    tm = _row_tile(n, tm)
    cols = tuple(o[0] for o in outs)
    scales = tuple(o[2] if len(o) > 2 else None for o in outs)
    outs = [o[:2] for o in outs]
    return pl.pallas_call(
        functools.partial(_proj_kernel, cols=cols, scales=scales),
        grid=(n // tm,),
        in_specs=[pl.BlockSpec((tm, d), lambda i: (i, 0)),
                  _resident(w_pad.shape, lambda i: (0, 0))],
        out_specs=[pl.BlockSpec((tm, c[2]), lambda i: (i, 0)) for c in cols],
        out_shape=[jax.ShapeDtypeStruct((n, c[2]), dt) for c, dt in outs],
        compiler_params=_params("parallel"),
        name=name,
    )(x, w_pad)


def _t5_bucket_np(rel):
    half = T5_BUCKETS // 2
    max_exact = half // 2
    n = np.abs(rel)
    nf = np.maximum(n, 1).astype(np.float32)
    large = max_exact + (np.log(nf / np.float32(max_exact)) / np.float32(math.log(T5_MAX_DIST / max_exact))
                         * np.float32(half - max_exact)).astype(np.int32)
    large = np.minimum(large, half - 1)
    return np.where(rel > 0, half, 0) + np.where(n < max_exact, n, large)


KEY_TILE = 2 * LANES


def _toeplitz(w, t_rows, s_cols):
    n = t_rows + s_cols
    lead = w.shape[:-1]
    wp = jnp.concatenate([w, jnp.zeros(lead + (1,), w.dtype)], axis=-1)
    flat = jnp.broadcast_to(wp[..., None, :], lead + (t_rows, n)).reshape(lead + (t_rows * n,))
    skew = flat[..., :t_rows * (n - 1)].reshape(lead + (t_rows, n - 1))
    return skew[..., t_rows - 1:t_rows - 1 + s_cols]


def _t5_bias_tiles(t5_table):
    assert T5_MAX_DIST <= KEY_TILE
    rel = np.arange(3 * KEY_TILE - 1) - (KEY_TILE - 1) - KEY_TILE
    far = t5_table[int(_t5_bucket_np(np.array(-T5_MAX_DIST)))]
    by_rel = (t5_table[_t5_bucket_np(rel)] - far).T
    strip = _toeplitz(by_rel, KEY_TILE, 2 * KEY_TILE)
    return jnp.stack([strip[:, :, KEY_TILE:], strip[:, :, :KEY_TILE]])


def _dsa_kernel(q_ref, qi_ref, wi_ref, kt_ref, v_ref, kit_ref, bias_ref, o_ref,
                key_scr, mask_scr, lg_scr, wib_scr, m_scr, l_scr, acc_scr, *, tq, q_off, l_true, n_sel, lp):
    j = pl.program_id(1)
    q0 = q_off + j * tq
    kend = jnp.minimum(((q0 + tq - 1) // CHUNK + 1) * CHUNK, l_true)
    n_tiles = (kend + KEY_TILE - 1) // KEY_TILE
    n_far = jnp.maximum(q0 // KEY_TILE - 1, 0)
    qpos = q0 + lax.broadcasted_iota(I32, (tq, LANES), 0)
    lane = lax.broadcasted_iota(I32, (tq, LANES), 1)
    q_chunk = qpos // CHUNK
    int_min = jnp.int32(-2 ** 31)
    wide = 2 * KEY_TILE

    def tile_off(kt):
        return pl.multiple_of(kt * KEY_TILE, KEY_TILE)

    def lane_blocks(off, width):
        return [pl.multiple_of(off + u * LANES, LANES) for u in range(width // LANES)]

    def admissible(kpos):
        return ((kpos // CHUNK) <= q_chunk) & (kpos < l_true)

    def slabs(lo, hi, fn):
        def pair(i, carry):
            fn(tile_off(lo + 2 * i), wide)
            return carry
        lax.fori_loop(0, (hi - lo) // 2, pair, 0)

        @pl.when((hi - lo) % 2 == 1)
        def _():
            fn(tile_off(hi - 1), KEY_TILE)

    wi = wi_ref[...] * (IDX_HEADS ** -0.5) * (IDX_DIM ** -0.5)
    for h in range(IDX_HEADS):
        wib_scr[h] = jnp.broadcast_to(wi[:, h:h + 1], (tq, LANES))

    def to_key(x):
        bits = lax.bitcast_convert_type(x, I32)
        return jnp.where(bits < 0, (bits ^ jnp.int32(0x7FFFFFFF)) + 1, bits)

    def score_slab(off, width):
        ki_t = kit_ref[:, pl.ds(off, width)]
        accs = [jnp.zeros((tq, width), F32)] * 2
        for h in range(IDX_HEADS):
            s = jnp.dot(qi_ref[:, h * IDX_DIM:(h + 1) * IDX_DIM], ki_t, preferred_element_type=F32)
            w = jnp.concatenate([wib_scr[h]] * (width // LANES), axis=1)
            accs[h % 2] = accs[h % 2] + jnp.maximum(s, 0.0) * w
        acc = accs[0] + accs[1]
        for u, o in enumerate(lane_blocks(off, width)):
            sc = jnp.where(admissible(o + lane), acc[:, u * LANES:(u + 1) * LANES], -jnp.inf)
            key_scr[:, pl.ds(o, LANES)] = to_key(sc)

    slabs(0, n_tiles, score_slab)

    @pl.when(n_tiles % 2 == 1)
    def _():
        key_scr[:, pl.ds(tile_off(n_tiles), KEY_TILE)] = to_key(jnp.full((tq, KEY_TILE), -jnp.inf, F32))

    ones = jnp.ones((LANES, LANES), BF16)

    def count(pred):
        half = min(tq, LANES)
        lane_half = lax.broadcasted_iota(I32, (half, LANES), 1)

        def count_rows(r0):
            rs = slice(r0, r0 + half)

            def body(ct, acc):
                off = pl.multiple_of(ct * wide, wide)
                keys = key_scr[rs, pl.ds(off, wide)]
                for u in range(wide // LANES):
                    hit = pred(keys[:, u * LANES:(u + 1) * LANES], off + u * LANES + lane_half, rs)
                    acc = acc + hit.astype(I32)
                return acc
            return lax.fori_loop(0, (n_tiles + 1) // 2, body, jnp.zeros((half, LANES), I32))

        acc = jnp.concatenate([count_rows(r0) for r0 in range(0, tq, half)], axis=0)
        return jnp.dot(acc.astype(F32).astype(BF16), ones, preferred_element_type=F32)

    check_every = 4

    def search_body(state):
        i, thr_u, settled, _ = state

        def one_bit(b, st):
            thr_u, settled = st
            cand_u = thr_u | lax.shift_left(jnp.int32(1), 31 - (i + b))
            cand = cand_u ^ int_min
            cnt = count(lambda key, kpos, rs: key >= cand[rs])
            thr_u = jnp.where((cnt >= n_sel) & (settled == 0), cand_u, thr_u)
            return thr_u, jnp.where(cnt == n_sel, 1, settled)

        thr_u, settled = lax.fori_loop(0, check_every, one_bit, (thr_u, settled))
        return i + check_every, thr_u, settled, jnp.min(settled.astype(F32))

    zeros = jnp.zeros((tq, LANES), I32)
    _, thr_u, _, _ = lax.while_loop(lambda s: (s[0] < 32) & (s[3] == 0.0), search_body,
                                    (jnp.int32(0), zeros, zeros, jnp.float32(0.0)))
    thr = thr_u ^ int_min
    need = n_sel - count(lambda key, kpos, rs: key > thr[rs])
    n_eq = count(lambda key, kpos, rs: key == thr[rs])
    has_tie = jnp.max(n_eq - need) > 0.0

    nbits = int(lp).bit_length()

    def tie_cut():
        def cut_body(i, cut):
            cand = cut | lax.shift_left(jnp.int32(1), nbits - 1 - i)
            cnt = count(lambda key, kpos, rs: (key == thr[rs]) & (kpos < cand[rs]))
            return jnp.where(cnt <= need, cand, cut)
        return lax.fori_loop(0, nbits, cut_body, jnp.zeros((tq, LANES), I32))

    cut = lax.cond(has_tie, tie_cut, lambda: jnp.full((tq, LANES), 2 ** 30, I32))

    def mask_slab(off, width):
        for o in lane_blocks(off, width):
            key = key_scr[:, pl.ds(o, LANES)]
            kpos = o + lane
            sel = ((key > thr) | ((key == thr) & (kpos < cut))) & admissible(kpos)
            mask_scr[:, pl.ds(o, LANES)] = jnp.where(sel, 0.0, NEG_BIG)

    slabs(0, n_tiles, mask_slab)

    grp = A_HEADS // A_KV_HEADS
    rows = grp * tq
    head_cols = lambda h: slice(h * A_HEAD_DIM, (h + 1) * A_HEAD_DIM)
    for g in range(A_KV_HEADS):
        qg = jnp.concatenate([q_ref[:, head_cols(g * grp + i)] for i in range(grp)], axis=0)
        m_scr[...] = jnp.full(m_scr.shape, NEG_BIG, F32)

        def qk_slab(off, width, near=False, qg=qg, g=g):
            lg = jnp.dot(qg, kt_ref[head_cols(g), pl.ds(off, width)], preferred_element_type=F32)
            lg = lg.reshape(grp, tq, width) + mask_scr[:, pl.ds(off, width)][None]
            if near:
                back = jnp.clip((q0 - off) // KEY_TILE, 0, 1)
                lg = lg + bias_ref[back, g * grp:(g + 1) * grp, 0:tq, :]
            lg = lg.reshape(rows, width)
            lg_scr[:, pl.ds(off, width)] = lg
            mx = lg[:, 0:LANES]
            for u in range(1, width // LANES):
                mx = jnp.maximum(mx, lg[:, u * LANES:(u + 1) * LANES])
            m_scr[...] = jnp.maximum(m_scr[...], mx)

        def near_body(kt, carry, qk_slab=qk_slab):
            qk_slab(tile_off(kt), KEY_TILE, near=True)
            return carry

        slabs(0, n_far, qk_slab)
        lax.fori_loop(n_far, n_tiles, near_body, 0)
        m_scr[...] = jnp.broadcast_to(jnp.max(m_scr[...], axis=-1, keepdims=True), m_scr.shape)
        l_scr[...] = jnp.zeros(l_scr.shape, F32)
        acc_scr[...] = jnp.zeros(acc_scr.shape, F32)

        def pv_slab(off, width, g=g):
            lg = lg_scr[:, pl.ds(off, width)]
            m_b = m_scr[...]
            ps = [jnp.exp(lg[:, u * LANES:(u + 1) * LANES] - m_b) for u in range(width // LANES)]
            l_scr[...] += functools.reduce(lambda a, b: a + b, ps)
            p = jnp.concatenate(ps, axis=1).astype(BF16)
            acc_scr[...] += jnp.dot(p, v_ref[pl.ds(off, width), head_cols(g)], preferred_element_type=F32)

        slabs(0, n_tiles, pv_slab)
        out = acc_scr[...] / jnp.sum(l_scr[...], axis=-1, keepdims=True)
        for i in range(grp):
            o_ref[:, head_cols(g * grp + i)] = out[i * tq:(i + 1) * tq].astype(o_ref.dtype)


def dsa_attention(q, qi, wi, k_t, v, ki_t, bias, *, tq, q_off, l_true, n_sel):
    bsz, t_len, _ = q.shape
    lp = v.shape[1]
    assert lp % (2 * KEY_TILE) == 0 and t_len % tq == 0 and q_off % KEY_TILE == 0
    assert tq == KEY_TILE or (t_len == tq and tq < KEY_TILE)
    grp = A_HEADS // A_KV_HEADS
    qspec = lambda w: pl.BlockSpec((None, tq, w), lambda b, j: (b, j, 0))
    whole = lambda a: pl.BlockSpec((None,) + a.shape[1:], lambda b, j: (b, 0, 0))
    return pl.pallas_call(
        functools.partial(_dsa_kernel, tq=tq, q_off=q_off, l_true=l_true, n_sel=n_sel, lp=lp),
        grid=(bsz, t_len // tq),
        in_specs=[qspec(q.shape[2]), qspec(qi.shape[2]), qspec(wi.shape[2]),
                  whole(k_t), whole(v), whole(ki_t),
                  _resident(bias.shape, lambda b, j: (0, 0, 0, 0))],
        out_specs=qspec(q.shape[2]),
        out_shape=jax.ShapeDtypeStruct(q.shape, BF16),
        scratch_shapes=[pltpu.VMEM((tq, lp), I32), pltpu.VMEM((tq, lp), F32), pltpu.VMEM((grp * tq, lp), F32),
                        pltpu.VMEM((IDX_HEADS, tq, LANES), F32), pltpu.VMEM((grp * tq, LANES), F32),
                        pltpu.VMEM((grp * tq, LANES), F32), pltpu.VMEM((grp * tq, A_HEAD_DIM), F32)],
        compiler_params=_params("parallel", "arbitrary"),
        name="dsa_attention",
    )(q, qi, wi, k_t, v, ki_t, bias)


BAND_ROWS = 4 * CHUNK


def _band_bias(rel_table, chunk, grp):
    win = BAND_WINDOW + grp
    t = np.arange(grp)[:, None]
    s = np.arange(win)[None, :]
    k = np.arange(grp + win - 1)
    idx = np.clip(BAND_WINDOW + (grp - 1) - k, -BAND_MAX_REL, BAND_MAX_REL) + BAND_MAX_REL
    bias = _toeplitz(rel_table[:, idx], grp, win)
    lo = (t // chunk) * chunk
    allowed = (s >= lo) & (s < lo + BAND_WINDOW + chunk)
    return jnp.where(allowed[None], bias, NEG_BIG)


def _band_kernel(q_ref, kp_ref, kc_ref, vp_ref, vc_ref, bias_ref, o_ref, kcat, vcat, *, grp, first_has_no_past):
    i = pl.program_id(1)
    prev = kp_ref.shape[0]
    tq = q_ref.shape[0]
    win = prev + grp
    kcat[0:prev, :] = kp_ref[...]
    kcat[prev:prev + tq, :] = kc_ref[...]
    vcat[0:prev, :] = vp_ref[...]
    vcat[prev:prev + tq, :] = vc_ref[...]
    scale = BAND_HEAD_DIM ** -0.5
    lane = lax.broadcasted_iota(I32, (grp, win), 1)
    pair_w = 2 * BAND_HEAD_DIM
    assert pair_w == LANES
    first_half = lax.broadcasted_iota(I32, (grp, pair_w), 1) < BAND_HEAD_DIM
    for r0 in range(0, tq, grp):
        if first_has_no_past:
            kpos_ok = (i * tq - prev + r0 + lane) >= 0
        for hp in range(BAND_HEADS // 2):
            cols = slice(hp * pair_w, (hp + 1) * pair_w)
            q_pair = q_ref[r0:r0 + grp, cols]
            k_pair = kcat[r0:r0 + win, cols]
            v_pair = vcat[r0:r0 + win, cols]
            outs = []
            for side in range(2):
                q_one = jnp.where(first_half == (side == 0), q_pair, jnp.zeros_like(q_pair))
                lg = lax.dot_general(q_one, k_pair, NT_DIMS, preferred_element_type=F32) * scale
                lg = lg + bias_ref[2 * hp + side]
                if first_has_no_past:
                    lg = jnp.where(kpos_ok, lg, NEG_BIG)
                p = jnp.exp(lg - jnp.max(lg, axis=-1, keepdims=True))
                den = jnp.sum(p, axis=-1, keepdims=True)
                outs.append(jnp.dot(p.astype(BF16), v_pair, preferred_element_type=F32) / den)
            o_ref[r0:r0 + grp, cols] = jnp.where(first_half, outs[0], outs[1]).astype(o_ref.dtype)


def band_attention_prompt(q, k, v, bias, *, tq=512, grp=128):
    bsz, length, w = q.shape
    assert tq == BAND_WINDOW and length % tq == 0
    cur = pl.BlockSpec((None, tq, w), lambda b, i: (b, i, 0))
    prv = pl.BlockSpec((None, tq, w), lambda b, i: (b, jnp.maximum(i - 1, 0), 0))
    return pl.pallas_call(
        functools.partial(_band_kernel, grp=grp, first_has_no_past=True),
        grid=(bsz, length // tq),
        in_specs=[cur, prv, cur, prv, cur, _resident(bias.shape, lambda b, i: (0, 0, 0))],
        out_specs=cur,
        out_shape=jax.ShapeDtypeStruct(q.shape, BF16),
        scratch_shapes=[pltpu.VMEM((2 * tq, w), BF16), pltpu.VMEM((2 * tq, w), BF16)],
        compiler_params=_params("parallel", "arbitrary"),
        name="band_attention",
    )(q, k, k, v, v, bias)


def band_attention_sample(q, k_past, k_new, v_past, v_new, bias):
    bsz, t_len, w = q.shape
    past = k_past.shape[1]
    assert past == BAND_WINDOW
    new = pl.BlockSpec((None, t_len, w), lambda b, i: (b, 0, 0))
    old = pl.BlockSpec((None, past, w), lambda b, i: (b, 0, 0))
    return pl.pallas_call(
        functools.partial(_band_kernel, grp=t_len, first_has_no_past=False),
        grid=(bsz, 1),
        in_specs=[new, old, new, old, new, _resident(bias.shape, lambda b, i: (0, 0, 0))],
        out_specs=new,
        out_shape=jax.ShapeDtypeStruct(q.shape, BF16),
        scratch_shapes=[pltpu.VMEM((past + t_len, w), BF16), pltpu.VMEM((past + t_len, w), BF16)],
        compiler_params=_params("parallel", "arbitrary"),
        name="band_attention_sample",
    )(q, k_past, k_new, v_past, v_new, bias)


HIST = 8
SSD_ROWS = 128
MLSTM_ROWS = 256


def _causal_conv_chunk(x_ref, hist_scr, w_ref, b_ref, tc, width):
    hist_scr[HIST:HIST + tc, :] = x_ref[...]
    base = HIST - (width - 1)
    out = b_ref[...] + hist_scr[base:base + tc, :] * w_ref[0:1, :]
    for j in range(1, width):
        out = out + hist_scr[base + j:base + j + tc, :] * w_ref[j:j + 1, :]
    tail = hist_scr[HIST + tc - (width - 1):HIST + tc, :]
    hist_scr[base:HIST, :] = tail
    return out


def _tri(tc):
    r = lax.broadcasted_iota(I32, (tc, tc), 0)
    c = lax.broadcasted_iota(I32, (tc, tc), 1)
    return r >= c


def _eye(n):
    r = lax.broadcasted_iota(I32, (n, n), 0)
    c = lax.broadcasted_iota(I32, (n, n), 1)
    return (r == c).astype(F32)


def _cumsum_rows(x, causal):
    return jnp.dot(causal.astype(F32), x, precision=lax.Precision.HIGHEST, preferred_element_type=F32)


def _transpose_f32(x):
    return lax.dot_general(_eye(x.shape[1]), x, NT_DIMS, precision=lax.Precision.HIGHEST,
                           preferred_element_type=F32)


def _ssd_kernel(xbc_ref, z_ref, dt_ref, conv0_ref, h0_ref, cw_ref, cb_ref, dtb_ref, alog_ref, dskip_ref,
                ng_ref, y_ref, hs_ref, hist_scr, y_scr, *, tc, d_inner, n_heads):
    c = pl.program_id(1)

    @pl.when(c == 0)
    def _():
        hs_ref[...] = h0_ref[...]
        hist_scr[HIST - (SSD_CONV - 1):HIST, :] = conv0_ref[...]

    xs = _silu(_causal_conv_chunk(xbc_ref, hist_scr, cw_ref, cb_ref, tc, SSD_CONV))
    gn = SSD_GROUPS * SSD_D_STATE
    hg = n_heads // SSD_GROUPS
    causal = _tri(tc)
    x_dt = dt_ref[...] + dtb_ref[...]
    dt = jnp.maximum(x_dt, 0.0) + jnp.log1p(jnp.exp(-jnp.abs(x_dt)))
    a_head = -jnp.exp(alog_ref[...])
    cum = _cumsum_rows(dt * a_head, causal)
    cum_t = _transpose_f32(cum)
    cum_last = cum[tc - 1:tc, :]
    pair_w = 2 * SSD_HEAD_DIM
    assert pair_w == LANES and hg % 2 == 0
    first_lanes = lax.broadcasted_iota(I32, (tc, pair_w), 1) < SSD_HEAD_DIM
    first_rows = lax.broadcasted_iota(I32, (pair_w, SSD_D_STATE), 0) < SSD_HEAD_DIM
    pick = lambda a, b: jnp.where(first_lanes, a, b)
    for g in range(SSD_GROUPS):
        bm = xs[:, d_inner + g * SSD_D_STATE:d_inner + (g + 1) * SSD_D_STATE].astype(BF16)
        cm = xs[:, d_inner + gn + g * SSD_D_STATE:d_inner + gn + (g + 1) * SSD_D_STATE].astype(BF16)
        cb = lax.dot_general(cm, bm, NT_DIMS, preferred_element_type=F32)
        for pp in range(hg // 2):
            p = g * (hg // 2) + pp
            ha, hb = 2 * p, 2 * p + 1
            cols = slice(p * pair_w, (p + 1) * pair_w)
            col_a, col_b = cum[:, ha:ha + 1], cum[:, hb:hb + 1]
            xh = xs[:, cols]
            xdt = xh * pick(dt[:, ha:ha + 1], dt[:, hb:hb + 1])
            xdt_b = xdt.astype(BF16)
            ys = []
            for h, col in ((ha, col_a), (hb, col_b)):
                decay = jnp.exp(jnp.where(causal, col - cum_t[h:h + 1, :], -jnp.inf))
                ys.append(jnp.dot((cb * decay).astype(BF16), xdt_b, preferred_element_type=F32))
            h0 = hs_ref[p]
            inter = lax.dot_general(cm, h0.astype(BF16), NT_DIMS, preferred_element_type=F32)
            y = pick(ys[0], ys[1]) + pick(jnp.exp(col_a), jnp.exp(col_b)) * inter
            y_scr[:, cols] = y + dskip_ref[:, cols] * xh
            last_a, last_b = cum_last[:, ha:ha + 1], cum_last[:, hb:hb + 1]
            w = (pick(jnp.exp(last_a - col_a), jnp.exp(last_b - col_b)) * xdt).astype(BF16)
            keep = jnp.where(first_rows, jnp.exp(last_a), jnp.exp(last_b))
            hs_ref[p] = keep * h0 + lax.dot_general(w, bm, TN_DIMS, preferred_element_type=F32)
    yz = y_scr[...] * _silu(z_ref[...])
    gw = d_inner // SSD_GROUPS
    for g in range(SSD_GROUPS):
        cols = slice(g * gw, (g + 1) * gw)
        seg = yz[:, cols]
        ms = jnp.mean(seg * seg, axis=-1, keepdims=True)
        y_ref[:, cols] = (seg * lax.rsqrt(ms + NORM_EPS) * ng_ref[:, cols]).astype(y_ref.dtype)


def ssd_scan(xbc, z, dt, conv0, h0, conv_w, conv_b, dt_bias, a_log, d_skip_cols, norm_g, *, tc):
    bsz, length, cc = xbc.shape
    d_inner = z.shape[2]
    n_heads = dt.shape[2]
    assert length % tc == 0
    state_shape = h0.shape
    h0 = h0.reshape(bsz, n_heads // 2, 2 * state_shape[2], state_shape[3])
    seq = lambda w: pl.BlockSpec((None, tc, w), lambda b, c: (b, c, 0))
    per_b3 = lambda s: pl.BlockSpec((None,) + s, lambda b, c: (b,) + (0,) * len(s))
    row = lambda w: _resident((1, w), lambda b, c: (0, 0))
    y, h_new = pl.pallas_call(
        functools.partial(_ssd_kernel, tc=tc, d_inner=d_inner, n_heads=n_heads),
        grid=(bsz, length // tc),
        in_specs=[seq(cc), seq(d_inner), seq(n_heads), per_b3(conv0.shape[1:]), per_b3(h0.shape[1:]),
                  _resident(conv_w.shape, lambda b, c: (0, 0)), row(cc), row(n_heads), row(n_heads),
                  row(d_inner), row(d_inner)],
        out_specs=[seq(d_inner), per_b3(h0.shape[1:])],
        out_shape=[jax.ShapeDtypeStruct((bsz, length, d_inner), BF16), jax.ShapeDtypeStruct(h0.shape, F32)],
        scratch_shapes=[pltpu.VMEM((HIST + tc, cc), F32), pltpu.VMEM((tc, d_inner), F32)],
        compiler_params=_params("parallel", "arbitrary"),
        name="ssd_scan",
    )(xbc, z, dt, conv0, h0, conv_w, conv_b, dt_bias, a_log, d_skip_cols, norm_g)
    return y, h_new.reshape(state_shape)


def _mlstm_kernel(xc_ref, v_ref, o_ref, gates_ref, conv0_ref, c0_ref, n0_ref, m0_ref, cw_ref, cb_ref, wq_ref,
                  wk_ref, gb_ref, ng_ref, h_ref, cs_ref, ns_ref, ms_ref, hist_scr, *, tc, d_inner):
    c = pl.program_id(1)

    @pl.when(c == 0)
    def _():
        cs_ref[...] = c0_ref[...]
        ns_ref[...] = n0_ref[...]
        ms_ref[...] = m0_ref[...]
        hist_scr[HIST - (MLSTM_CONV - 1):HIST, :] = conv0_ref[...]

    xa = _silu(_causal_conv_chunk(xc_ref, hist_scr, cw_ref, cb_ref, tc, MLSTM_CONV)).astype(BF16)
    blk = wq_ref.shape[1]
    dh = d_inner // MLSTM_HEADS
    q = jnp.concatenate([jnp.dot(xa[:, j * blk:(j + 1) * blk], wq_ref[j], preferred_element_type=F32)
                         for j in range(d_inner // blk)], axis=1)
    k = jnp.concatenate([jnp.dot(xa[:, j * blk:(j + 1) * blk], wk_ref[j], preferred_element_type=F32)
                         for j in range(d_inner // blk)], axis=1) * dh ** -0.5
    causal = _tri(tc)
    gates = gates_ref[...] + gb_ref[...]
    log_f = jnp.minimum(gates, 0.0) - jnp.log1p(jnp.exp(-jnp.abs(gates)))
    f_cum = _cumsum_rows(log_f, causal)
    f_cum_t = _transpose_f32(f_cum)
    gates_t = _transpose_f32(gates)
    for hh in range(MLSTM_HEADS):
        cols = slice(hh * dh, (hh + 1) * dh)
        fh = MLSTM_HEADS + hh
        qh = q[:, cols].astype(BF16)
        kh = k[:, cols]
        vh = v_ref[:, cols]
        fc = f_cum[:, fh:fh + 1]
        i_col = gates[:, hh:hh + 1]
        m0 = ms_ref[0:1, hh:hh + 1]
        d_log = jnp.where(causal, fc - f_cum_t[fh:fh + 1, :] + gates_t[hh:hh + 1, :], -jnp.inf)
        inter = fc + m0
        m = jnp.maximum(jnp.max(d_log, axis=-1, keepdims=True), inter)
        s = lax.dot_general(qh, kh.astype(BF16), NT_DIMS, preferred_element_type=F32) * jnp.exp(d_log - m)
        w_inter = jnp.exp(inter - m)
        c0 = cs_ref[hh]
        n0 = ns_ref[hh:hh + 1, :]
        num = (jnp.dot(s.astype(BF16), vh, preferred_element_type=F32)
               + w_inter * jnp.dot(qh, c0.astype(BF16), preferred_element_type=F32))
        den = (jnp.sum(s, axis=-1, keepdims=True)
               + w_inter * jnp.sum(q[:, cols] * n0, axis=-1, keepdims=True))
        h = num / jnp.maximum(jnp.abs(den), jnp.exp(-m))
        m_end = m[tc - 1:tc, :]
        f_last = fc[tc - 1:tc, :]
        w_end = jnp.exp(f_last - fc + i_col - m_end)
        decay = jnp.exp(f_last + m0 - m_end)
        wk = w_end * kh
        cs_ref[hh] = decay * c0 + lax.dot_general(wk.astype(BF16), vh, TN_DIMS, preferred_element_type=F32)
        ns_ref[hh:hh + 1, :] = decay * n0 + jnp.sum(wk, axis=0, keepdims=True)
        ms_ref[0:1, hh:hh + 1] = m_end
        h = jax.nn.sigmoid(o_ref[:, cols]) * h
        h = h - jnp.mean(h, axis=-1, keepdims=True)
        h = h * lax.rsqrt(jnp.mean(h * h, axis=-1, keepdims=True) + NORM_EPS)
        h_ref[:, cols] = (h * ng_ref[:, cols]).astype(h_ref.dtype)


def mlstm_scan(xc, v, o, gates, conv0, c0, n0, m0, conv_w, conv_b, wq_bd, wk_bd, gate_b, norm_g, *, tc):
    bsz, length, d_inner = xc.shape
    assert length % tc == 0
    seq = lambda w: pl.BlockSpec((None, tc, w), lambda b, c: (b, c, 0))
    per_b = lambda s: pl.BlockSpec((None,) + s, lambda b, c: (b,) + (0,) * len(s))
    row = lambda w: _resident((1, w), lambda b, c: (0, 0))
    return pl.pallas_call(
        functools.partial(_mlstm_kernel, tc=tc, d_inner=d_inner),
        grid=(bsz, length // tc),
        in_specs=[seq(d_inner), seq(d_inner), seq(d_inner), seq(gates.shape[2]), per_b(conv0.shape[1:]),
                  per_b(c0.shape[1:]), per_b(n0.shape[1:]), per_b(m0.shape[1:]),
                  _resident(conv_w.shape, lambda b, c: (0, 0)), row(d_inner),
                  _resident(wq_bd.shape, lambda b, c: (0, 0, 0)), _resident(wk_bd.shape, lambda b, c: (0, 0, 0)),
                  row(gates.shape[2]), row(d_inner)],
        out_specs=[seq(d_inner), per_b(c0.shape[1:]), per_b(n0.shape[1:]), per_b(m0.shape[1:])],
        out_shape=[jax.ShapeDtypeStruct((bsz, length, d_inner), BF16), jax.ShapeDtypeStruct(c0.shape, F32),
                   jax.ShapeDtypeStruct(n0.shape, F32), jax.ShapeDtypeStruct(m0.shape, F32)],
        scratch_shapes=[pltpu.VMEM((HIST + tc, d_inner), F32)],
        compiler_params=_params("parallel", "arbitrary"),
        name="mlstm_scan",
    )(xc, v, o, gates, conv0, c0, n0, m0, conv_w, conv_b, wq_bd, wk_bd, gate_b, norm_g)


def _block_diag(w_blk, blk):
    n, c, d = w_blk.shape
    per = blk // c
    eye = jnp.eye(per, dtype=w_blk.dtype)
    tiles = w_blk.reshape(n // per, per, c, d)
    return jnp.einsum("jpcd,pq->jpcqd", tiles, eye).reshape(n // per, per * c, per * d)


def _mixer_a(xp, xs, shapes, cache_k, cache_v, cache_ki, w_in, t5_table):
    (bp, lp_), (bs, ls) = shapes
    hd, kvd, idd = A_HEADS * A_HEAD_DIM, A_KV_HEADS * A_HEAD_DIM, IDX_HEADS * IDX_DIM
    w_pad, cols = _pad_cols(w_in, (hd, kvd, kvd, idd, IDX_DIM, IDX_HEADS))
    outs = [(cols[0], BF16, A_HEAD_DIM ** -0.5), (cols[1], F32), (cols[1], BF16), (cols[2], F32), (cols[2], BF16),
            (cols[3], BF16), (cols[4], F32), (cols[4], BF16), (cols[5], F32)]
    w_pad = w_pad.astype(BF16)
    bias = _t5_bias_tiles(t5_table)
    keys_on_lanes = lambda a: jnp.swapaxes(a, 1, 2)

    q, k, kb, v, vb, qi, ki, kib, wi = project(xp, w_pad, outs, name="project_a")
    r3 = lambda a, b_, l: a.reshape(b_, l, a.shape[-1])
    att_p = dsa_attention(r3(q, bp, lp_), r3(qi, bp, lp_), r3(wi, bp, lp_), keys_on_lanes(r3(kb, bp, lp_)),
                          r3(vb, bp, lp_), keys_on_lanes(r3(kib, bp, lp_)), bias, tq=KEY_TILE, q_off=0, l_true=lp_,
                          n_sel=min(IDX_TOPK_MAX, lp_ // 4))
    outs_p = (k.reshape(bp, lp_, A_KV_HEADS, A_HEAD_DIM), v.reshape(bp, lp_, A_KV_HEADS, A_HEAD_DIM),
              ki.reshape(bp, lp_, IDX_DIM))

    q, k, kb, v, vb, qi, ki, kib, wi = project(xs, w_pad, outs, name="project_a")
    past = cache_k.shape[1]
    total = past + ls
    lpad = -(-total // (2 * KEY_TILE)) * (2 * KEY_TILE)

    def with_past(cache, new):
        parts = [cache.reshape(bs, past, -1).astype(BF16), r3(new, bs, ls)]
        if lpad != total:
            parts.append(jnp.zeros((bs, lpad - total, new.shape[-1]), BF16))
        return jnp.concatenate(parts, axis=1)

    att_s = dsa_attention(r3(q, bs, ls), r3(qi, bs, ls), r3(wi, bs, ls), keys_on_lanes(with_past(cache_k, kb)),
                          with_past(cache_v, vb), keys_on_lanes(with_past(cache_ki, kib)), bias, tq=ls, q_off=past,
                          l_true=total, n_sel=min(IDX_TOPK_MAX, total // 4))
    outs_s = (k.reshape(bs, ls, A_KV_HEADS, A_HEAD_DIM), v.reshape(bs, ls, A_KV_HEADS, A_HEAD_DIM),
              ki.reshape(bs, ls, IDX_DIM))
    return att_p.reshape(bp * lp_, hd), att_s.reshape(bs * ls, hd), outs_p, outs_s


def _mixer_b(xp, xs, shapes, cache_k, cache_v, w_in, rel_table):
    (bp, lp_), (bs, ls) = shapes
    hd = BAND_HEADS * BAND_HEAD_DIM
    w_pad, cols = _pad_cols(w_in, (hd, hd, hd))
    outs = [(cols[0], BF16), (cols[1], F32), (cols[1], BF16), (cols[2], F32), (cols[2], BF16)]
    w_pad = w_pad.astype(BF16)

    q, k, kb, v, vb = project(xp, w_pad, outs, name="project_b")
    r3 = lambda a, b_, l: a.reshape(b_, l, hd)
    att_p = band_attention_prompt(r3(q, bp, lp_), r3(kb, bp, lp_), r3(vb, bp, lp_),
                                  _band_bias(rel_table, CHUNK, BAND_ROWS), grp=BAND_ROWS)
    keep = min(BAND_WINDOW, lp_)
    heads = lambda a, b_, l: a.reshape(b_, l, BAND_HEADS, BAND_HEAD_DIM)
    outs_p = (heads(k, bp, lp_)[:, lp_ - keep:], heads(v, bp, lp_)[:, lp_ - keep:])

    q, k, kb, v, vb = project(xs, w_pad, outs, name="project_b")
    past = cache_k.shape[1]
    att_s = band_attention_sample(r3(q, bs, ls), cache_k.reshape(bs, past, hd).astype(BF16), r3(kb, bs, ls),
                                  cache_v.reshape(bs, past, hd).astype(BF16), r3(vb, bs, ls),
                                  _band_bias(rel_table, ls, ls))
    outs_s = (jnp.concatenate([cache_k, heads(k, bs, ls)], axis=1)[:, ls:],
              jnp.concatenate([cache_v, heads(v, bs, ls)], axis=1)[:, ls:])
    return att_p.reshape(bp * lp_, hd), att_s.reshape(bs * ls, hd), outs_p, outs_s


def _mixer_c(xp, xs, shapes, ssm0, conv0, w_in, conv_w, conv_b, dt_bias, a_log, d_skip, norm_g):
    n_heads = a_log.shape[0]
    d_inner = n_heads * SSD_HEAD_DIM
    conv_dim = conv_w.shape[1]
    w_pad, cols = _pad_cols(w_in, (d_inner, conv_dim, n_heads))
    outs = [(cols[0], F32), (cols[1], F32), (cols[2], F32)]
    w_pad = w_pad.astype(BF16)
    row = lambda a: a.reshape(1, -1)
    d_skip_cols = jnp.repeat(d_skip, SSD_HEAD_DIM).reshape(1, d_inner)
    res = []
    for x, (b_, l), h0, c0 in ((xp, shapes[0], None, None), (xs, shapes[1], ssm0, conv0)):
        if h0 is None:
            h0 = jnp.zeros((b_, n_heads, SSD_HEAD_DIM, SSD_D_STATE), F32)
            c0 = jnp.zeros((b_, SSD_CONV - 1, conv_dim), F32)
        z, xbc, dt = project(x, w_pad, outs, name="project_c")
        xbc3 = xbc.reshape(b_, l, conv_dim)
        y, h_new = ssd_scan(xbc3, z.reshape(b_, l, d_inner), dt.reshape(b_, l, n_heads), c0, h0, conv_w,
                            row(conv_b), row(dt_bias), row(a_log), d_skip_cols, row(norm_g), tc=min(SSD_ROWS, l))
        conv_new = jnp.concatenate([c0, xbc3], axis=1)[:, l:]
        res.append((y.reshape(b_ * l, d_inner), (h_new, conv_new)))
    return res[0][0], res[1][0], res[0][1], res[1][1]


def _mixer_d(xp, xs, shapes, c0s, n0s, m0s, conv0s, w_in, conv_w, conv_b, wq_blk, wk_blk, gate_b, norm_g):
    d_inner = conv_w.shape[1]
    dh = d_inner // MLSTM_HEADS
    w_pad, cols = _pad_cols(w_in, (d_inner, d_inner, d_inner, 2 * MLSTM_HEADS))
    gcol = (cols[3][0], cols[3][1], cols[3][1])
    outs = [(cols[0], F32), (cols[1], BF16), (cols[2], F32), (gcol, F32)]
    w_pad = w_pad.astype(BF16)
    blk = 2 * LANES
    wq_bd = _block_diag(wq_blk, blk).astype(BF16)
    wk_bd = _block_diag(wk_blk, blk).astype(BF16)
    gate_b_pad = jnp.zeros((1, gcol[1]), F32).at[0, :2 * MLSTM_HEADS].set(gate_b)
    row = lambda a: a.reshape(1, -1)
    res = []
    for x, (b_, l), st in ((xp, shapes[0], None), (xs, shapes[1], (c0s, n0s, m0s, conv0s))):
        if st is None:
            st = (jnp.zeros((b_, MLSTM_HEADS, dh, dh), F32), jnp.zeros((b_, MLSTM_HEADS, dh), F32),
                  jnp.zeros((b_, MLSTM_HEADS), F32), jnp.zeros((b_, MLSTM_CONV - 1, d_inner), F32))
        c0, n0, m0, conv0 = st
        xc, v, o, gates = project(x, w_pad, outs, name="project_d")
        r3 = lambda a: a.reshape(b_, l, a.shape[-1])
        h, c_new, n_new, m_new = mlstm_scan(r3(xc), r3(v), r3(o), r3(gates), conv0, c0, n0,
                                            m0.reshape(b_, 1, MLSTM_HEADS), conv_w, row(conv_b), wq_bd, wk_bd,
                                            gate_b_pad, row(norm_g), tc=min(MLSTM_ROWS, l))
        conv_new = jnp.concatenate([conv0, r3(xc)], axis=1)[:, l:]
        res.append((h.reshape(b_ * l, d_inner), (c_new, n_new, m_new.reshape(b_, MLSTM_HEADS), conv_new)))
    return res[0][0], res[1][0], res[0][1], res[1][1]


def kernel(x_prompt, x_sample, cache_a_k, cache_a_v, cache_a_kidx, cache_b_k, cache_b_v, state_c_ssm, state_c_conv, state_d_c, state_d_n, state_d_m, state_d_conv, a_w_in, a_w_out, t5_table, b_w_in, b_w_out, b_rel_table, c_w_in, c_conv_w, c_conv_b, c_dt_bias, c_a_log, c_d_skip, c_norm_g, c_w_out, d_w_in, d_conv_w, d_conv_b, d_wq_blk, d_wk_blk, d_gate_b, d_norm_g, d_w_out, ffn1_wg, ffn1_wu, ffn1_wd, ffn2_wg, ffn2_wu, ffn2_wd, ln_g, ln_b):
    bp, lp_, d = x_prompt.shape
    bs, ls, _ = x_sample.shape
    depth = ffn1_wg.shape[0]
    alpha = (2.0 * depth) ** 0.25
    shapes = ((bp, lp_), (bs, ls))
    xp = x_prompt.reshape(bp * lp_, d)
    xs = x_sample.reshape(bs * ls, d)
    ffn_w = [[w.astype(BF16) for w in ws] for ws in ((ffn1_wg, ffn1_wu, ffn1_wd), (ffn2_wg, ffn2_wu, ffn2_wd))]
    w_out = [w.astype(BF16) for w in (a_w_out, b_w_out, c_w_out, d_w_out)]
    row = lambda a: a.reshape(1, -1)
    extra = {}
    for i in range(depth):
        g, b = ln_g[i], ln_b[i]
        xp = ffn_postnorm(xp, *ffn_w[0], i, row(g[0]), row(b[0]), alpha)
        xs = ffn_postnorm(xs, *ffn_w[0], i, row(g[0]), row(b[0]), alpha)
        kind = i % 4
        if kind == 0:
            mp, ms, op, os_ = _mixer_a(xp, xs, shapes, cache_a_k, cache_a_v, cache_a_kidx, a_w_in, t5_table)
        elif kind == 1:
            mp, ms, op, os_ = _mixer_b(xp, xs, shapes, cache_b_k, cache_b_v, b_w_in, b_rel_table)
        elif kind == 2:
            mp, ms, op, os_ = _mixer_c(xp, xs, shapes, state_c_ssm, state_c_conv, c_w_in, c_conv_w, c_conv_b,
                                       c_dt_bias, c_a_log, c_d_skip, c_norm_g)
        else:
            mp, ms, op, os_ = _mixer_d(xp, xs, shapes, state_d_c, state_d_n, state_d_m, state_d_conv, d_w_in,
                                       d_conv_w, d_conv_b, d_wq_blk, d_wk_blk, d_gate_b, d_norm_g)
        extra[kind] = (op, os_)
        xp = mixout_ffn(xp, mp, w_out[kind], row(g[1]), row(b[1]), *ffn_w[1], i, row(g[2]), row(b[2]), alpha)
        xs = mixout_ffn(xs, ms, w_out[kind], row(g[1]), row(b[1]), *ffn_w[1], i, row(g[2]), row(b[2]), alpha)
    prompt_side = tuple(t for kind in range(4) for t in extra[kind][0])
    sample_side = tuple(t for kind in range(4) for t in extra[kind][1])
    return (xp.reshape(bp, lp_, d), xs.reshape(bs, ls, d)) + prompt_side + sample_side
```

```python
import functools
import math

import numpy as np
import jax
import jax.numpy as jnp
from jax import lax
from jax.experimental import pallas as pl
from jax.experimental.pallas import tpu as pltpu

F32 = jnp.float32
BF16 = jnp.bfloat16
I32 = jnp.int32

CHUNK = 64
NORM_EPS = 1e-5
A_HEADS, A_KV_HEADS, A_HEAD_DIM = 8, 2, 128
IDX_HEADS, IDX_DIM, IDX_TOPK_MAX = 8, 64, 256
T5_BUCKETS, T5_MAX_DIST = 32, 128
BAND_HEADS, BAND_HEAD_DIM, BAND_LEFT_CHUNKS, BAND_MAX_REL = 16, 64, 8, 128
BAND_WINDOW = BAND_LEFT_CHUNKS * CHUNK
SSD_HEAD_DIM, SSD_GROUPS, SSD_D_STATE, SSD_CONV = 64, 8, 128, 4
MLSTM_HEADS, MLSTM_CONV, MLSTM_QK_BLOCK = 4, 4, 4

LANES = 128
NEG_BIG = -1e30
VMEM_LIMIT = 56 * 1024 * 1024

NT_DIMS = (((1,), (1,)), ((), ()))
TN_DIMS = (((0,), (0,)), ((), ()))


def _params(*sem):
    return pltpu.CompilerParams(dimension_semantics=sem, vmem_limit_bytes=VMEM_LIMIT)


def _resident(shape, index_map):
    return pl.BlockSpec(shape, index_map, pipeline_mode=pl.Buffered(1))


def _layer_norm(y, g, b):
    mu = jnp.mean(y, axis=-1, keepdims=True)
    yc = y - mu
    var = jnp.mean(yc * yc, axis=-1, keepdims=True)
    return yc * lax.rsqrt(var + NORM_EPS) * g + b


def _silu(x):
    return x * jax.nn.sigmoid(x)


def _row_tile(n, want):
    t = min(n, want)
    assert n % t == 0, (n, t)
    return t


def _ffn_block(x, wg_ref, wu_ref, wd_ref, g_ref, b_ref, alpha, f_cuts):
    xb = x.astype(BF16)
    acc = jnp.zeros(x.shape, F32)
    for lo, hi in zip(f_cuts[:-1], f_cuts[1:]):
        sl = slice(lo, hi)
        gate = jnp.dot(xb, wg_ref[:, sl], preferred_element_type=F32)
        up = jnp.dot(xb, wu_ref[:, sl], preferred_element_type=F32)
        h = (_silu(gate) * up).astype(BF16)
        acc = acc + jnp.dot(h, wd_ref[sl, :], preferred_element_type=F32)
    return _layer_norm(alpha * x + 0.5 * acc, g_ref[...], b_ref[...])


def _ffn_kernel(x_ref, wg_ref, wu_ref, wd_ref, g_ref, b_ref, o_ref, *, alpha, f_cuts):
    o_ref[...] = _ffn_block(x_ref[...], wg_ref, wu_ref, wd_ref, g_ref, b_ref, alpha, f_cuts)


def _mixout_ffn_kernel(x_ref, m_ref, wo_ref, g1_ref, b1_ref, wg_ref, wu_ref, wd_ref, g2_ref, b2_ref, o_ref, *,
                       alpha, f_cuts):
    sub = jnp.dot(m_ref[...], wo_ref[...], preferred_element_type=F32)
    x1 = _layer_norm(alpha * x_ref[...] + sub, g1_ref[...], b1_ref[...])
    o_ref[...] = _ffn_block(x1, wg_ref, wu_ref, wd_ref, g2_ref, b2_ref, alpha, f_cuts)


MXU_WIDTH = 256


def _ffn_cuts(d_ff, n_chunks):
    tiles = -(-d_ff // MXU_WIDTH)
    cuts = [min(d_ff, MXU_WIDTH * (-(-tiles * c // n_chunks))) for c in range(n_chunks + 1)]
    return tuple(cuts)


def ffn_postnorm(x, wg, wu, wd, layer, g, b, alpha, tm=512, n_chunks=2):
    n, d = x.shape
    d_ff = wg.shape[2]
    tm = _row_tile(n, tm)
    return pl.pallas_call(
        functools.partial(_ffn_kernel, alpha=alpha, f_cuts=_ffn_cuts(d_ff, n_chunks)),
        grid=(n // tm,),
        in_specs=[
            pl.BlockSpec((tm, d), lambda i: (i, 0)),
            _resident((None, d, d_ff), lambda i: (layer, 0, 0)),
            _resident((None, d, d_ff), lambda i: (layer, 0, 0)),
            _resident((None, d_ff, d), lambda i: (layer, 0, 0)),
            _resident((1, d), lambda i: (0, 0)),
            _resident((1, d), lambda i: (0, 0)),
        ],
        out_specs=pl.BlockSpec((tm, d), lambda i: (i, 0)),
        out_shape=jax.ShapeDtypeStruct((n, d), F32),
        compiler_params=_params("parallel"),
        name="ffn_postnorm",
    )(x, wg, wu, wd, g, b)


def mixout_ffn(x, m, w_out, g1, b1, wg, wu, wd, layer, g2, b2, alpha, tm=512, n_chunks=2):
    n, d = x.shape
    k = m.shape[1]
    d_ff = wg.shape[2]
    tm = _row_tile(n, tm)
    vec = lambda: _resident((1, d), lambda i: (0, 0))
    return pl.pallas_call(
        functools.partial(_mixout_ffn_kernel, alpha=alpha, f_cuts=_ffn_cuts(d_ff, n_chunks)),
        grid=(n // tm,),
        in_specs=[
            pl.BlockSpec((tm, d), lambda i: (i, 0)),
            pl.BlockSpec((tm, k), lambda i: (i, 0)),
            _resident((k, d), lambda i: (0, 0)), vec(), vec(),
            _resident((None, d, d_ff), lambda i: (layer, 0, 0)),
            _resident((None, d, d_ff), lambda i: (layer, 0, 0)),
            _resident((None, d_ff, d), lambda i: (layer, 0, 0)), vec(), vec(),
        ],
        out_specs=pl.BlockSpec((tm, d), lambda i: (i, 0)),
        out_shape=jax.ShapeDtypeStruct((n, d), F32),
        compiler_params=_params("parallel"),
        name="mixout_ffn",
    )(x, m, w_out, g1, b1, wg, wu, wd, g2, b2)


def _proj_kernel(x_ref, w_ref, *o_refs, cols, scales):
    xb = x_ref[...].astype(BF16)
    done = {}
    for (off, pad_w, true_w), scale, o_ref in zip(cols, scales, o_refs):
        if (off, pad_w) not in done:
            done[(off, pad_w)] = jnp.dot(xb, w_ref[:, off:off + pad_w], preferred_element_type=F32)
        y = done[(off, pad_w)][:, :true_w]
        o_ref[...] = (y if scale is None else y * scale).astype(o_ref.dtype)


def _pad_cols(w, widths):
    pieces, offs, off, src = [], [], 0, 0
    for wd in widths:
        pad_w = -(-wd // LANES) * LANES
        pieces.append(w[:, src:src + wd])
        if pad_w != wd:
            pieces.append(jnp.zeros((w.shape[0], pad_w - wd), w.dtype))
        offs.append((off, pad_w, wd))
        off += pad_w
        src += wd
    return jnp.concatenate(pieces, axis=1), offs


def project(x, w_pad, outs, tm=512, name="project"):
    n, d = x.shape
    tm = _row_tile(n, tm)
    cols = tuple(o[0] for o in outs)
    scales = tuple(o[2] if len(o) > 2 else None for o in outs)
    outs = [o[:2] for o in outs]
    return pl.pallas_call(
        functools.partial(_proj_kernel, cols=cols, scales=scales),
        grid=(n // tm,),
        in_specs=[pl.BlockSpec((tm, d), lambda i: (i, 0)),
                  _resident(w_pad.shape, lambda i: (0, 0))],
        out_specs=[pl.BlockSpec((tm, c[2]), lambda i: (i, 0)) for c in cols],
        out_shape=[jax.ShapeDtypeStruct((n, c[2]), dt) for c, dt in outs],
        compiler_params=_params("parallel"),
        name=name,
    )(x, w_pad)


def _t5_bucket_np(rel):
    half = T5_BUCKETS // 2
    max_exact = half // 2
    n = np.abs(rel)
    nf = np.maximum(n, 1).astype(np.float32)
    large = max_exact + (np.log(nf / np.float32(max_exact)) / np.float32(math.log(T5_MAX_DIST / max_exact))
                         * np.float32(half - max_exact)).astype(np.int32)
    large = np.minimum(large, half - 1)
    return np.where(rel > 0, half, 0) + np.where(n < max_exact, n, large)


KEY_TILE = 2 * LANES


def _toeplitz(w, t_rows, s_cols):
    n = t_rows + s_cols
    lead = w.shape[:-1]
    wp = jnp.concatenate([w, jnp.zeros(lead + (1,), w.dtype)], axis=-1)
    flat = jnp.broadcast_to(wp[..., None, :], lead + (t_rows, n)).reshape(lead + (t_rows * n,))
    skew = flat[..., :t_rows * (n - 1)].reshape(lead + (t_rows, n - 1))
    return skew[..., t_rows - 1:t_rows - 1 + s_cols]


def _t5_bias_tiles(t5_table):
    assert T5_MAX_DIST <= KEY_TILE
    rel = np.arange(3 * KEY_TILE - 1) - (KEY_TILE - 1) - KEY_TILE
    far = t5_table[int(_t5_bucket_np(np.array(-T5_MAX_DIST)))]
    by_rel = (t5_table[_t5_bucket_np(rel)] - far).T
    strip = _toeplitz(by_rel, KEY_TILE, 2 * KEY_TILE)
    return jnp.stack([strip[:, :, KEY_TILE:], strip[:, :, :KEY_TILE]])


def _dsa_kernel(q_ref, qit_ref, wit_ref, kt_ref, v_ref, ki_ref, bias_ref, o_ref,
                keyt_scr, mask_scr, lg_scr, m_scr, l_scr, acc_scr, *, tq, q_off, l_true, n_sel, lp):
    j = pl.program_id(1)
    q0 = q_off + j * tq
    kend = jnp.minimum(((q0 + tq - 1) // CHUNK + 1) * CHUNK, l_true)
    n_tiles = (kend + KEY_TILE - 1) // KEY_TILE
    n_far = jnp.maximum(q0 // KEY_TILE - 1, 0)
    int_min = jnp.int32(-2 ** 31)
    wide = 2 * KEY_TILE

    def tile_off(kt):
        return pl.multiple_of(kt * KEY_TILE, KEY_TILE)

    def slabs(lo, hi, fn):
        def pair(i, carry):
            fn(tile_off(lo + 2 * i), wide)
            return carry
        lax.fori_loop(0, (hi - lo) // 2, pair, 0)

        @pl.when((hi - lo) % 2 == 1)
        def _():
            fn(tile_off(hi - 1), KEY_TILE)

    wi = wit_ref[...] * (IDX_HEADS ** -0.5) * (IDX_DIM ** -0.5)

    def positions(off, n_keys):
        kpos = off + lax.broadcasted_iota(I32, (n_keys, tq), 0)
        q_chunk = (q0 + lax.broadcasted_iota(I32, (n_keys, tq), 1)) // CHUNK
        return kpos, ((kpos // CHUNK) <= q_chunk) & (kpos < l_true)

    def to_key(x):
        bits = lax.bitcast_convert_type(x, I32)
        return jnp.where(bits < 0, (bits ^ jnp.int32(0x7FFFFFFF)) + 1, bits)

    last_off = tile_off(n_tiles - 1)

    def score_slab(off, width, check_admissible=False):
        ki_t = ki_ref[pl.ds(off, width), :]
        accs = [jnp.zeros((width, tq), F32)] * 2
        for h in range(IDX_HEADS):
            s = jnp.dot(ki_t, qit_ref[h * IDX_DIM:(h + 1) * IDX_DIM, :], preferred_element_type=F32)
            accs[h % 2] = accs[h % 2] + jnp.maximum(s, 0.0) * wi[h:h + 1, :]
        sc = accs[0] + accs[1]
        if check_admissible:
            sc = jnp.where(positions(off, width)[1], sc, -jnp.inf)
        keyt_scr[pl.ds(off, width), :] = to_key(sc)

    slabs(0, n_tiles - 1, score_slab)
    score_slab(last_off, KEY_TILE, check_admissible=True)

    @pl.when(n_tiles % 2 == 1)
    def _():
        keyt_scr[pl.ds(tile_off(n_tiles), KEY_TILE), :] = to_key(jnp.full((KEY_TILE, tq), -jnp.inf, F32))

    sublanes = 8

    def count(pred):
        def body(ct, acc):
            off = pl.multiple_of(ct * wide, wide)
            kpos = off + lax.broadcasted_iota(I32, (wide, tq), 0)
            hit = pred(keyt_scr[pl.ds(off, wide), :], kpos).astype(I32)
            return acc + jnp.sum(hit.reshape(wide // sublanes, sublanes, tq), axis=0)
        acc = lax.fori_loop(0, (n_tiles + 1) // 2, body, jnp.zeros((sublanes, tq), I32))
        return jnp.sum(acc.astype(F32), axis=0, keepdims=True)

    check_every = 4

    def search_body(state):
        i, thr_u, settled, _ = state

        def one_bit(b, st):
            thr_u, settled = st
            cand_u = thr_u | lax.shift_left(jnp.int32(1), 31 - (i + b))
            cand = cand_u ^ int_min
            cnt = count(lambda key, kpos: key >= cand)
            thr_u = jnp.where((cnt >= n_sel) & (settled == 0), cand_u, thr_u)
            return thr_u, jnp.where(cnt == n_sel, 1, settled)

        thr_u, settled = lax.fori_loop(0, check_every, one_bit, (thr_u, settled))
        return i + check_every, thr_u, settled, jnp.min(settled.astype(F32))

    zeros = jnp.zeros((1, tq), I32)
    _, thr_u, _, _ = lax.while_loop(lambda s: (s[0] < 32) & (s[3] == 0.0), search_body,
                                    (jnp.int32(0), zeros, zeros, jnp.float32(0.0)))
    thr = thr_u ^ int_min
    need = n_sel - count(lambda key, kpos: key > thr)
    n_eq = count(lambda key, kpos: key == thr)
    has_tie = jnp.max(n_eq - need) > 0.0

    nbits = int(lp).bit_length()

    def tie_cut():
        def cut_body(i, cut):
            cand = cut | lax.shift_left(jnp.int32(1), nbits - 1 - i)
            cnt = count(lambda key, kpos: (key == thr) & (kpos < cand))
            return jnp.where(cnt <= need, cand, cut)
        return lax.fori_loop(0, nbits, cut_body, jnp.zeros((1, tq), I32))

    cut = lax.cond(has_tie, tie_cut, lambda: jnp.full((1, tq), 2 ** 30, I32))

    eye = (lax.broadcasted_iota(I32, (tq, tq), 0) == lax.broadcasted_iota(I32, (tq, tq), 1)).astype(BF16)

    def mask_tile(off, tie, check_admissible=False):
        key = keyt_scr[pl.ds(off, KEY_TILE), :]
        kpos, adm = positions(off, KEY_TILE)
        sel = ((key > thr) | ((key == thr) & (kpos < cut))) if tie else (key >= thr)
        if check_admissible:
            sel = sel & adm
        sel_q = lax.dot_general(eye, jnp.where(sel, 1.0, 0.0).astype(BF16), NT_DIMS, preferred_element_type=F32)
        mask_scr[:, pl.ds(off, KEY_TILE)] = jnp.where(sel_q > 0.5, 0.0, NEG_BIG)

    def mask_all(tie):
        def slab(off, width):
            for u in range(width // KEY_TILE):
                mask_tile(pl.multiple_of(off + u * KEY_TILE, KEY_TILE), tie)

        def run():
            slabs(0, n_tiles - 1, slab)
            mask_tile(last_off, tie, check_admissible=True)
        return run

    lax.cond(has_tie, mask_all(True), mask_all(False))

    grp = A_HEADS // A_KV_HEADS
    rows = grp * tq
    head_cols = lambda h: slice(h * A_HEAD_DIM, (h + 1) * A_HEAD_DIM)
    for g in range(A_KV_HEADS):
        qg = jnp.concatenate([q_ref[:, head_cols(g * grp + i)] for i in range(grp)], axis=0)
        m_scr[...] = jnp.full(m_scr.shape, NEG_BIG, F32)

        def qk_slab(off, width, near=False, qg=qg, g=g):
            lg = jnp.dot(qg, kt_ref[head_cols(g), pl.ds(off, width)], preferred_element_type=F32)
            lg = lg.reshape(grp, tq, width) + mask_scr[:, pl.ds(off, width)][None]
            if near:
                back = jnp.clip((q0 - off) // KEY_TILE, 0, 1)
                lg = lg + bias_ref[back, g * grp:(g + 1) * grp, 0:tq, :]
            lg = lg.reshape(rows, width)
            lg_scr[:, pl.ds(off, width)] = lg
            mx = lg[:, 0:LANES]
            for u in range(1, width // LANES):
                mx = jnp.maximum(mx, lg[:, u * LANES:(u + 1) * LANES])
            m_scr[...] = jnp.maximum(m_scr[...], mx)

        def near_body(kt, carry, qk_slab=qk_slab):
            qk_slab(tile_off(kt), KEY_TILE, near=True)
            return carry

        slabs(0, n_far, qk_slab)
        lax.fori_loop(n_far, n_tiles, near_body, 0)
        m_scr[...] = jnp.broadcast_to(jnp.max(m_scr[...], axis=-1, keepdims=True), m_scr.shape)
        l_scr[...] = jnp.zeros(l_scr.shape, F32)
        acc_scr[...] = jnp.zeros(acc_scr.shape, F32)

        def pv_slab(off, width, g=g):
            lg = lg_scr[:, pl.ds(off, width)]
            m_b = m_scr[...]
            ps = [jnp.exp(lg[:, u * LANES:(u + 1) * LANES] - m_b) for u in range(width // LANES)]
            l_scr[...] += functools.reduce(lambda a, b: a + b, ps)
            p = jnp.concatenate(ps, axis=1).astype(BF16)
            acc_scr[...] += jnp.dot(p, v_ref[pl.ds(off, width), head_cols(g)], preferred_element_type=F32)

        slabs(0, n_tiles, pv_slab)
        out = acc_scr[...] / jnp.sum(l_scr[...], axis=-1, keepdims=True)
        for i in range(grp):
            o_ref[:, head_cols(g * grp + i)] = out[i * tq:(i + 1) * tq].astype(o_ref.dtype)


def dsa_attention(q, qi_t, wi_t, k_t, v, ki, bias, *, tq, q_off, l_true, n_sel):
    bsz, t_len, _ = q.shape
    lp = v.shape[1]
    assert lp % (2 * KEY_TILE) == 0 and t_len % tq == 0 and q_off % KEY_TILE == 0
    assert tq == KEY_TILE or (t_len == tq and tq < KEY_TILE)
    grp = A_HEADS // A_KV_HEADS
    qspec = lambda w: pl.BlockSpec((None, tq, w), lambda b, j: (b, j, 0))
    qtspec = lambda h: pl.BlockSpec((None, h, tq), lambda b, j: (b, 0, j))
    whole = lambda a: pl.BlockSpec((None,) + a.shape[1:], lambda b, j: (b, 0, 0))
    return pl.pallas_call(
        functools.partial(_dsa_kernel, tq=tq, q_off=q_off, l_true=l_true, n_sel=n_sel, lp=lp),
        grid=(bsz, t_len // tq),
        in_specs=[qspec(q.shape[2]), qtspec(qi_t.shape[1]), qtspec(wi_t.shape[1]),
                  whole(k_t), whole(v), whole(ki),
                  _resident(bias.shape, lambda b, j: (0, 0, 0, 0))],
        out_specs=qspec(q.shape[2]),
        out_shape=jax.ShapeDtypeStruct(q.shape, BF16),
        scratch_shapes=[pltpu.VMEM((lp, tq), I32), pltpu.VMEM((tq, lp), F32), pltpu.VMEM((grp * tq, lp), F32),
                        pltpu.VMEM((grp * tq, LANES), F32), pltpu.VMEM((grp * tq, LANES), F32),
                        pltpu.VMEM((grp * tq, A_HEAD_DIM), F32)],
        compiler_params=_params("parallel", "arbitrary"),
        name="dsa_attention",
    )(q, qi_t, wi_t, k_t, v, ki, bias)


BAND_ROWS = 4 * CHUNK


def _band_bias(rel_table, chunk, grp):
    win = BAND_WINDOW + grp
    t = np.arange(grp)[:, None]
    s = np.arange(win)[None, :]
    k = np.arange(grp + win - 1)
    idx = np.clip(BAND_WINDOW + (grp - 1) - k, -BAND_MAX_REL, BAND_MAX_REL) + BAND_MAX_REL
    bias = _toeplitz(rel_table[:, idx], grp, win)
    lo = (t // chunk) * chunk
    allowed = (s >= lo) & (s < lo + BAND_WINDOW + chunk)
    return jnp.where(allowed[None], bias, NEG_BIG)


def _band_kernel(q_ref, kp_ref, kc_ref, vp_ref, vc_ref, bias_ref, o_ref, kcat, vcat, *, grp, first_has_no_past):
    i = pl.program_id(1)
    prev = kp_ref.shape[0]
    tq = q_ref.shape[0]
    win = prev + grp
    kcat[0:prev, :] = kp_ref[...]
    kcat[prev:prev + tq, :] = kc_ref[...]
    vcat[0:prev, :] = vp_ref[...]
    vcat[prev:prev + tq, :] = vc_ref[...]
    scale = BAND_HEAD_DIM ** -0.5
    lane = lax.broadcasted_iota(I32, (grp, win), 1)
    pair_w = 2 * BAND_HEAD_DIM
    assert pair_w == LANES
    first_half = lax.broadcasted_iota(I32, (grp, pair_w), 1) < BAND_HEAD_DIM
    for r0 in range(0, tq, grp):
        if first_has_no_past:
            kpos_ok = (i * tq - prev + r0 + lane) >= 0
        for hp in range(BAND_HEADS // 2):
            cols = slice(hp * pair_w, (hp + 1) * pair_w)
            q_pair = q_ref[r0:r0 + grp, cols]
            k_pair = kcat[r0:r0 + win, cols]
            v_pair = vcat[r0:r0 + win, cols]
            outs = []
            for side in range(2):
                q_one = jnp.where(first_half == (side == 0), q_pair, jnp.zeros_like(q_pair))
                lg = lax.dot_general(q_one, k_pair, NT_DIMS, preferred_element_type=F32) * scale
                lg = lg + bias_ref[2 * hp + side]
                if first_has_no_past:
                    lg = jnp.where(kpos_ok, lg, NEG_BIG)
                p = jnp.exp(lg - jnp.max(lg, axis=-1, keepdims=True))
                den = jnp.sum(p, axis=-1, keepdims=True)
                outs.append(jnp.dot(p.astype(BF16), v_pair, preferred_element_type=F32) / den)
            o_ref[r0:r0 + grp, cols] = jnp.where(first_half, outs[0], outs[1]).astype(o_ref.dtype)


def band_attention_prompt(q, k, v, bias, *, tq=512, grp=128):
    bsz, length, w = q.shape
    assert tq == BAND_WINDOW and length % tq == 0
    cur = pl.BlockSpec((None, tq, w), lambda b, i: (b, i, 0))
    prv = pl.BlockSpec((None, tq, w), lambda b, i: (b, jnp.maximum(i - 1, 0), 0))
    return pl.pallas_call(
        functools.partial(_band_kernel, grp=grp, first_has_no_past=True),
        grid=(bsz, length // tq),
        in_specs=[cur, prv, cur, prv, cur, _resident(bias.shape, lambda b, i: (0, 0, 0))],
        out_specs=cur,
        out_shape=jax.ShapeDtypeStruct(q.shape, BF16),
        scratch_shapes=[pltpu.VMEM((2 * tq, w), BF16), pltpu.VMEM((2 * tq, w), BF16)],
        compiler_params=_params("parallel", "arbitrary"),
        name="band_attention",
    )(q, k, k, v, v, bias)


def band_attention_sample(q, k_past, k_new, v_past, v_new, bias):
    bsz, t_len, w = q.shape
    past = k_past.shape[1]
    assert past == BAND_WINDOW
    new = pl.BlockSpec((None, t_len, w), lambda b, i: (b, 0, 0))
    old = pl.BlockSpec((None, past, w), lambda b, i: (b, 0, 0))
    return pl.pallas_call(
        functools.partial(_band_kernel, grp=t_len, first_has_no_past=False),
        grid=(bsz, 1),
        in_specs=[new, old, new, old, new, _resident(bias.shape, lambda b, i: (0, 0, 0))],
        out_specs=new,
        out_shape=jax.ShapeDtypeStruct(q.shape, BF16),
        scratch_shapes=[pltpu.VMEM((past + t_len, w), BF16), pltpu.VMEM((past + t_len, w), BF16)],
        compiler_params=_params("parallel", "arbitrary"),
        name="band_attention_sample",
    )(q, k_past, k_new, v_past, v_new, bias)


HIST = 8
SSD_ROWS = 128
MLSTM_ROWS = 256


def _causal_conv_chunk(x_ref, hist_scr, w_ref, b_ref, tc, width):
    hist_scr[HIST:HIST + tc, :] = x_ref[...]
    base = HIST - (width - 1)
    out = b_ref[...] + hist_scr[base:base + tc, :] * w_ref[0:1, :]
    for j in range(1, width):
        out = out + hist_scr[base + j:base + j + tc, :] * w_ref[j:j + 1, :]
    tail = hist_scr[HIST + tc - (width - 1):HIST + tc, :]
    hist_scr[base:HIST, :] = tail
    return out


def _tri(tc):
    r = lax.broadcasted_iota(I32, (tc, tc), 0)
    c = lax.broadcasted_iota(I32, (tc, tc), 1)
    return r >= c


def _eye(n):
    r = lax.broadcasted_iota(I32, (n, n), 0)
    c = lax.broadcasted_iota(I32, (n, n), 1)
    return (r == c).astype(F32)


def _cumsum_rows(x, causal):
    return jnp.dot(causal.astype(F32), x, precision=lax.Precision.HIGHEST, preferred_element_type=F32)


def _transpose_f32(x):
    return lax.dot_general(_eye(x.shape[1]), x, NT_DIMS, precision=lax.Precision.HIGHEST,
                           preferred_element_type=F32)


def _ssd_kernel(xbc_ref, z_ref, dt_ref, conv0_ref, h0_ref, cw_ref, cb_ref, dtb_ref, alog_ref, dskip_ref,
                ng_ref, y_ref, hs_ref, hist_scr, y_scr, *, tc, d_inner, n_heads):
    c = pl.program_id(1)

    @pl.when(c == 0)
    def _():
        hs_ref[...] = h0_ref[...]
        hist_scr[HIST - (SSD_CONV - 1):HIST, :] = conv0_ref[...]

    xs = _silu(_causal_conv_chunk(xbc_ref, hist_scr, cw_ref, cb_ref, tc, SSD_CONV))
    gn = SSD_GROUPS * SSD_D_STATE
    hg = n_heads // SSD_GROUPS
    causal = _tri(tc)
    x_dt = dt_ref[...] + dtb_ref[...]
    dt = jnp.maximum(x_dt, 0.0) + jnp.log1p(jnp.exp(-jnp.abs(x_dt)))
    a_head = -jnp.exp(alog_ref[...])
    cum = _cumsum_rows(dt * a_head, causal)
    cum_t = _transpose_f32(cum)
    cum_last = cum[tc - 1:tc, :]
    pair_w = 2 * SSD_HEAD_DIM
    assert pair_w == LANES and hg % 2 == 0
    first_lanes = lax.broadcasted_iota(I32, (tc, pair_w), 1) < SSD_HEAD_DIM
    first_rows = lax.broadcasted_iota(I32, (pair_w, SSD_D_STATE), 0) < SSD_HEAD_DIM
    pick = lambda a, b: jnp.where(first_lanes, a, b)
    for g in range(SSD_GROUPS):
        bm = xs[:, d_inner + g * SSD_D_STATE:d_inner + (g + 1) * SSD_D_STATE].astype(BF16)
        cm = xs[:, d_inner + gn + g * SSD_D_STATE:d_inner + gn + (g + 1) * SSD_D_STATE].astype(BF16)
        cb = lax.dot_general(cm, bm, NT_DIMS, preferred_element_type=F32)
        for pp in range(hg // 2):
            p = g * (hg // 2) + pp
            ha, hb = 2 * p, 2 * p + 1
            cols = slice(p * pair_w, (p + 1) * pair_w)
            col_a, col_b = cum[:, ha:ha + 1], cum[:, hb:hb + 1]
            xh = xs[:, cols]
            xdt = xh * pick(dt[:, ha:ha + 1], dt[:, hb:hb + 1])
            xdt_b = xdt.astype(BF16)
            ys = []
            for h, col in ((ha, col_a), (hb, col_b)):
                decay = jnp.exp(jnp.where(causal, col - cum_t[h:h + 1, :], -jnp.inf))
                ys.append(jnp.dot((cb * decay).astype(BF16), xdt_b, preferred_element_type=F32))
            h0 = hs_ref[p]
            inter = lax.dot_general(cm, h0.astype(BF16), NT_DIMS, preferred_element_type=F32)
            y = pick(ys[0], ys[1]) + pick(jnp.exp(col_a), jnp.exp(col_b)) * inter
            y_scr[:, cols] = y + dskip_ref[:, cols] * xh
            last_a, last_b = cum_last[:, ha:ha + 1], cum_last[:, hb:hb + 1]
            w = (pick(jnp.exp(last_a - col_a), jnp.exp(last_b - col_b)) * xdt).astype(BF16)
            keep = jnp.where(first_rows, jnp.exp(last_a), jnp.exp(last_b))
            hs_ref[p] = keep * h0 + lax.dot_general(w, bm, TN_DIMS, preferred_element_type=F32)
    yz = y_scr[...] * _silu(z_ref[...])
    gw = d_inner // SSD_GROUPS
    for g in range(SSD_GROUPS):
        cols = slice(g * gw, (g + 1) * gw)
        seg = yz[:, cols]
        ms = jnp.mean(seg * seg, axis=-1, keepdims=True)
        y_ref[:, cols] = (seg * lax.rsqrt(ms + NORM_EPS) * ng_ref[:, cols]).astype(y_ref.dtype)


def ssd_scan(xbc, z, dt, conv0, h0, conv_w, conv_b, dt_bias, a_log, d_skip_cols, norm_g, *, tc):
    bsz, length, cc = xbc.shape
    d_inner = z.shape[2]
    n_heads = dt.shape[2]
    assert length % tc == 0
    state_shape = h0.shape
    h0 = h0.reshape(bsz, n_heads // 2, 2 * state_shape[2], state_shape[3])
    seq = lambda w: pl.BlockSpec((None, tc, w), lambda b, c: (b, c, 0))
    per_b3 = lambda s: pl.BlockSpec((None,) + s, lambda b, c: (b,) + (0,) * len(s))
    row = lambda w: _resident((1, w), lambda b, c: (0, 0))
    y, h_new = pl.pallas_call(
        functools.partial(_ssd_kernel, tc=tc, d_inner=d_inner, n_heads=n_heads),
        grid=(bsz, length // tc),
        in_specs=[seq(cc), seq(d_inner), seq(n_heads), per_b3(conv0.shape[1:]), per_b3(h0.shape[1:]),
                  _resident(conv_w.shape, lambda b, c: (0, 0)), row(cc), row(n_heads), row(n_heads),
                  row(d_inner), row(d_inner)],
        out_specs=[seq(d_inner), per_b3(h0.shape[1:])],
        out_shape=[jax.ShapeDtypeStruct((bsz, length, d_inner), BF16), jax.ShapeDtypeStruct(h0.shape, F32)],
        scratch_shapes=[pltpu.VMEM((HIST + tc, cc), F32), pltpu.VMEM((tc, d_inner), F32)],
        compiler_params=_params("parallel", "arbitrary"),
        name="ssd_scan",
    )(xbc, z, dt, conv0, h0, conv_w, conv_b, dt_bias, a_log, d_skip_cols, norm_g)
    return y, h_new.reshape(state_shape)


def _mlstm_kernel(xc_ref, v_ref, o_ref, gates_ref, conv0_ref, c0_ref, n0_ref, m0_ref, cw_ref, cb_ref, wq_ref,
                  wk_ref, gb_ref, ng_ref, h_ref, cs_ref, ns_ref, ms_ref, hist_scr, *, tc, d_inner):
    c = pl.program_id(1)

    @pl.when(c == 0)
    def _():
        cs_ref[...] = c0_ref[...]
        ns_ref[...] = n0_ref[...]
        ms_ref[...] = m0_ref[...]
        hist_scr[HIST - (MLSTM_CONV - 1):HIST, :] = conv0_ref[...]

    xa = _silu(_causal_conv_chunk(xc_ref, hist_scr, cw_ref, cb_ref, tc, MLSTM_CONV)).astype(BF16)
    blk = wq_ref.shape[1]
    dh = d_inner // MLSTM_HEADS
    q = jnp.concatenate([jnp.dot(xa[:, j * blk:(j + 1) * blk], wq_ref[j], preferred_element_type=F32)
                         for j in range(d_inner // blk)], axis=1)
    k = jnp.concatenate([jnp.dot(xa[:, j * blk:(j + 1) * blk], wk_ref[j], preferred_element_type=F32)
                         for j in range(d_inner // blk)], axis=1) * dh ** -0.5
    causal = _tri(tc)
    gates = gates_ref[...] + gb_ref[...]
    log_f = jnp.minimum(gates, 0.0) - jnp.log1p(jnp.exp(-jnp.abs(gates)))
    f_cum = _cumsum_rows(log_f, causal)
    f_cum_t = _transpose_f32(f_cum)
    gates_t = _transpose_f32(gates)
    for hh in range(MLSTM_HEADS):
        cols = slice(hh * dh, (hh + 1) * dh)
        fh = MLSTM_HEADS + hh
        qh = q[:, cols].astype(BF16)
        kh = k[:, cols]
        vh = v_ref[:, cols]
        fc = f_cum[:, fh:fh + 1]
        i_col = gates[:, hh:hh + 1]
        m0 = ms_ref[0:1, hh:hh + 1]
        d_log = jnp.where(causal, fc - f_cum_t[fh:fh + 1, :] + gates_t[hh:hh + 1, :], -jnp.inf)
        inter = fc + m0
        m = jnp.maximum(jnp.max(d_log, axis=-1, keepdims=True), inter)
        s = lax.dot_general(qh, kh.astype(BF16), NT_DIMS, preferred_element_type=F32) * jnp.exp(d_log - m)
        w_inter = jnp.exp(inter - m)
        c0 = cs_ref[hh]
        n0 = ns_ref[hh:hh + 1, :]
        num = (jnp.dot(s.astype(BF16), vh, preferred_element_type=F32)
               + w_inter * jnp.dot(qh, c0.astype(BF16), preferred_element_type=F32))
        den = (jnp.sum(s, axis=-1, keepdims=True)
               + w_inter * jnp.sum(q[:, cols] * n0, axis=-1, keepdims=True))
        h = num / jnp.maximum(jnp.abs(den), jnp.exp(-m))
        m_end = m[tc - 1:tc, :]
        f_last = fc[tc - 1:tc, :]
        w_end = jnp.exp(f_last - fc + i_col - m_end)
        decay = jnp.exp(f_last + m0 - m_end)
        wk = w_end * kh
        cs_ref[hh] = decay * c0 + lax.dot_general(wk.astype(BF16), vh, TN_DIMS, preferred_element_type=F32)
        ns_ref[hh:hh + 1, :] = decay * n0 + jnp.sum(wk, axis=0, keepdims=True)
        ms_ref[0:1, hh:hh + 1] = m_end
        h = jax.nn.sigmoid(o_ref[:, cols]) * h
        h = h - jnp.mean(h, axis=-1, keepdims=True)
        h = h * lax.rsqrt(jnp.mean(h * h, axis=-1, keepdims=True) + NORM_EPS)
        h_ref[:, cols] = (h * ng_ref[:, cols]).astype(h_ref.dtype)


def mlstm_scan(xc, v, o, gates, conv0, c0, n0, m0, conv_w, conv_b, wq_bd, wk_bd, gate_b, norm_g, *, tc):
    bsz, length, d_inner = xc.shape
    assert length % tc == 0
    seq = lambda w: pl.BlockSpec((None, tc, w), lambda b, c: (b, c, 0))
    per_b = lambda s: pl.BlockSpec((None,) + s, lambda b, c: (b,) + (0,) * len(s))
    row = lambda w: _resident((1, w), lambda b, c: (0, 0))
    return pl.pallas_call(
        functools.partial(_mlstm_kernel, tc=tc, d_inner=d_inner),
        grid=(bsz, length // tc),
        in_specs=[seq(d_inner), seq(d_inner), seq(d_inner), seq(gates.shape[2]), per_b(conv0.shape[1:]),
                  per_b(c0.shape[1:]), per_b(n0.shape[1:]), per_b(m0.shape[1:]),
                  _resident(conv_w.shape, lambda b, c: (0, 0)), row(d_inner),
                  _resident(wq_bd.shape, lambda b, c: (0, 0, 0)), _resident(wk_bd.shape, lambda b, c: (0, 0, 0)),
                  row(gates.shape[2]), row(d_inner)],
        out_specs=[seq(d_inner), per_b(c0.shape[1:]), per_b(n0.shape[1:]), per_b(m0.shape[1:])],
        out_shape=[jax.ShapeDtypeStruct((bsz, length, d_inner), BF16), jax.ShapeDtypeStruct(c0.shape, F32),
                   jax.ShapeDtypeStruct(n0.shape, F32), jax.ShapeDtypeStruct(m0.shape, F32)],
        scratch_shapes=[pltpu.VMEM((HIST + tc, d_inner), F32)],
        compiler_params=_params("parallel", "arbitrary"),
        name="mlstm_scan",
    )(xc, v, o, gates, conv0, c0, n0, m0, conv_w, conv_b, wq_bd, wk_bd, gate_b, norm_g)


def _block_diag(w_blk, blk):
    n, c, d = w_blk.shape
    per = blk // c
    eye = jnp.eye(per, dtype=w_blk.dtype)
    tiles = w_blk.reshape(n // per, per, c, d)
    return jnp.einsum("jpcd,pq->jpcqd", tiles, eye).reshape(n // per, per * c, per * d)


def _mixer_a(xp, xs, shapes, cache_k, cache_v, cache_ki, w_in, t5_table):
    (bp, lp_), (bs, ls) = shapes
    hd, kvd, idd = A_HEADS * A_HEAD_DIM, A_KV_HEADS * A_HEAD_DIM, IDX_HEADS * IDX_DIM
    w_pad, cols = _pad_cols(w_in, (hd, kvd, kvd, idd, IDX_DIM, IDX_HEADS))
    outs = [(cols[0], BF16, A_HEAD_DIM ** -0.5), (cols[1], F32), (cols[1], BF16), (cols[2], F32), (cols[2], BF16),
            (cols[3], BF16), (cols[4], F32), (cols[4], BF16), (cols[5], F32)]
    w_pad = w_pad.astype(BF16)
    bias = _t5_bias_tiles(t5_table)
    on_lanes = lambda a: jnp.swapaxes(a, 1, 2)

    q, k, kb, v, vb, qi, ki, kib, wi = project(xp, w_pad, outs, name="project_a")
    r3 = lambda a, b_, l: a.reshape(b_, l, a.shape[-1])
    att_p = dsa_attention(r3(q, bp, lp_), on_lanes(r3(qi, bp, lp_)), on_lanes(r3(wi, bp, lp_)),
                          on_lanes(r3(kb, bp, lp_)), r3(vb, bp, lp_), r3(kib, bp, lp_), bias, tq=KEY_TILE, q_off=0,
                          l_true=lp_,
                          n_sel=min(IDX_TOPK_MAX, lp_ // 4))
    outs_p = (k.reshape(bp, lp_, A_KV_HEADS, A_HEAD_DIM), v.reshape(bp, lp_, A_KV_HEADS, A_HEAD_DIM),
              ki.reshape(bp, lp_, IDX_DIM))

    q, k, kb, v, vb, qi, ki, kib, wi = project(xs, w_pad, outs, name="project_a")
    past = cache_k.shape[1]
    total = past + ls
    lpad = -(-total // (2 * KEY_TILE)) * (2 * KEY_TILE)

    def with_past(cache, new):
        parts = [cache.reshape(bs, past, -1).astype(BF16), r3(new, bs, ls)]
        if lpad != total:
            parts.append(jnp.zeros((bs, lpad - total, new.shape[-1]), BF16))
        return jnp.concatenate(parts, axis=1)

    att_s = dsa_attention(r3(q, bs, ls), on_lanes(r3(qi, bs, ls)), on_lanes(r3(wi, bs, ls)),
                          on_lanes(with_past(cache_k, kb)), with_past(cache_v, vb), with_past(cache_ki, kib), bias,
                          tq=ls, q_off=past,
                          l_true=total, n_sel=min(IDX_TOPK_MAX, total // 4))
    outs_s = (k.reshape(bs, ls, A_KV_HEADS, A_HEAD_DIM), v.reshape(bs, ls, A_KV_HEADS, A_HEAD_DIM),
              ki.reshape(bs, ls, IDX_DIM))
    return att_p.reshape(bp * lp_, hd), att_s.reshape(bs * ls, hd), outs_p, outs_s


def _mixer_b(xp, xs, shapes, cache_k, cache_v, w_in, rel_table):
    (bp, lp_), (bs, ls) = shapes
    hd = BAND_HEADS * BAND_HEAD_DIM
    w_pad, cols = _pad_cols(w_in, (hd, hd, hd))
    outs = [(cols[0], BF16), (cols[1], F32), (cols[1], BF16), (cols[2], F32), (cols[2], BF16)]
    w_pad = w_pad.astype(BF16)

    q, k, kb, v, vb = project(xp, w_pad, outs, name="project_b")
    r3 = lambda a, b_, l: a.reshape(b_, l, hd)
    att_p = band_attention_prompt(r3(q, bp, lp_), r3(kb, bp, lp_), r3(vb, bp, lp_),
                                  _band_bias(rel_table, CHUNK, BAND_ROWS), grp=BAND_ROWS)
    keep = min(BAND_WINDOW, lp_)
    heads = lambda a, b_, l: a.reshape(b_, l, BAND_HEADS, BAND_HEAD_DIM)
    outs_p = (heads(k, bp, lp_)[:, lp_ - keep:], heads(v, bp, lp_)[:, lp_ - keep:])

    q, k, kb, v, vb = project(xs, w_pad, outs, name="project_b")
    past = cache_k.shape[1]
    att_s = band_attention_sample(r3(q, bs, ls), cache_k.reshape(bs, past, hd).astype(BF16), r3(kb, bs, ls),
                                  cache_v.reshape(bs, past, hd).astype(BF16), r3(vb, bs, ls),
                                  _band_bias(rel_table, ls, ls))
    outs_s = (jnp.concatenate([cache_k, heads(k, bs, ls)], axis=1)[:, ls:],
              jnp.concatenate([cache_v, heads(v, bs, ls)], axis=1)[:, ls:])
    return att_p.reshape(bp * lp_, hd), att_s.reshape(bs * ls, hd), outs_p, outs_s


def _mixer_c(xp, xs, shapes, ssm0, conv0, w_in, conv_w, conv_b, dt_bias, a_log, d_skip, norm_g):
    n_heads = a_log.shape[0]
    d_inner = n_heads * SSD_HEAD_DIM
    conv_dim = conv_w.shape[1]
    w_pad, cols = _pad_cols(w_in, (d_inner, conv_dim, n_heads))
    outs = [(cols[0], F32), (cols[1], F32), (cols[2], F32)]
    w_pad = w_pad.astype(BF16)
    row = lambda a: a.reshape(1, -1)
    d_skip_cols = jnp.repeat(d_skip, SSD_HEAD_DIM).reshape(1, d_inner)
    res = []
    for x, (b_, l), h0, c0 in ((xp, shapes[0], None, None), (xs, shapes[1], ssm0, conv0)):
        if h0 is None:
            h0 = jnp.zeros((b_, n_heads, SSD_HEAD_DIM, SSD_D_STATE), F32)
            c0 = jnp.zeros((b_, SSD_CONV - 1, conv_dim), F32)
        z, xbc, dt = project(x, w_pad, outs, name="project_c")
        xbc3 = xbc.reshape(b_, l, conv_dim)
        y, h_new = ssd_scan(xbc3, z.reshape(b_, l, d_inner), dt.reshape(b_, l, n_heads), c0, h0, conv_w,
                            row(conv_b), row(dt_bias), row(a_log), d_skip_cols, row(norm_g), tc=min(SSD_ROWS, l))
        conv_new = jnp.concatenate([c0, xbc3], axis=1)[:, l:]
        res.append((y.reshape(b_ * l, d_inner), (h_new, conv_new)))
    return res[0][0], res[1][0], res[0][1], res[1][1]


def _mixer_d(xp, xs, shapes, c0s, n0s, m0s, conv0s, w_in, conv_w, conv_b, wq_blk, wk_blk, gate_b, norm_g):
    d_inner = conv_w.shape[1]
    dh = d_inner // MLSTM_HEADS
    w_pad, cols = _pad_cols(w_in, (d_inner, d_inner, d_inner, 2 * MLSTM_HEADS))
    gcol = (cols[3][0], cols[3][1], cols[3][1])
    outs = [(cols[0], F32), (cols[1], BF16), (cols[2], F32), (gcol, F32)]
    w_pad = w_pad.astype(BF16)
    blk = 2 * LANES
    wq_bd = _block_diag(wq_blk, blk).astype(BF16)
    wk_bd = _block_diag(wk_blk, blk).astype(BF16)
    gate_b_pad = jnp.zeros((1, gcol[1]), F32).at[0, :2 * MLSTM_HEADS].set(gate_b)
    row = lambda a: a.reshape(1, -1)
    res = []
    for x, (b_, l), st in ((xp, shapes[0], None), (xs, shapes[1], (c0s, n0s, m0s, conv0s))):
        if st is None:
            st = (jnp.zeros((b_, MLSTM_HEADS, dh, dh), F32), jnp.zeros((b_, MLSTM_HEADS, dh), F32),
                  jnp.zeros((b_, MLSTM_HEADS), F32), jnp.zeros((b_, MLSTM_CONV - 1, d_inner), F32))
        c0, n0, m0, conv0 = st
        xc, v, o, gates = project(x, w_pad, outs, name="project_d")
        r3 = lambda a: a.reshape(b_, l, a.shape[-1])
        h, c_new, n_new, m_new = mlstm_scan(r3(xc), r3(v), r3(o), r3(gates), conv0, c0, n0,
                                            m0.reshape(b_, 1, MLSTM_HEADS), conv_w, row(conv_b), wq_bd, wk_bd,
                                            gate_b_pad, row(norm_g), tc=min(MLSTM_ROWS, l))
        conv_new = jnp.concatenate([conv0, r3(xc)], axis=1)[:, l:]
        res.append((h.reshape(b_ * l, d_inner), (c_new, n_new, m_new.reshape(b_, MLSTM_HEADS), conv_new)))
    return res[0][0], res[1][0], res[0][1], res[1][1]


def kernel(x_prompt, x_sample, cache_a_k, cache_a_v, cache_a_kidx, cache_b_k, cache_b_v, state_c_ssm, state_c_conv, state_d_c, state_d_n, state_d_m, state_d_conv, a_w_in, a_w_out, t5_table, b_w_in, b_w_out, b_rel_table, c_w_in, c_conv_w, c_conv_b, c_dt_bias, c_a_log, c_d_skip, c_norm_g, c_w_out, d_w_in, d_conv_w, d_conv_b, d_wq_blk, d_wk_blk, d_gate_b, d_norm_g, d_w_out, ffn1_wg, ffn1_wu, ffn1_wd, ffn2_wg, ffn2_wu, ffn2_wd, ln_g, ln_b):
    bp, lp_, d = x_prompt.shape
    bs, ls, _ = x_sample.shape
    depth = ffn1_wg.shape[0]
    alpha = (2.0 * depth) ** 0.25
    shapes = ((bp, lp_), (bs, ls))
    xp = x_prompt.reshape(bp * lp_, d)
    xs = x_sample.reshape(bs * ls, d)
    ffn_w = [[w.astype(BF16) for w in ws] for ws in ((ffn1_wg, ffn1_wu, ffn1_wd), (ffn2_wg, ffn2_wu, ffn2_wd))]
    w_out = [w.astype(BF16) for w in (a_w_out, b_w_out, c_w_out, d_w_out)]
    row = lambda a: a.reshape(1, -1)
    extra = {}
    for i in range(depth):
        g, b = ln_g[i], ln_b[i]
        xp = ffn_postnorm(xp, *ffn_w[0], i, row(g[0]), row(b[0]), alpha)
        xs = ffn_postnorm(xs, *ffn_w[0], i, row(g[0]), row(b[0]), alpha)
        kind = i % 4
        if kind == 0:
            mp, ms, op, os_ = _mixer_a(xp, xs, shapes, cache_a_k, cache_a_v, cache_a_kidx, a_w_in, t5_table)
        elif kind == 1:
            mp, ms, op, os_ = _mixer_b(xp, xs, shapes, cache_b_k, cache_b_v, b_w_in, b_rel_table)
        elif kind == 2:
            mp, ms, op, os_ = _mixer_c(xp, xs, shapes, state_c_ssm, state_c_conv, c_w_in, c_conv_w, c_conv_b,
                                       c_dt_bias, c_a_log, c_d_skip, c_norm_g)
        else:
            mp, ms, op, os_ = _mixer_d(xp, xs, shapes, state_d_c, state_d_n, state_d_m, state_d_conv, d_w_in,
                                       d_conv_w, d_conv_b, d_wq_blk, d_wk_blk, d_gate_b, d_norm_g)
        extra[kind] = (op, os_)
        xp = mixout_ffn(xp, mp, w_out[kind], row(g[1]), row(b[1]), *ffn_w[1], i, row(g[2]), row(b[2]), alpha)
        xs = mixout_ffn(xs, ms, w_out[kind], row(g[1]), row(b[1]), *ffn_w[1], i, row(g[2]), row(b[2]), alpha)
    prompt_side = tuple(t for kind in range(4) for t in extra[kind][0])
    sample_side = tuple(t for kind in range(4) for t in extra[kind][1])
    return (xp.reshape(bp, lp_, d), xs.reshape(bs, ls, d)) + prompt_side + sample_side
```

```python
import functools
import math

import numpy as np
import jax
import jax.numpy as jnp
from jax import lax
from jax.experimental import pallas as pl
from jax.experimental.pallas import tpu as pltpu

F32 = jnp.float32
BF16 = jnp.bfloat16
I32 = jnp.int32

CHUNK = 64
NORM_EPS = 1e-5
A_HEADS, A_KV_HEADS, A_HEAD_DIM = 8, 2, 128
IDX_HEADS, IDX_DIM, IDX_TOPK_MAX = 8, 64, 256
T5_BUCKETS, T5_MAX_DIST = 32, 128
BAND_HEADS, BAND_HEAD_DIM, BAND_LEFT_CHUNKS, BAND_MAX_REL = 16, 64, 8, 128
BAND_WINDOW = BAND_LEFT_CHUNKS * CHUNK
SSD_HEAD_DIM, SSD_GROUPS, SSD_D_STATE, SSD_CONV = 64, 8, 128, 4
MLSTM_HEADS, MLSTM_CONV, MLSTM_QK_BLOCK = 4, 4, 4

LANES = 128
NEG_BIG = -1e30
VMEM_LIMIT = 56 * 1024 * 1024

NT_DIMS = (((1,), (1,)), ((), ()))
TN_DIMS = (((0,), (0,)), ((), ()))


def _params(*sem):
    return pltpu.CompilerParams(dimension_semantics=sem, vmem_limit_bytes=VMEM_LIMIT)


def _resident(shape, index_map):
    return pl.BlockSpec(shape, index_map, pipeline_mode=pl.Buffered(1))


def _layer_norm(y, g, b):
    mu = jnp.mean(y, axis=-1, keepdims=True)
    yc = y - mu
    var = jnp.mean(yc * yc, axis=-1, keepdims=True)
    return yc * lax.rsqrt(var + NORM_EPS) * g + b


def _silu(x):
    return x * jax.nn.sigmoid(x)


def _row_tile(n, want):
    t = min(n, want)
    assert n % t == 0, (n, t)
    return t


def _ffn_block(x, wg_ref, wu_ref, wd_ref, g_ref, b_ref, alpha, f_cuts):
    xb = x.astype(BF16)
    acc = jnp.zeros(x.shape, F32)
    for lo, hi in zip(f_cuts[:-1], f_cuts[1:]):
        sl = slice(lo, hi)
        gate = jnp.dot(xb, wg_ref[:, sl], preferred_element_type=F32)
        up = jnp.dot(xb, wu_ref[:, sl], preferred_element_type=F32)
        h = (_silu(gate) * up).astype(BF16)
        acc = acc + jnp.dot(h, wd_ref[sl, :], preferred_element_type=F32)
    return _layer_norm(alpha * x + 0.5 * acc, g_ref[...], b_ref[...])


def _ffn_kernel(x_ref, wg_ref, wu_ref, wd_ref, g_ref, b_ref, o_ref, *, alpha, f_cuts):
    o_ref[...] = _ffn_block(x_ref[...], wg_ref, wu_ref, wd_ref, g_ref, b_ref, alpha, f_cuts)


def _mixout_ffn_kernel(x_ref, m_ref, wo_ref, g1_ref, b1_ref, wg_ref, wu_ref, wd_ref, g2_ref, b2_ref, o_ref, *,
                       alpha, f_cuts):
    sub = jnp.dot(m_ref[...], wo_ref[...], preferred_element_type=F32)
    x1 = _layer_norm(alpha * x_ref[...] + sub, g1_ref[...], b1_ref[...])
    o_ref[...] = _ffn_block(x1, wg_ref, wu_ref, wd_ref, g2_ref, b2_ref, alpha, f_cuts)


MXU_WIDTH = 256


def _ffn_cuts(d_ff, n_chunks):
    tiles = -(-d_ff // MXU_WIDTH)
    cuts = [min(d_ff, MXU_WIDTH * (-(-tiles * c // n_chunks))) for c in range(n_chunks + 1)]
    return tuple(cuts)


def ffn_postnorm(x, wg, wu, wd, layer, g, b, alpha, tm=512, n_chunks=2):
    n, d = x.shape
    d_ff = wg.shape[2]
    tm = _row_tile(n, tm)
    return pl.pallas_call(
        functools.partial(_ffn_kernel, alpha=alpha, f_cuts=_ffn_cuts(d_ff, n_chunks)),
        grid=(n // tm,),
        in_specs=[
            pl.BlockSpec((tm, d), lambda i: (i, 0)),
            _resident((None, d, d_ff), lambda i: (layer, 0, 0)),
            _resident((None, d, d_ff), lambda i: (layer, 0, 0)),
            _resident((None, d_ff, d), lambda i: (layer, 0, 0)),
            _resident((1, d), lambda i: (0, 0)),
            _resident((1, d), lambda i: (0, 0)),
        ],
        out_specs=pl.BlockSpec((tm, d), lambda i: (i, 0)),
        out_shape=jax.ShapeDtypeStruct((n, d), F32),
        compiler_params=_params("parallel"),
        name="ffn_postnorm",
    )(x, wg, wu, wd, g, b)


def mixout_ffn(x, m, w_out, g1, b1, wg, wu, wd, layer, g2, b2, alpha, tm=512, n_chunks=2):
    n, d = x.shape
    k = m.shape[1]
    d_ff = wg.shape[2]
    tm = _row_tile(n, tm)
    vec = lambda: _resident((1, d), lambda i: (0, 0))
    return pl.pallas_call(
        functools.partial(_mixout_ffn_kernel, alpha=alpha, f_cuts=_ffn_cuts(d_ff, n_chunks)),
        grid=(n // tm,),
        in_specs=[
            pl.BlockSpec((tm, d), lambda i: (i, 0)),
            pl.BlockSpec((tm, k), lambda i: (i, 0)),
            _resident((k, d), lambda i: (0, 0)), vec(), vec(),
            _resident((None, d, d_ff), lambda i: (layer, 0, 0)),
            _resident((None, d, d_ff), lambda i: (layer, 0, 0)),
            _resident((None, d_ff, d), lambda i: (layer, 0, 0)), vec(), vec(),
        ],
        out_specs=pl.BlockSpec((tm, d), lambda i: (i, 0)),
        out_shape=jax.ShapeDtypeStruct((n, d), F32),
        compiler_params=_params("parallel"),
        name="mixout_ffn",
    )(x, m, w_out, g1, b1, wg, wu, wd, g2, b2)


def _proj_kernel(x_ref, w_ref, *o_refs, cols, scales):
    xb = x_ref[...].astype(BF16)
    done = {}
    for (off, pad_w, true_w), scale, o_ref in zip(cols, scales, o_refs):
        if (off, pad_w) not in done:
            done[(off, pad_w)] = jnp.dot(xb, w_ref[:, off:off + pad_w], preferred_element_type=F32)
        y = done[(off, pad_w)][:, :true_w]
        o_ref[...] = (y if scale is None else y * scale).astype(o_ref.dtype)


def _pad_cols(w, widths):
    pieces, offs, off, src = [], [], 0, 0
    for wd in widths:
        pad_w = -(-wd // LANES) * LANES
        pieces.append(w[:, src:src + wd])
        if pad_w != wd:
            pieces.append(jnp.zeros((w.shape[0], pad_w - wd), w.dtype))
        offs.append((off, pad_w, wd))
        off += pad_w
        src += wd
    return jnp.concatenate(pieces, axis=1), offs


def project(x, w_pad, outs, tm=512, name="project"):
    n, d = x.shape
    tm = _row_tile(n, tm)
    cols = tuple(o[0] for o in outs)
    scales = tuple(o[2] if len(o) > 2 else None for o in outs)
    outs = [o[:2] for o in outs]
    return pl.pallas_call(
        functools.partial(_proj_kernel, cols=cols, scales=scales),
        grid=(n // tm,),
        in_specs=[pl.BlockSpec((tm, d), lambda i: (i, 0)),
                  _resident(w_pad.shape, lambda i: (0, 0))],
        out_specs=[pl.BlockSpec((tm, c[2]), lambda i: (i, 0)) for c in cols],
        out_shape=[jax.ShapeDtypeStruct((n, c[2]), dt) for c, dt in outs],
        compiler_params=_params("parallel"),
        name=name,
    )(x, w_pad)


def _t5_bucket_np(rel):
    half = T5_BUCKETS // 2
    max_exact = half // 2
    n = np.abs(rel)
    nf = np.maximum(n, 1).astype(np.float32)
    large = max_exact + (np.log(nf / np.float32(max_exact)) / np.float32(math.log(T5_MAX_DIST / max_exact))
                         * np.float32(half - max_exact)).astype(np.int32)
    large = np.minimum(large, half - 1)
    return np.where(rel > 0, half, 0) + np.where(n < max_exact, n, large)


KEY_TILE = 2 * LANES


def _toeplitz(w, t_rows, s_cols):
    n = t_rows + s_cols
    lead = w.shape[:-1]
    wp = jnp.concatenate([w, jnp.zeros(lead + (1,), w.dtype)], axis=-1)
    flat = jnp.broadcast_to(wp[..., None, :], lead + (t_rows, n)).reshape(lead + (t_rows * n,))
    skew = flat[..., :t_rows * (n - 1)].reshape(lead + (t_rows, n - 1))
    return skew[..., t_rows - 1:t_rows - 1 + s_cols]


def _t5_bias_tiles(t5_table):
    assert T5_MAX_DIST <= KEY_TILE
    rel = np.arange(3 * KEY_TILE - 1) - (KEY_TILE - 1) - KEY_TILE
    far = t5_table[int(_t5_bucket_np(np.array(-T5_MAX_DIST)))]
    by_rel = (t5_table[_t5_bucket_np(rel)] - far).T
    strip = _toeplitz(by_rel, KEY_TILE, 2 * KEY_TILE)
    return jnp.stack([strip[:, :, KEY_TILE:], strip[:, :, :KEY_TILE]])


def _dsa_kernel(q_ref, qit_ref, wit_ref, kt_ref, v_ref, ki_ref, bias_ref, o_ref,
                keyt_scr, mask_scr, lg_scr, m_scr, l_scr, acc_scr, *, tq, q_off, l_true, n_sel, lp):
    j = pl.program_id(1)
    q0 = q_off + j * tq
    kend = jnp.minimum(((q0 + tq - 1) // CHUNK + 1) * CHUNK, l_true)
    n_tiles = (kend + KEY_TILE - 1) // KEY_TILE
    n_far = jnp.maximum(q0 // KEY_TILE - 1, 0)
    int_min = jnp.int32(-2 ** 31)
    wide = 2 * KEY_TILE

    def tile_off(kt):
        return pl.multiple_of(kt * KEY_TILE, KEY_TILE)

    def slabs(lo, hi, fn):
        def pair(i, carry):
            fn(tile_off(lo + 2 * i), wide)
            return carry
        lax.fori_loop(0, (hi - lo) // 2, pair, 0)

        @pl.when((hi - lo) % 2 == 1)
        def _():
            fn(tile_off(hi - 1), KEY_TILE)

    wi = wit_ref[...] * (IDX_HEADS ** -0.5) * (IDX_DIM ** -0.5)

    def positions(off, n_keys):
        kpos = off + lax.broadcasted_iota(I32, (n_keys, tq), 0)
        q_chunk = (q0 + lax.broadcasted_iota(I32, (n_keys, tq), 1)) // CHUNK
        return kpos, ((kpos // CHUNK) <= q_chunk) & (kpos < l_true)

    def to_key(x):
        bits = lax.bitcast_convert_type(x, I32)
        return jnp.where(bits < 0, (bits ^ jnp.int32(0x7FFFFFFF)) + 1, bits)

    last_off = tile_off(n_tiles - 1)

    def score_slab(off, width, check_admissible=False):
        ki_t = ki_ref[pl.ds(off, width), :]
        accs = [jnp.zeros((width, tq), F32)] * 2
        for h in range(IDX_HEADS):
            s = jnp.dot(ki_t, qit_ref[h * IDX_DIM:(h + 1) * IDX_DIM, :], preferred_element_type=F32)
            accs[h % 2] = accs[h % 2] + jnp.maximum(s, 0.0) * wi[h:h + 1, :]
        sc = accs[0] + accs[1]
        if check_admissible:
            sc = jnp.where(positions(off, width)[1], sc, -jnp.inf)
        keyt_scr[pl.ds(off, width), :] = to_key(sc)

    slabs(0, n_tiles - 1, score_slab)
    score_slab(last_off, KEY_TILE, check_admissible=True)

    @pl.when(n_tiles % 2 == 1)
    def _():
        keyt_scr[pl.ds(tile_off(n_tiles), KEY_TILE), :] = to_key(jnp.full((KEY_TILE, tq), -jnp.inf, F32))

    sublanes = 8

    def count(pred):
        def body(ct, acc):
            off = pl.multiple_of(ct * wide, wide)
            kpos = off + lax.broadcasted_iota(I32, (wide, tq), 0)
            hit = pred(keyt_scr[pl.ds(off, wide), :], kpos).astype(I32)
            return acc + jnp.sum(hit.reshape(wide // sublanes, sublanes, tq), axis=0)
        acc = lax.fori_loop(0, (n_tiles + 1) // 2, body, jnp.zeros((sublanes, tq), I32))
        return jnp.sum(acc.astype(F32), axis=0, keepdims=True)

    check_every = 4

    def search_body(state):
        i, thr_u, settled, _ = state

        def one_bit(b, st):
            thr_u, settled = st
            cand_u = thr_u | lax.shift_left(jnp.int32(1), 31 - (i + b))
            cand = cand_u ^ int_min
            cnt = count(lambda key, kpos: key >= cand)
            thr_u = jnp.where((cnt >= n_sel) & (settled == 0), cand_u, thr_u)
            return thr_u, jnp.where(cnt == n_sel, 1, settled)

        thr_u, settled = lax.fori_loop(0, check_every, one_bit, (thr_u, settled))
        return i + check_every, thr_u, settled, jnp.min(settled.astype(F32))

    zeros = jnp.zeros((1, tq), I32)
    _, thr_u, _, _ = lax.while_loop(lambda s: (s[0] < 32) & (s[3] == 0.0), search_body,
                                    (jnp.int32(0), zeros, zeros, jnp.float32(0.0)))
    thr = thr_u ^ int_min
    need = n_sel - count(lambda key, kpos: key > thr)
    n_eq = count(lambda key, kpos: key == thr)
    has_tie = jnp.max(n_eq - need) > 0.0

    nbits = int(lp).bit_length()

    def tie_cut():
        def cut_body(i, cut):
            cand = cut | lax.shift_left(jnp.int32(1), nbits - 1 - i)
            cnt = count(lambda key, kpos: (key == thr) & (kpos < cand))
            return jnp.where(cnt <= need, cand, cut)
        return lax.fori_loop(0, nbits, cut_body, jnp.zeros((1, tq), I32))

    cut = lax.cond(has_tie, tie_cut, lambda: jnp.full((1, tq), 2 ** 30, I32))

    eye = (lax.broadcasted_iota(I32, (tq, tq), 0) == lax.broadcasted_iota(I32, (tq, tq), 1)).astype(BF16)

    def mask_tile(off, tie, check_admissible=False):
        key = keyt_scr[pl.ds(off, KEY_TILE), :]
        kpos, adm = positions(off, KEY_TILE)
        sel = ((key > thr) | ((key == thr) & (kpos < cut))) if tie else (key >= thr)
        if check_admissible:
            sel = sel & adm
        sel_q = lax.dot_general(eye, jnp.where(sel, 1.0, 0.0).astype(BF16), NT_DIMS, preferred_element_type=F32)
        mask_scr[:, pl.ds(off, KEY_TILE)] = jnp.where(sel_q > 0.5, 0.0, NEG_BIG)

    def mask_all(tie):
        def slab(off, width):
            for u in range(width // KEY_TILE):
                mask_tile(pl.multiple_of(off + u * KEY_TILE, KEY_TILE), tie)

        def run():
            slabs(0, n_tiles - 1, slab)
            mask_tile(last_off, tie, check_admissible=True)
        return run

    lax.cond(has_tie, mask_all(True), mask_all(False))

    grp = A_HEADS // A_KV_HEADS
    rows = grp * tq
    head_cols = lambda h: slice(h * A_HEAD_DIM, (h + 1) * A_HEAD_DIM)
    for g in range(A_KV_HEADS):
        qg = jnp.concatenate([q_ref[:, head_cols(g * grp + i)] for i in range(grp)], axis=0)
        m_scr[...] = jnp.full(m_scr.shape, NEG_BIG, F32)

        def qk_slab(off, width, near=False, qg=qg, g=g):
            lg = jnp.dot(qg, kt_ref[head_cols(g), pl.ds(off, width)], preferred_element_type=F32)
            lg = lg.reshape(grp, tq, width) + mask_scr[:, pl.ds(off, width)][None]
            if near:
                back = jnp.clip((q0 - off) // KEY_TILE, 0, 1)
                lg = lg + bias_ref[back, g * grp:(g + 1) * grp, 0:tq, :]
            lg = lg.reshape(rows, width)
            lg_scr[:, pl.ds(off, width)] = lg
            mx = lg[:, 0:LANES]
            for u in range(1, width // LANES):
                mx = jnp.maximum(mx, lg[:, u * LANES:(u + 1) * LANES])
            m_scr[...] = jnp.maximum(m_scr[...], mx)

        def near_body(kt, carry, qk_slab=qk_slab):
            qk_slab(tile_off(kt), KEY_TILE, near=True)
            return carry

        slabs(0, n_far, qk_slab)
        lax.fori_loop(n_far, n_tiles, near_body, 0)
        m_scr[...] = jnp.broadcast_to(jnp.max(m_scr[...], axis=-1, keepdims=True), m_scr.shape)
        l_scr[...] = jnp.zeros(l_scr.shape, F32)
        acc_scr[...] = jnp.zeros(acc_scr.shape, F32)

        def pv_slab(off, width, g=g):
            lg = lg_scr[:, pl.ds(off, width)]
            m_b = m_scr[...]
            ps = [jnp.exp(lg[:, u * LANES:(u + 1) * LANES] - m_b) for u in range(width // LANES)]
            l_scr[...] += functools.reduce(lambda a, b: a + b, ps)
            p = jnp.concatenate(ps, axis=1).astype(BF16)
            acc_scr[...] += jnp.dot(p, v_ref[pl.ds(off, width), head_cols(g)], preferred_element_type=F32)

        slabs(0, n_tiles, pv_slab)
        out = acc_scr[...] / jnp.sum(l_scr[...], axis=-1, keepdims=True)
        for i in range(grp):
            o_ref[:, head_cols(g * grp + i)] = out[i * tq:(i + 1) * tq].astype(o_ref.dtype)


def dsa_attention(q, qi_t, wi_t, k_t, v, ki, bias, *, tq, q_off, l_true, n_sel):
    bsz, t_len, _ = q.shape
    lp = v.shape[1]
    assert lp % (2 * KEY_TILE) == 0 and t_len % tq == 0 and q_off % KEY_TILE == 0
    assert tq == KEY_TILE or (t_len == tq and tq < KEY_TILE)
    grp = A_HEADS // A_KV_HEADS
    qspec = lambda w: pl.BlockSpec((None, tq, w), lambda b, j: (b, j, 0))
    qtspec = lambda h: pl.BlockSpec((None, h, tq), lambda b, j: (b, 0, j))
    whole = lambda a: pl.BlockSpec((None,) + a.shape[1:], lambda b, j: (b, 0, 0))
    return pl.pallas_call(
        functools.partial(_dsa_kernel, tq=tq, q_off=q_off, l_true=l_true, n_sel=n_sel, lp=lp),
        grid=(bsz, t_len // tq),
        in_specs=[qspec(q.shape[2]), qtspec(qi_t.shape[1]), qtspec(wi_t.shape[1]),
                  whole(k_t), whole(v), whole(ki),
                  _resident(bias.shape, lambda b, j: (0, 0, 0, 0))],
        out_specs=qspec(q.shape[2]),
        out_shape=jax.ShapeDtypeStruct(q.shape, BF16),
        scratch_shapes=[pltpu.VMEM((lp, tq), I32), pltpu.VMEM((tq, lp), F32), pltpu.VMEM((grp * tq, lp), F32),
                        pltpu.VMEM((grp * tq, LANES), F32), pltpu.VMEM((grp * tq, LANES), F32),
                        pltpu.VMEM((grp * tq, A_HEAD_DIM), F32)],
        compiler_params=_params("parallel", "arbitrary"),
        name="dsa_attention",
    )(q, qi_t, wi_t, k_t, v, ki, bias)


BAND_ROWS = 4 * CHUNK


def _band_bias(rel_table, chunk, grp):
    win = BAND_WINDOW + grp
    t = np.arange(grp)[:, None]
    s = np.arange(win)[None, :]
    k = np.arange(grp + win - 1)
    idx = np.clip(BAND_WINDOW + (grp - 1) - k, -BAND_MAX_REL, BAND_MAX_REL) + BAND_MAX_REL
    bias = _toeplitz(rel_table[:, idx], grp, win)
    lo = (t // chunk) * chunk
    allowed = (s >= lo) & (s < lo + BAND_WINDOW + chunk)
    return jnp.where(allowed[None], bias, NEG_BIG)


def _band_kernel(q_ref, kp_ref, kc_ref, vp_ref, vc_ref, bias_ref, o_ref, kcat, vcat, *, grp, first_has_no_past):
    i = pl.program_id(1)
    prev = kp_ref.shape[0]
    tq = q_ref.shape[0]
    win = prev + grp
    kcat[0:prev, :] = kp_ref[...]
    kcat[prev:prev + tq, :] = kc_ref[...]
    vcat[0:prev, :] = vp_ref[...]
    vcat[prev:prev + tq, :] = vc_ref[...]
    lane = lax.broadcasted_iota(I32, (grp, win), 1)
    pair_w = 2 * BAND_HEAD_DIM
    assert pair_w == LANES
    first_half = lax.broadcasted_iota(I32, (grp, pair_w), 1) < BAND_HEAD_DIM

    def attend(mask_missing_past):
        for r0 in range(0, tq, grp):
            if mask_missing_past:
                kpos_ok = (r0 - prev + lane) >= 0
            for hp in range(BAND_HEADS // 2):
                cols = slice(hp * pair_w, (hp + 1) * pair_w)
                q_pair = q_ref[r0:r0 + grp, cols]
                k_pair = kcat[r0:r0 + win, cols]
                v_pair = vcat[r0:r0 + win, cols]
                outs = []
                for side in range(2):
                    q_one = jnp.where(first_half == (side == 0), q_pair, jnp.zeros_like(q_pair))
                    lg = lax.dot_general(q_one, k_pair, NT_DIMS, preferred_element_type=F32)
                    lg = lg + bias_ref[2 * hp + side]
                    if mask_missing_past:
                        lg = jnp.where(kpos_ok, lg, NEG_BIG)
                    p = jnp.exp(lg - jnp.max(lg, axis=-1, keepdims=True))
                    den = jnp.sum(p, axis=-1, keepdims=True)
                    outs.append(jnp.dot(p.astype(BF16), v_pair, preferred_element_type=F32) / den)
                o_ref[r0:r0 + grp, cols] = jnp.where(first_half, outs[0], outs[1]).astype(o_ref.dtype)

    if first_has_no_past:
        lax.cond(i == 0, lambda: attend(True), lambda: attend(False))
    else:
        attend(False)


def band_attention_prompt(q, k, v, bias, *, tq=512, grp=128):
    bsz, length, w = q.shape
    assert tq == BAND_WINDOW and length % tq == 0
    cur = pl.BlockSpec((None, tq, w), lambda b, i: (b, i, 0))
    prv = pl.BlockSpec((None, tq, w), lambda b, i: (b, jnp.maximum(i - 1, 0), 0))
    return pl.pallas_call(
        functools.partial(_band_kernel, grp=grp, first_has_no_past=True),
        grid=(bsz, length // tq),
        in_specs=[cur, prv, cur, prv, cur, _resident(bias.shape, lambda b, i: (0, 0, 0))],
        out_specs=cur,
        out_shape=jax.ShapeDtypeStruct(q.shape, BF16),
        scratch_shapes=[pltpu.VMEM((2 * tq, w), BF16), pltpu.VMEM((2 * tq, w), BF16)],
        compiler_params=_params("parallel", "arbitrary"),
        name="band_attention",
    )(q, k, k, v, v, bias)


def band_attention_sample(q, k_past, k_new, v_past, v_new, bias):
    bsz, t_len, w = q.shape
    past = k_past.shape[1]
    assert past == BAND_WINDOW
    new = pl.BlockSpec((None, t_len, w), lambda b, i: (b, 0, 0))
    old = pl.BlockSpec((None, past, w), lambda b, i: (b, 0, 0))
    return pl.pallas_call(
        functools.partial(_band_kernel, grp=t_len, first_has_no_past=False),
        grid=(bsz, 1),
        in_specs=[new, old, new, old, new, _resident(bias.shape, lambda b, i: (0, 0, 0))],
        out_specs=new,
        out_shape=jax.ShapeDtypeStruct(q.shape, BF16),
        scratch_shapes=[pltpu.VMEM((past + t_len, w), BF16), pltpu.VMEM((past + t_len, w), BF16)],
        compiler_params=_params("parallel", "arbitrary"),
        name="band_attention_sample",
    )(q, k_past, k_new, v_past, v_new, bias)


CARRY_ROWS = 8
SSD_ROWS = 128
MLSTM_ROWS = 256


def _conv_carry_init(conv0_ref, w_ref, carry_scr, width):
    for k in range(1, width):
        z = None
        for i in range(k, width):
            term = w_ref[width - 1 - i:width - i, :] * conv0_ref[width - 2 - (i - k):width - 1 - (i - k), :]
            z = term if z is None else z + term
        carry_scr[k - 1:k, :] = z


def _causal_conv_chunk(x_ref, carry_scr, w_ref, b_ref, tc, width):
    x = x_ref[...]
    first_row = lax.broadcasted_iota(I32, x.shape, 0) == 0
    z = None
    for k in range(width - 1, 0, -1):
        y = x * w_ref[width - 1 - k:width - k, :]
        z = y if z is None else y + delayed
        delayed = jnp.where(first_row, carry_scr[k - 1:k, :], pltpu.roll(z, 1, 0))
        carry_scr[k - 1:k, :] = z[tc - 1:tc, :]
    return b_ref[...] + x * w_ref[width - 1:width, :] + delayed


def _tri(tc):
    r = lax.broadcasted_iota(I32, (tc, tc), 0)
    c = lax.broadcasted_iota(I32, (tc, tc), 1)
    return r >= c


def _eye(n):
    r = lax.broadcasted_iota(I32, (n, n), 0)
    c = lax.broadcasted_iota(I32, (n, n), 1)
    return (r == c).astype(F32)


def _cumsum_rows(x, causal):
    return jnp.dot(causal.astype(F32), x, precision=lax.Precision.HIGHEST, preferred_element_type=F32)


def _transpose_f32(x):
    return lax.dot_general(_eye(x.shape[1]), x, NT_DIMS, precision=lax.Precision.HIGHEST,
                           preferred_element_type=F32)


def _ssd_kernel(xbc_ref, z_ref, dt_ref, conv0_ref, h0_ref, cw_ref, cb_ref, dtb_ref, alog_ref, dskip_ref,
                ng_ref, y_ref, hs_ref, carry_scr, y_scr, *, tc, d_inner, n_heads):
    c = pl.program_id(1)

    @pl.when(c == 0)
    def _():
        hs_ref[...] = h0_ref[...]
        _conv_carry_init(conv0_ref, cw_ref, carry_scr, SSD_CONV)

    xs = _silu(_causal_conv_chunk(xbc_ref, carry_scr, cw_ref, cb_ref, tc, SSD_CONV))
    gn = SSD_GROUPS * SSD_D_STATE
    hg = n_heads // SSD_GROUPS
    causal = _tri(tc)
    x_dt = dt_ref[...] + dtb_ref[...]
    dt = jnp.maximum(x_dt, 0.0) + jnp.log1p(jnp.exp(-jnp.abs(x_dt)))
    a_head = -jnp.exp(alog_ref[...])
    cum = _cumsum_rows(dt * a_head, causal)
    cum_t = _transpose_f32(cum)
    cum_last = cum[tc - 1:tc, :]
    pair_w = 2 * SSD_HEAD_DIM
    assert pair_w == LANES and hg % 2 == 0
    first_lanes = lax.broadcasted_iota(I32, (tc, pair_w), 1) < SSD_HEAD_DIM
    first_rows = lax.broadcasted_iota(I32, (pair_w, SSD_D_STATE), 0) < SSD_HEAD_DIM
    pick = lambda a, b: jnp.where(first_lanes, a, b)
    for g in range(SSD_GROUPS):
        bm = xs[:, d_inner + g * SSD_D_STATE:d_inner + (g + 1) * SSD_D_STATE].astype(BF16)
        cm = xs[:, d_inner + gn + g * SSD_D_STATE:d_inner + gn + (g + 1) * SSD_D_STATE].astype(BF16)
        cb = lax.dot_general(cm, bm, NT_DIMS, preferred_element_type=F32)
        for pp in range(hg // 2):
            p = g * (hg // 2) + pp
            ha, hb = 2 * p, 2 * p + 1
            cols = slice(p * pair_w, (p + 1) * pair_w)
            col_a, col_b = cum[:, ha:ha + 1], cum[:, hb:hb + 1]
            xh = xs[:, cols]
            xdt = xh * pick(dt[:, ha:ha + 1], dt[:, hb:hb + 1])
            xdt_b = xdt.astype(BF16)
            ys = []
            for h, col in ((ha, col_a), (hb, col_b)):
                decay = jnp.exp(jnp.where(causal, col - cum_t[h:h + 1, :], -jnp.inf))
                ys.append(jnp.dot((cb * decay).astype(BF16), xdt_b, preferred_element_type=F32))
            h0 = hs_ref[p]
            inter = lax.dot_general(cm, h0.astype(BF16), NT_DIMS, preferred_element_type=F32)
            y = pick(ys[0], ys[1]) + pick(jnp.exp(col_a), jnp.exp(col_b)) * inter
            y_scr[:, cols] = y + dskip_ref[:, cols] * xh
            last_a, last_b = cum_last[:, ha:ha + 1], cum_last[:, hb:hb + 1]
            w = (pick(jnp.exp(last_a - col_a), jnp.exp(last_b - col_b)) * xdt).astype(BF16)
            keep = jnp.where(first_rows, jnp.exp(last_a), jnp.exp(last_b))
            hs_ref[p] = keep * h0 + lax.dot_general(w, bm, TN_DIMS, preferred_element_type=F32)
    yz = y_scr[...] * _silu(z_ref[...])
    gw = d_inner // SSD_GROUPS
    for g in range(SSD_GROUPS):
        cols = slice(g * gw, (g + 1) * gw)
        seg = yz[:, cols]
        ms = jnp.mean(seg * seg, axis=-1, keepdims=True)
        y_ref[:, cols] = (seg * lax.rsqrt(ms + NORM_EPS) * ng_ref[:, cols]).astype(y_ref.dtype)


def ssd_scan(xbc, z, dt, conv0, h0, conv_w, conv_b, dt_bias, a_log, d_skip_cols, norm_g, *, tc):
    bsz, length, cc = xbc.shape
    d_inner = z.shape[2]
    n_heads = dt.shape[2]
    assert length % tc == 0
    state_shape = h0.shape
    h0 = h0.reshape(bsz, n_heads // 2, 2 * state_shape[2], state_shape[3])
    seq = lambda w: pl.BlockSpec((None, tc, w), lambda b, c: (b, c, 0))
    per_b3 = lambda s: pl.BlockSpec((None,) + s, lambda b, c: (b,) + (0,) * len(s))
    row = lambda w: _resident((1, w), lambda b, c: (0, 0))
    y, h_new = pl.pallas_call(
        functools.partial(_ssd_kernel, tc=tc, d_inner=d_inner, n_heads=n_heads),
        grid=(bsz, length // tc),
        in_specs=[seq(cc), seq(d_inner), seq(n_heads), per_b3(conv0.shape[1:]), per_b3(h0.shape[1:]),
                  _resident(conv_w.shape, lambda b, c: (0, 0)), row(cc), row(n_heads), row(n_heads),
                  row(d_inner), row(d_inner)],
        out_specs=[seq(d_inner), per_b3(h0.shape[1:])],
        out_shape=[jax.ShapeDtypeStruct((bsz, length, d_inner), BF16), jax.ShapeDtypeStruct(h0.shape, F32)],
        scratch_shapes=[pltpu.VMEM((CARRY_ROWS, cc), F32), pltpu.VMEM((tc, d_inner), F32)],
        compiler_params=_params("parallel", "arbitrary"),
        name="ssd_scan",
    )(xbc, z, dt, conv0, h0, conv_w, conv_b, dt_bias, a_log, d_skip_cols, norm_g)
    return y, h_new.reshape(state_shape)


def _mlstm_kernel(xc_ref, v_ref, o_ref, gates_ref, conv0_ref, c0_ref, n0_ref, m0_ref, cw_ref, cb_ref, wq_ref,
                  wk_ref, gb_ref, ng_ref, h_ref, cs_ref, ns_ref, ms_ref, carry_scr, *, tc, d_inner):
    c = pl.program_id(1)

    @pl.when(c == 0)
    def _():
        cs_ref[...] = c0_ref[...]
        ns_ref[...] = n0_ref[...]
        ms_ref[...] = m0_ref[...]
        _conv_carry_init(conv0_ref, cw_ref, carry_scr, MLSTM_CONV)

    xa = _silu(_causal_conv_chunk(xc_ref, carry_scr, cw_ref, cb_ref, tc, MLSTM_CONV)).astype(BF16)
    blk = wq_ref.shape[1]
    dh = d_inner // MLSTM_HEADS
    q = jnp.concatenate([jnp.dot(xa[:, j * blk:(j + 1) * blk], wq_ref[j], preferred_element_type=F32)
                         for j in range(d_inner // blk)], axis=1)
    k = jnp.concatenate([jnp.dot(xa[:, j * blk:(j + 1) * blk], wk_ref[j], preferred_element_type=F32)
                         for j in range(d_inner // blk)], axis=1) * dh ** -0.5
    causal = _tri(tc)
    gates = gates_ref[...] + gb_ref[...]
    log_f = jnp.minimum(gates, 0.0) - jnp.log1p(jnp.exp(-jnp.abs(gates)))
    f_cum = _cumsum_rows(log_f, causal)
    f_cum_t = _transpose_f32(f_cum)
    gates_t = _transpose_f32(gates)
    for hh in range(MLSTM_HEADS):
        cols = slice(hh * dh, (hh + 1) * dh)
        fh = MLSTM_HEADS + hh
        qh = q[:, cols].astype(BF16)
        kh = k[:, cols]
        vh = v_ref[:, cols]
        fc = f_cum[:, fh:fh + 1]
        i_col = gates[:, hh:hh + 1]
        m0 = ms_ref[0:1, hh:hh + 1]
        d_log = jnp.where(causal, fc - f_cum_t[fh:fh + 1, :] + gates_t[hh:hh + 1, :], -jnp.inf)
        inter = fc + m0
        m = jnp.maximum(jnp.max(d_log, axis=-1, keepdims=True), inter)
        s = lax.dot_general(qh, kh.astype(BF16), NT_DIMS, preferred_element_type=F32) * jnp.exp(d_log - m)
        w_inter = jnp.exp(inter - m)
        c0 = cs_ref[hh]
        n0 = ns_ref[hh:hh + 1, :]
        num = (jnp.dot(s.astype(BF16), vh, preferred_element_type=F32)
               + w_inter * jnp.dot(qh, c0.astype(BF16), preferred_element_type=F32))
        den = (jnp.sum(s, axis=-1, keepdims=True)
               + w_inter * jnp.sum(q[:, cols] * n0, axis=-1, keepdims=True))
        h = num / jnp.maximum(jnp.abs(den), jnp.exp(-m))
        m_end = m[tc - 1:tc, :]
        f_last = fc[tc - 1:tc, :]
        w_end = jnp.exp(f_last - fc + i_col - m_end)
        decay = jnp.exp(f_last + m0 - m_end)
        wk = w_end * kh
        cs_ref[hh] = decay * c0 + lax.dot_general(wk.astype(BF16), vh, TN_DIMS, preferred_element_type=F32)
        ns_ref[hh:hh + 1, :] = decay * n0 + jnp.sum(wk, axis=0, keepdims=True)
        ms_ref[0:1, hh:hh + 1] = m_end
        h = jax.nn.sigmoid(o_ref[:, cols]) * h
        h = h - jnp.mean(h, axis=-1, keepdims=True)
        h = h * lax.rsqrt(jnp.mean(h * h, axis=-1, keepdims=True) + NORM_EPS)
        h_ref[:, cols] = (h * ng_ref[:, cols]).astype(h_ref.dtype)


def mlstm_scan(xc, v, o, gates, conv0, c0, n0, m0, conv_w, conv_b, wq_bd, wk_bd, gate_b, norm_g, *, tc):
    bsz, length, d_inner = xc.shape
    assert length % tc == 0
    seq = lambda w: pl.BlockSpec((None, tc, w), lambda b, c: (b, c, 0))
    per_b = lambda s: pl.BlockSpec((None,) + s, lambda b, c: (b,) + (0,) * len(s))
    row = lambda w: _resident((1, w), lambda b, c: (0, 0))
    return pl.pallas_call(
        functools.partial(_mlstm_kernel, tc=tc, d_inner=d_inner),
        grid=(bsz, length // tc),
        in_specs=[seq(d_inner), seq(d_inner), seq(d_inner), seq(gates.shape[2]), per_b(conv0.shape[1:]),
                  per_b(c0.shape[1:]), per_b(n0.shape[1:]), per_b(m0.shape[1:]),
                  _resident(conv_w.shape, lambda b, c: (0, 0)), row(d_inner),
                  _resident(wq_bd.shape, lambda b, c: (0, 0, 0)), _resident(wk_bd.shape, lambda b, c: (0, 0, 0)),
                  row(gates.shape[2]), row(d_inner)],
        out_specs=[seq(d_inner), per_b(c0.shape[1:]), per_b(n0.shape[1:]), per_b(m0.shape[1:])],
        out_shape=[jax.ShapeDtypeStruct((bsz, length, d_inner), BF16), jax.ShapeDtypeStruct(c0.shape, F32),
                   jax.ShapeDtypeStruct(n0.shape, F32), jax.ShapeDtypeStruct(m0.shape, F32)],
        scratch_shapes=[pltpu.VMEM((CARRY_ROWS, d_inner), F32)],
        compiler_params=_params("parallel", "arbitrary"),
        name="mlstm_scan",
    )(xc, v, o, gates, conv0, c0, n0, m0, conv_w, conv_b, wq_bd, wk_bd, gate_b, norm_g)


def _block_diag(w_blk, blk):
    n, c, d = w_blk.shape
    per = blk // c
    eye = jnp.eye(per, dtype=w_blk.dtype)
    tiles = w_blk.reshape(n // per, per, c, d)
    return jnp.einsum("jpcd,pq->jpcqd", tiles, eye).reshape(n // per, per * c, per * d)


def _mixer_a(xp, xs, shapes, cache_k, cache_v, cache_ki, w_in, t5_table):
    (bp, lp_), (bs, ls) = shapes
    hd, kvd, idd = A_HEADS * A_HEAD_DIM, A_KV_HEADS * A_HEAD_DIM, IDX_HEADS * IDX_DIM
    w_pad, cols = _pad_cols(w_in, (hd, kvd, kvd, idd, IDX_DIM, IDX_HEADS))
    outs = [(cols[0], BF16, A_HEAD_DIM ** -0.5), (cols[1], F32), (cols[1], BF16), (cols[2], F32), (cols[2], BF16),
            (cols[3], BF16), (cols[4], F32), (cols[4], BF16), (cols[5], F32)]
    w_pad = w_pad.astype(BF16)
    bias = _t5_bias_tiles(t5_table)
    on_lanes = lambda a: jnp.swapaxes(a, 1, 2)

    q, k, kb, v, vb, qi, ki, kib, wi = project(xp, w_pad, outs, name="project_a")
    r3 = lambda a, b_, l: a.reshape(b_, l, a.shape[-1])
    att_p = dsa_attention(r3(q, bp, lp_), on_lanes(r3(qi, bp, lp_)), on_lanes(r3(wi, bp, lp_)),
                          on_lanes(r3(kb, bp, lp_)), r3(vb, bp, lp_), r3(kib, bp, lp_), bias, tq=KEY_TILE, q_off=0,
                          l_true=lp_,
                          n_sel=min(IDX_TOPK_MAX, lp_ // 4))
    outs_p = (k.reshape(bp, lp_, A_KV_HEADS, A_HEAD_DIM), v.reshape(bp, lp_, A_KV_HEADS, A_HEAD_DIM),
              ki.reshape(bp, lp_, IDX_DIM))

    q, k, kb, v, vb, qi, ki, kib, wi = project(xs, w_pad, outs, name="project_a")
    past = cache_k.shape[1]
    total = past + ls
    lpad = -(-total // (2 * KEY_TILE)) * (2 * KEY_TILE)

    def with_past(cache, new):
        parts = [cache.reshape(bs, past, -1).astype(BF16), r3(new, bs, ls)]
        if lpad != total:
            parts.append(jnp.zeros((bs, lpad - total, new.shape[-1]), BF16))
        return jnp.concatenate(parts, axis=1)

    att_s = dsa_attention(r3(q, bs, ls), on_lanes(r3(qi, bs, ls)), on_lanes(r3(wi, bs, ls)),
                          on_lanes(with_past(cache_k, kb)), with_past(cache_v, vb), with_past(cache_ki, kib), bias,
                          tq=ls, q_off=past,
                          l_true=total, n_sel=min(IDX_TOPK_MAX, total // 4))
    outs_s = (k.reshape(bs, ls, A_KV_HEADS, A_HEAD_DIM), v.reshape(bs, ls, A_KV_HEADS, A_HEAD_DIM),
              ki.reshape(bs, ls, IDX_DIM))
    return att_p.reshape(bp * lp_, hd), att_s.reshape(bs * ls, hd), outs_p, outs_s


def _mixer_b(xp, xs, shapes, cache_k, cache_v, w_in, rel_table):
    (bp, lp_), (bs, ls) = shapes
    hd = BAND_HEADS * BAND_HEAD_DIM
    w_pad, cols = _pad_cols(w_in, (hd, hd, hd))
    outs = [(cols[0], BF16, BAND_HEAD_DIM ** -0.5), (cols[1], F32), (cols[1], BF16), (cols[2], F32),
            (cols[2], BF16)]
    w_pad = w_pad.astype(BF16)

    q, k, kb, v, vb = project(xp, w_pad, outs, name="project_b")
    r3 = lambda a, b_, l: a.reshape(b_, l, hd)
    att_p = band_attention_prompt(r3(q, bp, lp_), r3(kb, bp, lp_), r3(vb, bp, lp_),
                                  _band_bias(rel_table, CHUNK, BAND_ROWS), grp=BAND_ROWS)
    keep = min(BAND_WINDOW, lp_)
    heads = lambda a, b_, l: a.reshape(b_, l, BAND_HEADS, BAND_HEAD_DIM)
    outs_p = (heads(k, bp, lp_)[:, lp_ - keep:], heads(v, bp, lp_)[:, lp_ - keep:])

    q, k, kb, v, vb = project(xs, w_pad, outs, name="project_b")
    past = cache_k.shape[1]
    att_s = band_attention_sample(r3(q, bs, ls), cache_k.reshape(bs, past, hd).astype(BF16), r3(kb, bs, ls),
                                  cache_v.reshape(bs, past, hd).astype(BF16), r3(vb, bs, ls),
                                  _band_bias(rel_table, ls, ls))
    outs_s = (jnp.concatenate([cache_k, heads(k, bs, ls)], axis=1)[:, ls:],
              jnp.concatenate([cache_v, heads(v, bs, ls)], axis=1)[:, ls:])
    return att_p.reshape(bp * lp_, hd), att_s.reshape(bs * ls, hd), outs_p, outs_s


def _mixer_c(xp, xs, shapes, ssm0, conv0, w_in, conv_w, conv_b, dt_bias, a_log, d_skip, norm_g):
    n_heads = a_log.shape[0]
    d_inner = n_heads * SSD_HEAD_DIM
    conv_dim = conv_w.shape[1]
    w_pad, cols = _pad_cols(w_in, (d_inner, conv_dim, n_heads))
    outs = [(cols[0], F32), (cols[1], F32), (cols[2], F32)]
    w_pad = w_pad.astype(BF16)
    row = lambda a: a.reshape(1, -1)
    d_skip_cols = jnp.repeat(d_skip, SSD_HEAD_DIM).reshape(1, d_inner)
    res = []
    for x, (b_, l), h0, c0 in ((xp, shapes[0], None, None), (xs, shapes[1], ssm0, conv0)):
        if h0 is None:
            h0 = jnp.zeros((b_, n_heads, SSD_HEAD_DIM, SSD_D_STATE), F32)
            c0 = jnp.zeros((b_, SSD_CONV - 1, conv_dim), F32)
        z, xbc, dt = project(x, w_pad, outs, name="project_c")
        xbc3 = xbc.reshape(b_, l, conv_dim)
        y, h_new = ssd_scan(xbc3, z.reshape(b_, l, d_inner), dt.reshape(b_, l, n_heads), c0, h0, conv_w,
                            row(conv_b), row(dt_bias), row(a_log), d_skip_cols, row(norm_g), tc=min(SSD_ROWS, l))
        conv_new = jnp.concatenate([c0, xbc3], axis=1)[:, l:]
        res.append((y.reshape(b_ * l, d_inner), (h_new, conv_new)))
    return res[0][0], res[1][0], res[0][1], res[1][1]


def _mixer_d(xp, xs, shapes, c0s, n0s, m0s, conv0s, w_in, conv_w, conv_b, wq_blk, wk_blk, gate_b, norm_g):
    d_inner = conv_w.shape[1]
    dh = d_inner // MLSTM_HEADS
    w_pad, cols = _pad_cols(w_in, (d_inner, d_inner, d_inner, 2 * MLSTM_HEADS))
    gcol = (cols[3][0], cols[3][1], cols[3][1])
    outs = [(cols[0], F32), (cols[1], BF16), (cols[2], F32), (gcol, F32)]
    w_pad = w_pad.astype(BF16)
    blk = 2 * LANES
    wq_bd = _block_diag(wq_blk, blk).astype(BF16)
    wk_bd = _block_diag(wk_blk, blk).astype(BF16)
    gate_b_pad = jnp.zeros((1, gcol[1]), F32).at[0, :2 * MLSTM_HEADS].set(gate_b)
    row = lambda a: a.reshape(1, -1)
    res = []
    for x, (b_, l), st in ((xp, shapes[0], None), (xs, shapes[1], (c0s, n0s, m0s, conv0s))):
        if st is None:
            st = (jnp.zeros((b_, MLSTM_HEADS, dh, dh), F32), jnp.zeros((b_, MLSTM_HEADS, dh), F32),
                  jnp.zeros((b_, MLSTM_HEADS), F32), jnp.zeros((b_, MLSTM_CONV - 1, d_inner), F32))
        c0, n0, m0, conv0 = st
        xc, v, o, gates = project(x, w_pad, outs, name="project_d")
        r3 = lambda a: a.reshape(b_, l, a.shape[-1])
        h, c_new, n_new, m_new = mlstm_scan(r3(xc), r3(v), r3(o), r3(gates), conv0, c0, n0,
                                            m0.reshape(b_, 1, MLSTM_HEADS), conv_w, row(conv_b), wq_bd, wk_bd,
                                            gate_b_pad, row(norm_g), tc=min(MLSTM_ROWS, l))
        conv_new = jnp.concatenate([conv0, r3(xc)], axis=1)[:, l:]
        res.append((h.reshape(b_ * l, d_inner), (c_new, n_new, m_new.reshape(b_, MLSTM_HEADS), conv_new)))
    return res[0][0], res[1][0], res[0][1], res[1][1]


def kernel(x_prompt, x_sample, cache_a_k, cache_a_v, cache_a_kidx, cache_b_k, cache_b_v, state_c_ssm, state_c_conv, state_d_c, state_d_n, state_d_m, state_d_conv, a_w_in, a_w_out, t5_table, b_w_in, b_w_out, b_rel_table, c_w_in, c_conv_w, c_conv_b, c_dt_bias, c_a_log, c_d_skip, c_norm_g, c_w_out, d_w_in, d_conv_w, d_conv_b, d_wq_blk, d_wk_blk, d_gate_b, d_norm_g, d_w_out, ffn1_wg, ffn1_wu, ffn1_wd, ffn2_wg, ffn2_wu, ffn2_wd, ln_g, ln_b):
    bp, lp_, d = x_prompt.shape
    bs, ls, _ = x_sample.shape
    depth = ffn1_wg.shape[0]
    alpha = (2.0 * depth) ** 0.25
    shapes = ((bp, lp_), (bs, ls))
    xp = x_prompt.reshape(bp * lp_, d)
    xs = x_sample.reshape(bs * ls, d)
    ffn_w = [[w.astype(BF16) for w in ws] for ws in ((ffn1_wg, ffn1_wu, ffn1_wd), (ffn2_wg, ffn2_wu, ffn2_wd))]
    w_out = [w.astype(BF16) for w in (a_w_out, b_w_out, c_w_out, d_w_out)]
    row = lambda a: a.reshape(1, -1)
    extra = {}
    for i in range(depth):
        g, b = ln_g[i], ln_b[i]
        xp = ffn_postnorm(xp, *ffn_w[0], i, row(g[0]), row(b[0]), alpha)
        xs = ffn_postnorm(xs, *ffn_w[0], i, row(g[0]), row(b[0]), alpha)
        kind = i % 4
        if kind == 0:
            mp, ms, op, os_ = _mixer_a(xp, xs, shapes, cache_a_k, cache_a_v, cache_a_kidx, a_w_in, t5_table)
        elif kind == 1:
            mp, ms, op, os_ = _mixer_b(xp, xs, shapes, cache_b_k, cache_b_v, b_w_in, b_rel_table)
        elif kind == 2:
            mp, ms, op, os_ = _mixer_c(xp, xs, shapes, state_c_ssm, state_c_conv, c_w_in, c_conv_w, c_conv_b,
                                       c_dt_bias, c_a_log, c_d_skip, c_norm_g)
        else:
            mp, ms, op, os_ = _mixer_d(xp, xs, shapes, state_d_c, state_d_n, state_d_m, state_d_conv, d_w_in,
                                       d_conv_w, d_conv_b, d_wq_blk, d_wk_blk, d_gate_b, d_norm_g)
        extra[kind] = (op, os_)
        xp = mixout_ffn(xp, mp, w_out[kind], row(g[1]), row(b[1]), *ffn_w[1], i, row(g[2]), row(b[2]), alpha)
        xs = mixout_ffn(xs, ms, w_out[kind], row(g[1]), row(b[1]), *ffn_w[1], i, row(g[2]), row(b[2]), alpha)
    prompt_side = tuple(t for kind in range(4) for t in extra[kind][0])
    sample_side = tuple(t for kind in range(4) for t in extra[kind][1])
    return (xp.reshape(bp, lp_, d), xs.reshape(bs, ls, d)) + prompt_side + sample_side
```

```python
import functools
import math

import numpy as np
import jax
import jax.numpy as jnp
from jax import lax
from jax.experimental import pallas as pl
from jax.experimental.pallas import tpu as pltpu

F32 = jnp.float32
BF16 = jnp.bfloat16
I32 = jnp.int32

CHUNK = 64
NORM_EPS = 1e-5
A_HEADS, A_KV_HEADS, A_HEAD_DIM = 8, 2, 128
IDX_HEADS, IDX_DIM, IDX_TOPK_MAX = 8, 64, 256
T5_BUCKETS, T5_MAX_DIST = 32, 128
BAND_HEADS, BAND_HEAD_DIM, BAND_LEFT_CHUNKS, BAND_MAX_REL = 16, 64, 8, 128
BAND_WINDOW = BAND_LEFT_CHUNKS * CHUNK
SSD_HEAD_DIM, SSD_GROUPS, SSD_D_STATE, SSD_CONV = 64, 8, 128, 4
MLSTM_HEADS, MLSTM_CONV, MLSTM_QK_BLOCK = 4, 4, 4

LANES = 128
NEG_BIG = -1e30
VMEM_LIMIT = 56 * 1024 * 1024

NT_DIMS = (((1,), (1,)), ((), ()))
TN_DIMS = (((0,), (0,)), ((), ()))


def _params(*sem):
    return pltpu.CompilerParams(dimension_semantics=sem, vmem_limit_bytes=VMEM_LIMIT)


def _resident(shape, index_map):
    return pl.BlockSpec(shape, index_map, pipeline_mode=pl.Buffered(1))


def _layer_norm(y, g, b):
    mu = jnp.mean(y, axis=-1, keepdims=True)
    yc = y - mu
    var = jnp.mean(yc * yc, axis=-1, keepdims=True)
    return yc * lax.rsqrt(var + NORM_EPS) * g + b


def _silu(x):
    return x * jax.nn.sigmoid(x)


def _row_tile(n, want):
    t = min(n, want)
    assert n % t == 0, (n, t)
    return t


def _ffn_block(x, wg_ref, wu_ref, wd_ref, g_ref, b_ref, alpha, f_cuts):
    xb = x.astype(BF16)
    acc = jnp.zeros(x.shape, F32)
    for lo, hi in zip(f_cuts[:-1], f_cuts[1:]):
        sl = slice(lo, hi)
        gate = jnp.dot(xb, wg_ref[:, sl], preferred_element_type=F32)
        up = jnp.dot(xb, wu_ref[:, sl], preferred_element_type=F32)
        h = (_silu(gate) * up).astype(BF16)
        acc = acc + jnp.dot(h, wd_ref[sl, :], preferred_element_type=F32)
    return _layer_norm(alpha * x + 0.5 * acc, g_ref[...], b_ref[...])


def _ffn_kernel(x_ref, wg_ref, wu_ref, wd_ref, g_ref, b_ref, o_ref, *, alpha, f_cuts):
    o_ref[...] = _ffn_block(x_ref[...], wg_ref, wu_ref, wd_ref, g_ref, b_ref, alpha, f_cuts)


def _mixout_ffn_kernel(x_ref, m_ref, wo_ref, g1_ref, b1_ref, wg_ref, wu_ref, wd_ref, g2_ref, b2_ref, o_ref, *,
                       alpha, f_cuts):
    sub = jnp.dot(m_ref[...], wo_ref[...], preferred_element_type=F32)
    x1 = _layer_norm(alpha * x_ref[...] + sub, g1_ref[...], b1_ref[...])
    o_ref[...] = _ffn_block(x1, wg_ref, wu_ref, wd_ref, g2_ref, b2_ref, alpha, f_cuts)


MXU_WIDTH = 256


def _ffn_cuts(d_ff, n_chunks):
    tiles = -(-d_ff // MXU_WIDTH)
    cuts = [min(d_ff, MXU_WIDTH * (-(-tiles * c // n_chunks))) for c in range(n_chunks + 1)]
    return tuple(cuts)


def ffn_postnorm(x, wg, wu, wd, layer, g, b, alpha, tm=512, n_chunks=2):
    n, d = x.shape
    d_ff = wg.shape[2]
    tm = _row_tile(n, tm)
    return pl.pallas_call(
        functools.partial(_ffn_kernel, alpha=alpha, f_cuts=_ffn_cuts(d_ff, n_chunks)),
        grid=(n // tm,),
        in_specs=[
            pl.BlockSpec((tm, d), lambda i: (i, 0)),
            _resident((None, d, d_ff), lambda i: (layer, 0, 0)),
            _resident((None, d, d_ff), lambda i: (layer, 0, 0)),
            _resident((None, d_ff, d), lambda i: (layer, 0, 0)),
            _resident((1, d), lambda i: (0, 0)),
            _resident((1, d), lambda i: (0, 0)),
        ],
        out_specs=pl.BlockSpec((tm, d), lambda i: (i, 0)),
        out_shape=jax.ShapeDtypeStruct((n, d), F32),
        compiler_params=_params("parallel"),
        name="ffn_postnorm",
    )(x, wg, wu, wd, g, b)


def mixout_ffn(x, m, w_out, g1, b1, wg, wu, wd, layer, g2, b2, alpha, tm=512, n_chunks=2):
    n, d = x.shape
    k = m.shape[1]
    d_ff = wg.shape[2]
    tm = _row_tile(n, tm)
    vec = lambda: _resident((1, d), lambda i: (0, 0))
    return pl.pallas_call(
        functools.partial(_mixout_ffn_kernel, alpha=alpha, f_cuts=_ffn_cuts(d_ff, n_chunks)),
        grid=(n // tm,),
        in_specs=[
            pl.BlockSpec((tm, d), lambda i: (i, 0)),
            pl.BlockSpec((tm, k), lambda i: (i, 0)),
            _resident((k, d), lambda i: (0, 0)), vec(), vec(),
            _resident((None, d, d_ff), lambda i: (layer, 0, 0)),
            _resident((None, d, d_ff), lambda i: (layer, 0, 0)),
            _resident((None, d_ff, d), lambda i: (layer, 0, 0)), vec(), vec(),
        ],
        out_specs=pl.BlockSpec((tm, d), lambda i: (i, 0)),
        out_shape=jax.ShapeDtypeStruct((n, d), F32),
        compiler_params=_params("parallel"),
        name="mixout_ffn",
    )(x, m, w_out, g1, b1, wg, wu, wd, g2, b2)


def _proj_kernel(x_ref, w_ref, *o_refs, cols, scales):
    xb = x_ref[...].astype(BF16)
    done = {}
    for (off, pad_w, true_w), scale, o_ref in zip(cols, scales, o_refs):
        if (off, pad_w) not in done:
            done[(off, pad_w)] = jnp.dot(xb, w_ref[:, off:off + pad_w], preferred_element_type=F32)
        y = done[(off, pad_w)][:, :true_w]
        y = (y if scale is None else y * scale).astype(o_ref.dtype)
        if len(o_ref.shape) == 3:
            head_w = o_ref.shape[2]
            for p in range(o_ref.shape[1]):
                o_ref[:, p, :] = y[:, p * head_w:(p + 1) * head_w]
        else:
            o_ref[...] = y


def _pad_cols(w, widths):
    pieces, offs, off, src = [], [], 0, 0
    for wd in widths:
        pad_w = -(-wd // LANES) * LANES
        pieces.append(w[:, src:src + wd])
        if pad_w != wd:
            pieces.append(jnp.zeros((w.shape[0], pad_w - wd), w.dtype))
        offs.append((off, pad_w, wd))
        off += pad_w
        src += wd
    return jnp.concatenate(pieces, axis=1), offs


def project(x, w_pad, outs, tm=512, name="project"):
    n, d = x.shape
    tm = _row_tile(n, tm)
    cols = tuple(o[0] for o in outs)
    scales = tuple(o[2] if len(o) > 2 else None for o in outs)

    def tail(o):
        heads = o[3] if len(o) > 3 else None
        return (o[0][2],) if heads is None else (heads, o[0][2] // heads)

    return pl.pallas_call(
        functools.partial(_proj_kernel, cols=cols, scales=scales),
        grid=(n // tm,),
        in_specs=[pl.BlockSpec((tm, d), lambda i: (i, 0)),
                  _resident(w_pad.shape, lambda i: (0, 0))],
        out_specs=[pl.BlockSpec((tm,) + tail(o), lambda i, nd=len(tail(o)): (i,) + (0,) * nd) for o in outs],
        out_shape=[jax.ShapeDtypeStruct((n,) + tail(o), o[1]) for o in outs],
        compiler_params=_params("parallel"),
        name=name,
    )(x, w_pad)


def _t5_bucket_np(rel):
    half = T5_BUCKETS // 2
    max_exact = half // 2
    n = np.abs(rel)
    nf = np.maximum(n, 1).astype(np.float32)
    large = max_exact + (np.log(nf / np.float32(max_exact)) / np.float32(math.log(T5_MAX_DIST / max_exact))
                         * np.float32(half - max_exact)).astype(np.int32)
    large = np.minimum(large, half - 1)
    return np.where(rel > 0, half, 0) + np.where(n < max_exact, n, large)


KEY_TILE = 2 * LANES


def _toeplitz(w, t_rows, s_cols):
    n = t_rows + s_cols
    lead = w.shape[:-1]
    wp = jnp.concatenate([w, jnp.zeros(lead + (1,), w.dtype)], axis=-1)
    flat = jnp.broadcast_to(wp[..., None, :], lead + (t_rows, n)).reshape(lead + (t_rows * n,))
    skew = flat[..., :t_rows * (n - 1)].reshape(lead + (t_rows, n - 1))
    return skew[..., t_rows - 1:t_rows - 1 + s_cols]


def _t5_bias_tiles(t5_table):
    assert T5_MAX_DIST <= KEY_TILE
    rel = np.arange(3 * KEY_TILE - 1) - (KEY_TILE - 1) - KEY_TILE
    far = t5_table[int(_t5_bucket_np(np.array(-T5_MAX_DIST)))]
    by_rel = (t5_table[_t5_bucket_np(rel)] - far).T
    strip = _toeplitz(by_rel, KEY_TILE, 2 * KEY_TILE)
    return jnp.stack([strip[:, :, KEY_TILE:], strip[:, :, :KEY_TILE]])


def _dsa_kernel(q_ref, qit_ref, wit_ref, kt_ref, v_ref, ki_ref, bias_ref, o_ref,
                keyt_scr, mask_scr, lg_scr, m_scr, acc_scr, *, tq, q_off, l_true, n_sel, lp):
    j = pl.program_id(1)
    q0 = q_off + j * tq
    kend = jnp.minimum(((q0 + tq - 1) // CHUNK + 1) * CHUNK, l_true)
    n_tiles = (kend + KEY_TILE - 1) // KEY_TILE
    n_far = jnp.maximum(q0 // KEY_TILE - 1, 0)
    int_min = jnp.int32(-2 ** 31)
    wide = 2 * KEY_TILE

    def tile_off(kt):
        return pl.multiple_of(kt * KEY_TILE, KEY_TILE)

    def slabs(lo, hi, fn):
        def pair(i, carry):
            fn(tile_off(lo + 2 * i), wide)
            return carry
        lax.fori_loop(0, (hi - lo) // 2, pair, 0)

        @pl.when((hi - lo) % 2 == 1)
        def _():
            fn(tile_off(hi - 1), KEY_TILE)

    wi = wit_ref[...] * (IDX_HEADS ** -0.5) * (IDX_DIM ** -0.5)

    def positions(off, n_keys):
        kpos = off + lax.broadcasted_iota(I32, (n_keys, tq), 0)
        q_chunk = (q0 + lax.broadcasted_iota(I32, (n_keys, tq), 1)) // CHUNK
        return kpos, ((kpos // CHUNK) <= q_chunk) & (kpos < l_true)

    def to_key(x):
        bits = lax.bitcast_convert_type(x, I32)
        return jnp.where(bits < 0, (bits ^ jnp.int32(0x7FFFFFFF)) + 1, bits)

    last_off = tile_off(n_tiles - 1)

    def score_slab(off, width, check_admissible=False):
        ki_t = ki_ref[pl.ds(off, width), :]
        accs = [jnp.zeros((width, tq), F32)] * 2
        for h in range(IDX_HEADS):
            s = jnp.dot(ki_t, qit_ref[h * IDX_DIM:(h + 1) * IDX_DIM, :], preferred_element_type=F32)
            accs[h % 2] = accs[h % 2] + jnp.maximum(s, 0.0) * wi[h:h + 1, :]
        sc = accs[0] + accs[1]
        if check_admissible:
            sc = jnp.where(positions(off, width)[1], sc, -jnp.inf)
        keyt_scr[pl.ds(off, width), :] = to_key(sc)

    slabs(0, n_tiles - 1, score_slab)
    score_slab(last_off, KEY_TILE, check_admissible=True)

    @pl.when(n_tiles % 2 == 1)
    def _():
        keyt_scr[pl.ds(tile_off(n_tiles), KEY_TILE), :] = to_key(jnp.full((KEY_TILE, tq), -jnp.inf, F32))

    sublanes = 8

    def count(pred):
        def body(ct, acc):
            off = pl.multiple_of(ct * wide, wide)
            kpos = off + lax.broadcasted_iota(I32, (wide, tq), 0)
            hit = pred(keyt_scr[pl.ds(off, wide), :], kpos).astype(I32)
            return acc + jnp.sum(hit.reshape(wide // sublanes, sublanes, tq), axis=0)
        acc = lax.fori_loop(0, (n_tiles + 1) // 2, body, jnp.zeros((sublanes, tq), I32))
        return jnp.sum(acc.astype(F32), axis=0, keepdims=True)

    check_every = 4

    def search_body(state):
        i, thr_u, settled, _ = state

        def one_bit(b, st):
            thr_u, settled = st
            cand_u = thr_u | lax.shift_left(jnp.int32(1), 31 - (i + b))
            cand = cand_u ^ int_min
            cnt = count(lambda key, kpos: key >= cand)
            thr_u = jnp.where((cnt >= n_sel) & (settled == 0), cand_u, thr_u)
            return thr_u, jnp.where(cnt == n_sel, 1, settled)

        thr_u, settled = lax.fori_loop(0, check_every, one_bit, (thr_u, settled))
        return i + check_every, thr_u, settled, jnp.min(settled.astype(F32))

    zeros = jnp.zeros((1, tq), I32)
    _, thr_u, _, _ = lax.while_loop(lambda s: (s[0] < 32) & (s[3] == 0.0), search_body,
                                    (jnp.int32(0), zeros, zeros, jnp.float32(0.0)))
    thr = thr_u ^ int_min
    need = n_sel - count(lambda key, kpos: key > thr)
    n_eq = count(lambda key, kpos: key == thr)
    has_tie = jnp.max(n_eq - need) > 0.0

    nbits = int(lp).bit_length()

    def tie_cut():
        def cut_body(i, cut):
            cand = cut | lax.shift_left(jnp.int32(1), nbits - 1 - i)
            cnt = count(lambda key, kpos: (key == thr) & (kpos < cand))
            return jnp.where(cnt <= need, cand, cut)
        return lax.fori_loop(0, nbits, cut_body, jnp.zeros((1, tq), I32))

    cut = lax.cond(has_tie, tie_cut, lambda: jnp.full((1, tq), 2 ** 30, I32))

    eye = (lax.broadcasted_iota(I32, (tq, tq), 0) == lax.broadcasted_iota(I32, (tq, tq), 1)).astype(BF16)

    def mask_tile(off, tie, check_admissible=False):
        key = keyt_scr[pl.ds(off, KEY_TILE), :]
        kpos, adm = positions(off, KEY_TILE)
        sel = ((key > thr) | ((key == thr) & (kpos < cut))) if tie else (key >= thr)
        if check_admissible:
            sel = sel & adm
        sel_q = lax.dot_general(eye, jnp.where(sel, 1.0, 0.0).astype(BF16), NT_DIMS, preferred_element_type=F32)
        mask_scr[:, pl.ds(off, KEY_TILE)] = jnp.where(sel_q > 0.5, 0.0, NEG_BIG)

    def mask_all(tie):
        def slab(off, width):
            for u in range(width // KEY_TILE):
                mask_tile(pl.multiple_of(off + u * KEY_TILE, KEY_TILE), tie)

        def run():
            slabs(0, n_tiles - 1, slab)
            mask_tile(last_off, tie, check_admissible=True)
        return run

    lax.cond(has_tie, mask_all(True), mask_all(False))

    grp = A_HEADS // A_KV_HEADS
    rows = grp * tq
    head_cols = lambda h: slice(h * A_HEAD_DIM, (h + 1) * A_HEAD_DIM)
    for g in range(A_KV_HEADS):
        qg = jnp.concatenate([q_ref[:, head_cols(g * grp + i)] for i in range(grp)], axis=0)
        m_scr[...] = jnp.full(m_scr.shape, NEG_BIG, F32)

        def qk_slab(off, width, near=False, qg=qg, g=g):
            lg = jnp.dot(qg, kt_ref[head_cols(g), pl.ds(off, width)], preferred_element_type=F32)
            lg = lg.reshape(grp, tq, width) + mask_scr[:, pl.ds(off, width)][None]
            if near:
                back = jnp.clip((q0 - off) // KEY_TILE, 0, 1)
                lg = lg + bias_ref[back, g * grp:(g + 1) * grp, 0:tq, :]
            lg = lg.reshape(rows, width)
            lg_scr[:, pl.ds(off, width)] = lg
            mx = lg[:, 0:LANES]
            for u in range(1, width // LANES):
                mx = jnp.maximum(mx, lg[:, u * LANES:(u + 1) * LANES])
            m_scr[...] = jnp.maximum(m_scr[...], mx)

        def near_body(kt, carry, qk_slab=qk_slab):
            qk_slab(tile_off(kt), KEY_TILE, near=True)
            return carry

        slabs(0, n_far, qk_slab)
        lax.fori_loop(n_far, n_tiles, near_body, 0)
        m_scr[...] = jnp.broadcast_to(jnp.max(m_scr[...], axis=-1, keepdims=True), m_scr.shape)
        acc_scr[...] = jnp.zeros(acc_scr.shape, F32)

        def pv_slab(off, width, g=g):
            lg = lg_scr[:, pl.ds(off, width)]
            m_b = m_scr[...]
            ps = [jnp.exp2(lg[:, u * LANES:(u + 1) * LANES] - m_b) for u in range(width // LANES)]
            p = jnp.concatenate(ps, axis=1).astype(BF16)
            v_ones = v_ref[pl.ds(off, width), 2 * g * A_HEAD_DIM:2 * (g + 1) * A_HEAD_DIM]
            acc_scr[...] += jnp.dot(p, v_ones, preferred_element_type=F32)

        slabs(0, n_tiles, pv_slab)
        out = acc_scr[:, 0:A_HEAD_DIM] / acc_scr[:, A_HEAD_DIM:2 * A_HEAD_DIM]
        for i in range(grp):
            o_ref[:, head_cols(g * grp + i)] = out[i * tq:(i + 1) * tq].astype(o_ref.dtype)


def dsa_attention(q, qi_t, wi_t, k_t, v, ki, bias, *, tq, q_off, l_true, n_sel):
    bsz, t_len, _ = q.shape
    lp = v.shape[1]
    assert lp % (2 * KEY_TILE) == 0 and t_len % tq == 0 and q_off % KEY_TILE == 0
    assert tq == KEY_TILE or (t_len == tq and tq < KEY_TILE)
    grp = A_HEADS // A_KV_HEADS
    qspec = lambda w: pl.BlockSpec((None, tq, w), lambda b, j: (b, j, 0))
    qtspec = lambda h: pl.BlockSpec((None, h, tq), lambda b, j: (b, 0, j))
    whole = lambda a: pl.BlockSpec((None,) + a.shape[1:], lambda b, j: (b, 0, 0))
    return pl.pallas_call(
        functools.partial(_dsa_kernel, tq=tq, q_off=q_off, l_true=l_true, n_sel=n_sel, lp=lp),
        grid=(bsz, t_len // tq),
        in_specs=[qspec(q.shape[2]), qtspec(qi_t.shape[1]), qtspec(wi_t.shape[1]),
                  whole(k_t), whole(v), whole(ki),
                  _resident(bias.shape, lambda b, j: (0, 0, 0, 0))],
        out_specs=qspec(q.shape[2]),
        out_shape=jax.ShapeDtypeStruct(q.shape, BF16),
        scratch_shapes=[pltpu.VMEM((lp, tq), I32), pltpu.VMEM((tq, lp), F32), pltpu.VMEM((grp * tq, lp), F32),
                        pltpu.VMEM((grp * tq, LANES), F32), pltpu.VMEM((grp * tq, 2 * A_HEAD_DIM), F32)],
        compiler_params=_params("parallel", "arbitrary"),
        name="dsa_attention",
    )(q, qi_t, wi_t, k_t, v, ki, bias)


BAND_ROWS = 4 * CHUNK


def _band_bias(rel_table, chunk, grp):
    win = BAND_WINDOW + grp
    t = np.arange(grp)[:, None]
    s = np.arange(win)[None, :]
    k = np.arange(grp + win - 1)
    idx = np.clip(BAND_WINDOW + (grp - 1) - k, -BAND_MAX_REL, BAND_MAX_REL) + BAND_MAX_REL
    bias = _toeplitz(rel_table[:, idx], grp, win)
    lo = (t // chunk) * chunk
    allowed = (s >= lo) & (s < lo + BAND_WINDOW + chunk)
    return jnp.where(allowed[None], bias, NEG_BIG)


def _band_kernel(q_ref, kp_ref, kc_ref, vp_ref, vc_ref, bias_ref, o_ref, kcat, vcat, *, grp, first_has_no_past):
    i = pl.program_id(1)
    prev = kp_ref.shape[0]
    tq = q_ref.shape[0]
    win = prev + grp
    kcat[0:prev, :] = kp_ref[...]
    kcat[prev:prev + tq, :] = kc_ref[...]
    vcat[0:prev, :] = vp_ref[...]
    vcat[prev:prev + tq, :] = vc_ref[...]
    lane = lax.broadcasted_iota(I32, (grp, win), 1)
    pair_w = 2 * BAND_HEAD_DIM
    assert pair_w == LANES
    first_half = lax.broadcasted_iota(I32, (grp, pair_w), 1) < BAND_HEAD_DIM

    def attend(mask_missing_past):
        for r0 in range(0, tq, grp):
            if mask_missing_past:
                kpos_ok = (r0 - prev + lane) >= 0
            for hp in range(BAND_HEADS // 2):
                cols = slice(hp * pair_w, (hp + 1) * pair_w)
                q_pair = q_ref[r0:r0 + grp, cols]
                k_pair = kcat[r0:r0 + win, cols]
                v_pair = vcat[r0:r0 + win, cols]
                outs = []
                for side in range(2):
                    q_one = jnp.where(first_half == (side == 0), q_pair, jnp.zeros_like(q_pair))
                    lg = lax.dot_general(q_one, k_pair, NT_DIMS, preferred_element_type=F32)
                    lg = lg + bias_ref[2 * hp + side]
                    if mask_missing_past:
                        lg = jnp.where(kpos_ok, lg, NEG_BIG)
                    p = jnp.exp(lg - jnp.max(lg, axis=-1, keepdims=True))
                    den = jnp.sum(p, axis=-1, keepdims=True)
                    outs.append(jnp.dot(p.astype(BF16), v_pair, preferred_element_type=F32) / den)
                o_ref[r0:r0 + grp, cols] = jnp.where(first_half, outs[0], outs[1]).astype(o_ref.dtype)

    if first_has_no_past:
        lax.cond(i == 0, lambda: attend(True), lambda: attend(False))
    else:
        attend(False)


def band_attention_prompt(q, k, v, bias, *, tq=512, grp=128):
    bsz, length, w = q.shape
    assert tq == BAND_WINDOW and length % tq == 0
    cur = pl.BlockSpec((None, tq, w), lambda b, i: (b, i, 0))
    prv = pl.BlockSpec((None, tq, w), lambda b, i: (b, jnp.maximum(i - 1, 0), 0))
    return pl.pallas_call(
        functools.partial(_band_kernel, grp=grp, first_has_no_past=True),
        grid=(bsz, length // tq),
        in_specs=[cur, prv, cur, prv, cur, _resident(bias.shape, lambda b, i: (0, 0, 0))],
        out_specs=cur,
        out_shape=jax.ShapeDtypeStruct(q.shape, BF16),
        scratch_shapes=[pltpu.VMEM((2 * tq, w), BF16), pltpu.VMEM((2 * tq, w), BF16)],
        compiler_params=_params("parallel", "arbitrary"),
        name="band_attention",
    )(q, k, k, v, v, bias)


def band_attention_sample(q, k_past, k_new, v_past, v_new, bias):
    bsz, t_len, w = q.shape
    past = k_past.shape[1]
    assert past == BAND_WINDOW
    new = pl.BlockSpec((None, t_len, w), lambda b, i: (b, 0, 0))
    old = pl.BlockSpec((None, past, w), lambda b, i: (b, 0, 0))
    return pl.pallas_call(
        functools.partial(_band_kernel, grp=t_len, first_has_no_past=False),
        grid=(bsz, 1),
        in_specs=[new, old, new, old, new, _resident(bias.shape, lambda b, i: (0, 0, 0))],
        out_specs=new,
        out_shape=jax.ShapeDtypeStruct(q.shape, BF16),
        scratch_shapes=[pltpu.VMEM((past + t_len, w), BF16), pltpu.VMEM((past + t_len, w), BF16)],
        compiler_params=_params("parallel", "arbitrary"),
        name="band_attention_sample",
    )(q, k_past, k_new, v_past, v_new, bias)


CARRY_ROWS = 8
SSD_ROWS = 128
MLSTM_ROWS = 256


def _conv_carry_init(conv0_ref, w_ref, carry_scr, width):
    for k in range(1, width):
        z = None
        for i in range(k, width):
            term = w_ref[width - 1 - i:width - i, :] * conv0_ref[width - 2 - (i - k):width - 1 - (i - k), :]
            z = term if z is None else z + term
        carry_scr[k - 1:k, :] = z


def _causal_conv_chunk(x_ref, carry_scr, w_ref, b_ref, tc, width):
    x = x_ref[...]
    first_row = lax.broadcasted_iota(I32, x.shape, 0) == 0
    z = None
    for k in range(width - 1, 0, -1):
        y = x * w_ref[width - 1 - k:width - k, :]
        z = y if z is None else y + delayed
        delayed = jnp.where(first_row, carry_scr[k - 1:k, :], pltpu.roll(z, 1, 0))
        carry_scr[k - 1:k, :] = z[tc - 1:tc, :]
    return b_ref[...] + x * w_ref[width - 1:width, :] + delayed


def _tri(tc):
    r = lax.broadcasted_iota(I32, (tc, tc), 0)
    c = lax.broadcasted_iota(I32, (tc, tc), 1)
    return r >= c


def _eye(n):
    r = lax.broadcasted_iota(I32, (n, n), 0)
    c = lax.broadcasted_iota(I32, (n, n), 1)
    return (r == c).astype(F32)


def _cumsum_rows(x, causal):
    return jnp.dot(causal.astype(F32), x, precision=lax.Precision.HIGHEST, preferred_element_type=F32)


def _transpose_f32(x):
    return lax.dot_general(_eye(x.shape[1]), x, NT_DIMS, precision=lax.Precision.HIGHEST,
                           preferred_element_type=F32)


def _ssd_kernel(xbc_ref, z_ref, dt_ref, conv0_ref, h0_ref, cw_ref, cb_ref, dtb_ref, alog_ref, dskip_ref,
                ng_ref, y_ref, hs_ref, carry_scr, y_scr, *, tc, d_inner, n_heads):
    c = pl.program_id(1)

    @pl.when(c == 0)
    def _():
        hs_ref[...] = h0_ref[...]
        _conv_carry_init(conv0_ref, cw_ref, carry_scr, SSD_CONV)

    xs = _silu(_causal_conv_chunk(xbc_ref, carry_scr, cw_ref, cb_ref, tc, SSD_CONV))
    gn = SSD_GROUPS * SSD_D_STATE
    hg = n_heads // SSD_GROUPS
    causal = _tri(tc)
    x_dt = dt_ref[...] + dtb_ref[...]
    dt = jnp.maximum(x_dt, 0.0) + jnp.log1p(jnp.exp(-jnp.abs(x_dt)))
    a_head = -jnp.exp(alog_ref[...])
    cum = _cumsum_rows(dt * a_head, causal)
    cum_t = _transpose_f32(cum)
    cum_last = cum[tc - 1:tc, :]
    pair_w = 2 * SSD_HEAD_DIM
    assert pair_w == LANES and hg % 2 == 0
    first_lanes = lax.broadcasted_iota(I32, (tc, pair_w), 1) < SSD_HEAD_DIM
    first_rows = lax.broadcasted_iota(I32, (pair_w, SSD_D_STATE), 0) < SSD_HEAD_DIM
    pick = lambda a, b: jnp.where(first_lanes, a, b)
    for g in range(SSD_GROUPS):
        bm = xs[:, d_inner + g * SSD_D_STATE:d_inner + (g + 1) * SSD_D_STATE].astype(BF16)
        cm = xs[:, d_inner + gn + g * SSD_D_STATE:d_inner + gn + (g + 1) * SSD_D_STATE].astype(BF16)
        cb = lax.dot_general(cm, bm, NT_DIMS, preferred_element_type=F32)
        for pp in range(hg // 2):
            p = g * (hg // 2) + pp
            ha, hb = 2 * p, 2 * p + 1
            cols = slice(p * pair_w, (p + 1) * pair_w)
            col_a, col_b = cum[:, ha:ha + 1], cum[:, hb:hb + 1]
            xh = xs[:, cols]
            xdt = xh * pick(dt[:, ha:ha + 1], dt[:, hb:hb + 1])
            xdt_b = xdt.astype(BF16)
            ys = []
            for h, col in ((ha, col_a), (hb, col_b)):
                decay = jnp.exp(jnp.where(causal, col - cum_t[h:h + 1, :], -jnp.inf))
                ys.append(jnp.dot((cb * decay).astype(BF16), xdt_b, preferred_element_type=F32))
            h0 = hs_ref[p]
            inter = lax.dot_general(cm, h0.astype(BF16), NT_DIMS, preferred_element_type=F32)
            y = pick(ys[0], ys[1]) + pick(jnp.exp(col_a), jnp.exp(col_b)) * inter
            y_scr[:, cols] = y + dskip_ref[:, cols] * xh
            last_a, last_b = cum_last[:, ha:ha + 1], cum_last[:, hb:hb + 1]
            w = (pick(jnp.exp(last_a - col_a), jnp.exp(last_b - col_b)) * xdt).astype(BF16)
            keep = jnp.where(first_rows, jnp.exp(last_a), jnp.exp(last_b))
            hs_ref[p] = keep * h0 + lax.dot_general(w, bm, TN_DIMS, preferred_element_type=F32)
    yz = y_scr[...] * _silu(z_ref[...])
    gw = d_inner // SSD_GROUPS
    for g in range(SSD_GROUPS):
        cols = slice(g * gw, (g + 1) * gw)
        seg = yz[:, cols]
        ms = jnp.mean(seg * seg, axis=-1, keepdims=True)
        y_ref[:, cols] = (seg * lax.rsqrt(ms + NORM_EPS) * ng_ref[:, cols]).astype(y_ref.dtype)


def ssd_scan(xbc, z, dt, conv0, h0, conv_w, conv_b, dt_bias, a_log, d_skip_cols, norm_g, *, tc):
    bsz, length, cc = xbc.shape
    d_inner = z.shape[2]
    n_heads = dt.shape[2]
    assert length % tc == 0
    state_shape = h0.shape
    h0 = h0.reshape(bsz, n_heads // 2, 2 * state_shape[2], state_shape[3])
    seq = lambda w: pl.BlockSpec((None, tc, w), lambda b, c: (b, c, 0))
    per_b3 = lambda s: pl.BlockSpec((None,) + s, lambda b, c: (b,) + (0,) * len(s))
    row = lambda w: _resident((1, w), lambda b, c: (0, 0))
    y, h_new = pl.pallas_call(
        functools.partial(_ssd_kernel, tc=tc, d_inner=d_inner, n_heads=n_heads),
        grid=(bsz, length // tc),
        in_specs=[seq(cc), seq(d_inner), seq(n_heads), per_b3(conv0.shape[1:]), per_b3(h0.shape[1:]),
                  _resident(conv_w.shape, lambda b, c: (0, 0)), row(cc), row(n_heads), row(n_heads),
                  row(d_inner), row(d_inner)],
        out_specs=[seq(d_inner), per_b3(h0.shape[1:])],
        out_shape=[jax.ShapeDtypeStruct((bsz, length, d_inner), BF16), jax.ShapeDtypeStruct(h0.shape, F32)],
        scratch_shapes=[pltpu.VMEM((CARRY_ROWS, cc), F32), pltpu.VMEM((tc, d_inner), F32)],
        compiler_params=_params("parallel", "arbitrary"),
        name="ssd_scan",
    )(xbc, z, dt, conv0, h0, conv_w, conv_b, dt_bias, a_log, d_skip_cols, norm_g)
    return y, h_new.reshape(state_shape)


def _mlstm_kernel(xc_ref, v_ref, o_ref, gates_ref, conv0_ref, c0_ref, n0_ref, m0_ref, cw_ref, cb_ref, wq_ref,
                  wk_ref, gb_ref, ng_ref, h_ref, cs_ref, ns_ref, ms_ref, carry_scr, *, tc, d_inner):
    c = pl.program_id(1)

    @pl.when(c == 0)
    def _():
        cs_ref[...] = c0_ref[...]
        ns_ref[...] = n0_ref[...]
        ms_ref[...] = m0_ref[...]
        _conv_carry_init(conv0_ref, cw_ref, carry_scr, MLSTM_CONV)

    xa = _silu(_causal_conv_chunk(xc_ref, carry_scr, cw_ref, cb_ref, tc, MLSTM_CONV)).astype(BF16)
    blk = wq_ref.shape[1]
    dh = d_inner // MLSTM_HEADS
    q = jnp.concatenate([jnp.dot(xa[:, j * blk:(j + 1) * blk], wq_ref[j], preferred_element_type=F32)
                         for j in range(d_inner // blk)], axis=1)
    k = jnp.concatenate([jnp.dot(xa[:, j * blk:(j + 1) * blk], wk_ref[j], preferred_element_type=F32)
                         for j in range(d_inner // blk)], axis=1) * dh ** -0.5
    causal = _tri(tc)
    gates = gates_ref[...] + gb_ref[...]
    log_f = jnp.minimum(gates, 0.0) - jnp.log1p(jnp.exp(-jnp.abs(gates)))
    f_cum = _cumsum_rows(log_f, causal)
    f_cum_t = _transpose_f32(f_cum)
    gates_t = _transpose_f32(gates)
    for hh in range(MLSTM_HEADS):
        cols = slice(hh * dh, (hh + 1) * dh)
        fh = MLSTM_HEADS + hh
        qh = q[:, cols].astype(BF16)
        kh = k[:, cols]
        vh = v_ref[:, cols]
        fc = f_cum[:, fh:fh + 1]
        i_col = gates[:, hh:hh + 1]
        m0 = ms_ref[0:1, hh:hh + 1]
        d_log = jnp.where(causal, fc - f_cum_t[fh:fh + 1, :] + gates_t[hh:hh + 1, :], -jnp.inf)
        inter = fc + m0
        m = jnp.maximum(jnp.max(d_log, axis=-1, keepdims=True), inter)
        s = lax.dot_general(qh, kh.astype(BF16), NT_DIMS, preferred_element_type=F32) * jnp.exp(d_log - m)
        w_inter = jnp.exp(inter - m)
        c0 = cs_ref[hh]
        n0 = ns_ref[hh:hh + 1, :]
        num = (jnp.dot(s.astype(BF16), vh, preferred_element_type=F32)
               + w_inter * jnp.dot(qh, c0.astype(BF16), preferred_element_type=F32))
        den = (jnp.sum(s, axis=-1, keepdims=True)
               + w_inter * jnp.sum(q[:, cols] * n0, axis=-1, keepdims=True))
        h = num / jnp.maximum(jnp.abs(den), jnp.exp(-m))
        m_end = m[tc - 1:tc, :]
        f_last = fc[tc - 1:tc, :]
        w_end = jnp.exp(f_last - fc + i_col - m_end)
        decay = jnp.exp(f_last + m0 - m_end)
        wk = w_end * kh
        cs_ref[hh] = decay * c0 + lax.dot_general(wk.astype(BF16), vh, TN_DIMS, preferred_element_type=F32)
        ns_ref[hh:hh + 1, :] = decay * n0 + jnp.sum(wk, axis=0, keepdims=True)
        ms_ref[0:1, hh:hh + 1] = m_end
        h = jax.nn.sigmoid(o_ref[:, cols]) * h
        h = h - jnp.mean(h, axis=-1, keepdims=True)
        h = h * lax.rsqrt(jnp.mean(h * h, axis=-1, keepdims=True) + NORM_EPS)
        h_ref[:, cols] = (h * ng_ref[:, cols]).astype(h_ref.dtype)


def mlstm_scan(xc, v, o, gates, conv0, c0, n0, m0, conv_w, conv_b, wq_bd, wk_bd, gate_b, norm_g, *, tc):
    bsz, length, d_inner = xc.shape
    assert length % tc == 0
    seq = lambda w: pl.BlockSpec((None, tc, w), lambda b, c: (b, c, 0))
    per_b = lambda s: pl.BlockSpec((None,) + s, lambda b, c: (b,) + (0,) * len(s))
    row = lambda w: _resident((1, w), lambda b, c: (0, 0))
    return pl.pallas_call(
        functools.partial(_mlstm_kernel, tc=tc, d_inner=d_inner),
        grid=(bsz, length // tc),
        in_specs=[seq(d_inner), seq(d_inner), seq(d_inner), seq(gates.shape[2]), per_b(conv0.shape[1:]),
                  per_b(c0.shape[1:]), per_b(n0.shape[1:]), per_b(m0.shape[1:]),
                  _resident(conv_w.shape, lambda b, c: (0, 0)), row(d_inner),
                  _resident(wq_bd.shape, lambda b, c: (0, 0, 0)), _resident(wk_bd.shape, lambda b, c: (0, 0, 0)),
                  row(gates.shape[2]), row(d_inner)],
        out_specs=[seq(d_inner), per_b(c0.shape[1:]), per_b(n0.shape[1:]), per_b(m0.shape[1:])],
        out_shape=[jax.ShapeDtypeStruct((bsz, length, d_inner), BF16), jax.ShapeDtypeStruct(c0.shape, F32),
                   jax.ShapeDtypeStruct(n0.shape, F32), jax.ShapeDtypeStruct(m0.shape, F32)],
        scratch_shapes=[pltpu.VMEM((CARRY_ROWS, d_inner), F32)],
        compiler_params=_params("parallel", "arbitrary"),
        name="mlstm_scan",
    )(xc, v, o, gates, conv0, c0, n0, m0, conv_w, conv_b, wq_bd, wk_bd, gate_b, norm_g)


def _block_diag(w_blk, blk):
    n, c, d = w_blk.shape
    per = blk // c
    eye = jnp.eye(per, dtype=w_blk.dtype)
    tiles = w_blk.reshape(n // per, per, c, d)
    return jnp.einsum("jpcd,pq->jpcqd", tiles, eye).reshape(n // per, per * c, per * d)


def _mixer_a(xp, xs, shapes, cache_k, cache_v, cache_ki, w_in, t5_table):
    (bp, lp_), (bs, ls) = shapes
    hd, kvd, idd = A_HEADS * A_HEAD_DIM, A_KV_HEADS * A_HEAD_DIM, IDX_HEADS * IDX_DIM
    w_pad, cols = _pad_cols(w_in, (hd, kvd, kvd, idd, IDX_DIM, IDX_HEADS))
    log2e = math.log2(math.e)
    outs = [(cols[0], BF16, A_HEAD_DIM ** -0.5 * log2e), (cols[1], F32, None, A_KV_HEADS), (cols[1], BF16),
            (cols[2], F32, None, A_KV_HEADS), (cols[2], BF16), (cols[3], BF16), (cols[4], F32), (cols[4], BF16),
            (cols[5], F32)]
    w_pad = w_pad.astype(BF16)
    bias = _t5_bias_tiles(t5_table) * log2e
    on_lanes = lambda a: jnp.swapaxes(a, 1, 2)

    def with_ones(vv):
        b_, l_, _ = vv.shape
        v4 = vv.reshape(b_, l_, A_KV_HEADS, A_HEAD_DIM)
        return jnp.concatenate([v4, jnp.ones_like(v4)], axis=-1).reshape(b_, l_, 2 * kvd)

    q, k, kb, v, vb, qi, ki, kib, wi = project(xp, w_pad, outs, name="project_a")
    r3 = lambda a, b_, l: a.reshape(b_, l, a.shape[-1])
    att_p = dsa_attention(r3(q, bp, lp_), on_lanes(r3(qi, bp, lp_)), on_lanes(r3(wi, bp, lp_)),
                          on_lanes(r3(kb, bp, lp_)), with_ones(r3(vb, bp, lp_)), r3(kib, bp, lp_), bias, tq=KEY_TILE,
                          q_off=0,
                          l_true=lp_,
                          n_sel=min(IDX_TOPK_MAX, lp_ // 4))
    outs_p = (k.reshape(bp, lp_, A_KV_HEADS, A_HEAD_DIM), v.reshape(bp, lp_, A_KV_HEADS, A_HEAD_DIM),
              ki.reshape(bp, lp_, IDX_DIM))

    q, k, kb, v, vb, qi, ki, kib, wi = project(xs, w_pad, outs, name="project_a")
    past = cache_k.shape[1]
    total = past + ls
    lpad = -(-total // (2 * KEY_TILE)) * (2 * KEY_TILE)

    def with_past(cache, new):
        parts = [cache.reshape(bs, past, -1).astype(BF16), r3(new, bs, ls)]
        if lpad != total:
            parts.append(jnp.zeros((bs, lpad - total, new.shape[-1]), BF16))
        return jnp.concatenate(parts, axis=1)

    att_s = dsa_attention(r3(q, bs, ls), on_lanes(r3(qi, bs, ls)), on_lanes(r3(wi, bs, ls)),
                          on_lanes(with_past(cache_k, kb)), with_ones(with_past(cache_v, vb)),
                          with_past(cache_ki, kib), bias,
                          tq=ls, q_off=past,
                          l_true=total, n_sel=min(IDX_TOPK_MAX, total // 4))
    outs_s = (k.reshape(bs, ls, A_KV_HEADS, A_HEAD_DIM), v.reshape(bs, ls, A_KV_HEADS, A_HEAD_DIM),
              ki.reshape(bs, ls, IDX_DIM))
    return att_p.reshape(bp * lp_, hd), att_s.reshape(bs * ls, hd), outs_p, outs_s


def _mixer_b(xp, xs, shapes, cache_k, cache_v, w_in, rel_table):
    (bp, lp_), (bs, ls) = shapes
    hd = BAND_HEADS * BAND_HEAD_DIM
    w_pad, cols = _pad_cols(w_in, (hd, hd, hd))
    attn_outs = [(cols[0], BF16, BAND_HEAD_DIM ** -0.5), (cols[1], BF16), (cols[2], BF16)]
    kv_outs = [(cols[1], F32, None, BAND_HEADS), (cols[2], F32, None, BAND_HEADS)]
    w_pad = w_pad.astype(BF16)

    q, kb, vb = project(xp, w_pad, attn_outs, name="project_b")
    r3 = lambda a, b_, l: a.reshape(b_, l, hd)
    att_p = band_attention_prompt(r3(q, bp, lp_), r3(kb, bp, lp_), r3(vb, bp, lp_),
                                  _band_bias(rel_table, CHUNK, BAND_ROWS), grp=BAND_ROWS)
    keep = min(BAND_WINDOW, lp_)
    x_keep = xp.reshape(bp, lp_, -1)[:, lp_ - keep:].reshape(bp * keep, -1)
    k, v = project(x_keep, w_pad, kv_outs, name="project_b_keep")
    heads = lambda a, b_, l: a.reshape(b_, l, BAND_HEADS, BAND_HEAD_DIM)
    outs_p = (heads(k, bp, keep), heads(v, bp, keep))

    q, kb, vb = project(xs, w_pad, attn_outs, name="project_b")
    k, v = project(xs, w_pad, kv_outs, name="project_b_keep")
    past = cache_k.shape[1]
    att_s = band_attention_sample(r3(q, bs, ls), cache_k.reshape(bs, past, hd).astype(BF16), r3(kb, bs, ls),
                                  cache_v.reshape(bs, past, hd).astype(BF16), r3(vb, bs, ls),
                                  _band_bias(rel_table, ls, ls))
    outs_s = (jnp.concatenate([cache_k, heads(k, bs, ls)], axis=1)[:, ls:],
              jnp.concatenate([cache_v, heads(v, bs, ls)], axis=1)[:, ls:])
    return att_p.reshape(bp * lp_, hd), att_s.reshape(bs * ls, hd), outs_p, outs_s


def _mixer_c(xp, xs, shapes, ssm0, conv0, w_in, conv_w, conv_b, dt_bias, a_log, d_skip, norm_g):
    n_heads = a_log.shape[0]
    d_inner = n_heads * SSD_HEAD_DIM
    conv_dim = conv_w.shape[1]
    w_pad, cols = _pad_cols(w_in, (d_inner, conv_dim, n_heads))
    outs = [(cols[0], F32), (cols[1], F32), (cols[2], F32)]
    w_pad = w_pad.astype(BF16)
    row = lambda a: a.reshape(1, -1)
    d_skip_cols = jnp.repeat(d_skip, SSD_HEAD_DIM).reshape(1, d_inner)
    res = []
    for x, (b_, l), h0, c0 in ((xp, shapes[0], None, None), (xs, shapes[1], ssm0, conv0)):
        if h0 is None:
            h0 = jnp.zeros((b_, n_heads, SSD_HEAD_DIM, SSD_D_STATE), F32)
            c0 = jnp.zeros((b_, SSD_CONV - 1, conv_dim), F32)
        z, xbc, dt = project(x, w_pad, outs, name="project_c")
        xbc3 = xbc.reshape(b_, l, conv_dim)
        y, h_new = ssd_scan(xbc3, z.reshape(b_, l, d_inner), dt.reshape(b_, l, n_heads), c0, h0, conv_w,
                            row(conv_b), row(dt_bias), row(a_log), d_skip_cols, row(norm_g), tc=min(SSD_ROWS, l))
        conv_new = jnp.concatenate([c0, xbc3], axis=1)[:, l:]
        res.append((y.reshape(b_ * l, d_inner), (h_new, conv_new)))
    return res[0][0], res[1][0], res[0][1], res[1][1]


def _mixer_d(xp, xs, shapes, c0s, n0s, m0s, conv0s, w_in, conv_w, conv_b, wq_blk, wk_blk, gate_b, norm_g):
    d_inner = conv_w.shape[1]
    dh = d_inner // MLSTM_HEADS
    w_pad, cols = _pad_cols(w_in, (d_inner, d_inner, d_inner, 2 * MLSTM_HEADS))
    gcol = (cols[3][0], cols[3][1], cols[3][1])
    outs = [(cols[0], F32), (cols[1], BF16), (cols[2], F32), (gcol, F32)]
    w_pad = w_pad.astype(BF16)
    blk = 2 * LANES
    wq_bd = _block_diag(wq_blk, blk).astype(BF16)
    wk_bd = _block_diag(wk_blk, blk).astype(BF16)
    gate_b_pad = jnp.zeros((1, gcol[1]), F32).at[0, :2 * MLSTM_HEADS].set(gate_b)
    row = lambda a: a.reshape(1, -1)
    res = []
    for x, (b_, l), st in ((xp, shapes[0], None), (xs, shapes[1], (c0s, n0s, m0s, conv0s))):
        if st is None:
            st = (jnp.zeros((b_, MLSTM_HEADS, dh, dh), F32), jnp.zeros((b_, MLSTM_HEADS, dh), F32),
                  jnp.zeros((b_, MLSTM_HEADS), F32), jnp.zeros((b_, MLSTM_CONV - 1, d_inner), F32))
        c0, n0, m0, conv0 = st
        xc, v, o, gates = project(x, w_pad, outs, name="project_d")
        r3 = lambda a: a.reshape(b_, l, a.shape[-1])
        h, c_new, n_new, m_new = mlstm_scan(r3(xc), r3(v), r3(o), r3(gates), conv0, c0, n0,
                                            m0.reshape(b_, 1, MLSTM_HEADS), conv_w, row(conv_b), wq_bd, wk_bd,
                                            gate_b_pad, row(norm_g), tc=min(MLSTM_ROWS, l))
        conv_new = jnp.concatenate([conv0, r3(xc)], axis=1)[:, l:]
        res.append((h.reshape(b_ * l, d_inner), (c_new, n_new, m_new.reshape(b_, MLSTM_HEADS), conv_new)))
    return res[0][0], res[1][0], res[0][1], res[1][1]


def kernel(x_prompt, x_sample, cache_a_k, cache_a_v, cache_a_kidx, cache_b_k, cache_b_v, state_c_ssm, state_c_conv, state_d_c, state_d_n, state_d_m, state_d_conv, a_w_in, a_w_out, t5_table, b_w_in, b_w_out, b_rel_table, c_w_in, c_conv_w, c_conv_b, c_dt_bias, c_a_log, c_d_skip, c_norm_g, c_w_out, d_w_in, d_conv_w, d_conv_b, d_wq_blk, d_wk_blk, d_gate_b, d_norm_g, d_w_out, ffn1_wg, ffn1_wu, ffn1_wd, ffn2_wg, ffn2_wu, ffn2_wd, ln_g, ln_b):
    bp, lp_, d = x_prompt.shape
    bs, ls, _ = x_sample.shape
    depth = ffn1_wg.shape[0]
    alpha = (2.0 * depth) ** 0.25
    shapes = ((bp, lp_), (bs, ls))
    xp = x_prompt.reshape(bp * lp_, d)
    xs = x_sample.reshape(bs * ls, d)
    ffn_w = [[w.astype(BF16) for w in ws] for ws in ((ffn1_wg, ffn1_wu, ffn1_wd), (ffn2_wg, ffn2_wu, ffn2_wd))]
    w_out = [w.astype(BF16) for w in (a_w_out, b_w_out, c_w_out, d_w_out)]
    row = lambda a: a.reshape(1, -1)
    extra = {}
    for i in range(depth):
        g, b = ln_g[i], ln_b[i]
        xp = ffn_postnorm(xp, *ffn_w[0], i, row(g[0]), row(b[0]), alpha)
        xs = ffn_postnorm(xs, *ffn_w[0], i, row(g[0]), row(b[0]), alpha)
        kind = i % 4
        if kind == 0:
            mp, ms, op, os_ = _mixer_a(xp, xs, shapes, cache_a_k, cache_a_v, cache_a_kidx, a_w_in, t5_table)
        elif kind == 1:
            mp, ms, op, os_ = _mixer_b(xp, xs, shapes, cache_b_k, cache_b_v, b_w_in, b_rel_table)
        elif kind == 2:
            mp, ms, op, os_ = _mixer_c(xp, xs, shapes, state_c_ssm, state_c_conv, c_w_in, c_conv_w, c_conv_b,
                                       c_dt_bias, c_a_log, c_d_skip, c_norm_g)
        else:
            mp, ms, op, os_ = _mixer_d(xp, xs, shapes, state_d_c, state_d_n, state_d_m, state_d_conv, d_w_in,
                                       d_conv_w, d_conv_b, d_wq_blk, d_wk_blk, d_gate_b, d_norm_g)
        extra[kind] = (op, os_)
        xp = mixout_ffn(xp, mp, w_out[kind], row(g[1]), row(b[1]), *ffn_w[1], i, row(g[2]), row(b[2]), alpha)
        xs = mixout_ffn(xs, ms, w_out[kind], row(g[1]), row(b[1]), *ffn_w[1], i, row(g[2]), row(b[2]), alpha)
    prompt_side = tuple(t for kind in range(4) for t in extra[kind][0])
    sample_side = tuple(t for kind in range(4) for t in extra[kind][1])
    return (xp.reshape(bp, lp_, d), xs.reshape(bs, ls, d)) + prompt_side + sample_side
```

```python
import functools
import math

import numpy as np
import jax
import jax.numpy as jnp
from jax import lax
from jax.experimental import pallas as pl
from jax.experimental.pallas import tpu as pltpu

F32 = jnp.float32
BF16 = jnp.bfloat16
I32 = jnp.int32

CHUNK = 64
NORM_EPS = 1e-5
A_HEADS, A_KV_HEADS, A_HEAD_DIM = 8, 2, 128
IDX_HEADS, IDX_DIM, IDX_TOPK_MAX = 8, 64, 256
T5_BUCKETS, T5_MAX_DIST = 32, 128
BAND_HEADS, BAND_HEAD_DIM, BAND_LEFT_CHUNKS, BAND_MAX_REL = 16, 64, 8, 128
BAND_WINDOW = BAND_LEFT_CHUNKS * CHUNK
SSD_HEAD_DIM, SSD_GROUPS, SSD_D_STATE, SSD_CONV = 64, 8, 128, 4
MLSTM_HEADS, MLSTM_CONV, MLSTM_QK_BLOCK = 4, 4, 4

LANES = 128
NEG_BIG = -1e30
VMEM_LIMIT = 56 * 1024 * 1024

NT_DIMS = (((1,), (1,)), ((), ()))
TN_DIMS = (((0,), (0,)), ((), ()))


def _params(*sem):
    return pltpu.CompilerParams(dimension_semantics=sem, vmem_limit_bytes=VMEM_LIMIT)


def _resident(shape, index_map):
    return pl.BlockSpec(shape, index_map, pipeline_mode=pl.Buffered(1))


def _layer_norm(y, g, b):
    mu = jnp.mean(y, axis=-1, keepdims=True)
    yc = y - mu
    var = jnp.mean(yc * yc, axis=-1, keepdims=True)
    return yc * lax.rsqrt(var + NORM_EPS) * g + b


def _silu(x):
    return x * jax.nn.sigmoid(x)


def _row_tile(n, want):
    t = min(n, want)
    assert n % t == 0, (n, t)
    return t


def _ffn_block(x, wg_ref, wu_ref, wd_ref, g_ref, b_ref, alpha, f_cuts):
    xb = x.astype(BF16)
    acc = jnp.zeros(x.shape, F32)
    for lo, hi in zip(f_cuts[:-1], f_cuts[1:]):
        sl = slice(lo, hi)
        gate = jnp.dot(xb, wg_ref[:, sl], preferred_element_type=F32)
        up = jnp.dot(xb, wu_ref[:, sl], preferred_element_type=F32)
        h = (_silu(gate) * up).astype(BF16)
        acc = acc + jnp.dot(h, wd_ref[sl, :], preferred_element_type=F32)
    return _layer_norm(alpha * x + 0.5 * acc, g_ref[...], b_ref[...])


def _ffn_kernel(x_ref, wg_ref, wu_ref, wd_ref, g_ref, b_ref, o_ref, *, alpha, f_cuts):
    o_ref[...] = _ffn_block(x_ref[...], wg_ref, wu_ref, wd_ref, g_ref, b_ref, alpha, f_cuts)


def _mixout_ffn_kernel(x_ref, m_ref, wo_ref, g1_ref, b1_ref, wg_ref, wu_ref, wd_ref, g2_ref, b2_ref, o_ref, *,
                       alpha, f_cuts):
    sub = jnp.dot(m_ref[...], wo_ref[...], preferred_element_type=F32)
    x1 = _layer_norm(alpha * x_ref[...] + sub, g1_ref[...], b1_ref[...])
    o_ref[...] = _ffn_block(x1, wg_ref, wu_ref, wd_ref, g2_ref, b2_ref, alpha, f_cuts)


MXU_WIDTH = 256


def _ffn_cuts(d_ff, n_chunks):
    tiles = -(-d_ff // MXU_WIDTH)
    cuts = [min(d_ff, MXU_WIDTH * (-(-tiles * c // n_chunks))) for c in range(n_chunks + 1)]
    return tuple(cuts)


def ffn_postnorm(x, wg, wu, wd, layer, g, b, alpha, tm=512, n_chunks=2):
    n, d = x.shape
    d_ff = wg.shape[2]
    tm = _row_tile(n, tm)
    return pl.pallas_call(
        functools.partial(_ffn_kernel, alpha=alpha, f_cuts=_ffn_cuts(d_ff, n_chunks)),
        grid=(n // tm,),
        in_specs=[
            pl.BlockSpec((tm, d), lambda i: (i, 0)),
            _resident((None, d, d_ff), lambda i: (layer, 0, 0)),
            _resident((None, d, d_ff), lambda i: (layer, 0, 0)),
            _resident((None, d_ff, d), lambda i: (layer, 0, 0)),
            _resident((1, d), lambda i: (0, 0)),
            _resident((1, d), lambda i: (0, 0)),
        ],
        out_specs=pl.BlockSpec((tm, d), lambda i: (i, 0)),
        out_shape=jax.ShapeDtypeStruct((n, d), F32),
        compiler_params=_params("parallel"),
        name="ffn_postnorm",
    )(x, wg, wu, wd, g, b)


def mixout_ffn(x, m, w_out, g1, b1, wg, wu, wd, layer, g2, b2, alpha, tm=512, n_chunks=2):
    n, d = x.shape
    k = m.shape[1]
    d_ff = wg.shape[2]
    tm = _row_tile(n, tm)
    vec = lambda: _resident((1, d), lambda i: (0, 0))
    return pl.pallas_call(
        functools.partial(_mixout_ffn_kernel, alpha=alpha, f_cuts=_ffn_cuts(d_ff, n_chunks)),
        grid=(n // tm,),
        in_specs=[
            pl.BlockSpec((tm, d), lambda i: (i, 0)),
            pl.BlockSpec((tm, k), lambda i: (i, 0)),
            _resident((k, d), lambda i: (0, 0)), vec(), vec(),
            _resident((None, d, d_ff), lambda i: (layer, 0, 0)),
            _resident((None, d, d_ff), lambda i: (layer, 0, 0)),
            _resident((None, d_ff, d), lambda i: (layer, 0, 0)), vec(), vec(),
        ],
        out_specs=pl.BlockSpec((tm, d), lambda i: (i, 0)),
        out_shape=jax.ShapeDtypeStruct((n, d), F32),
        compiler_params=_params("parallel"),
        name="mixout_ffn",
    )(x, m, w_out, g1, b1, wg, wu, wd, g2, b2)


def _proj_kernel(x_ref, w_ref, *o_refs, cols, scales):
    xb = x_ref[...].astype(BF16)
    done = {}
    for (off, pad_w, true_w), scale, o_ref in zip(cols, scales, o_refs):
        if (off, pad_w) not in done:
            done[(off, pad_w)] = jnp.dot(xb, w_ref[:, off:off + pad_w], preferred_element_type=F32)
        y = done[(off, pad_w)][:, :true_w]
        y = (y if scale is None else y * scale).astype(o_ref.dtype)
        if len(o_ref.shape) == 3:
            head_w = o_ref.shape[2]
            for p in range(o_ref.shape[1]):
                o_ref[:, p, :] = y[:, p * head_w:(p + 1) * head_w]
        else:
            o_ref[...] = y


def _pad_cols(w, widths):
    pieces, offs, off, src = [], [], 0, 0
    for wd in widths:
        pad_w = -(-wd // LANES) * LANES
        pieces.append(w[:, src:src + wd])
        if pad_w != wd:
            pieces.append(jnp.zeros((w.shape[0], pad_w - wd), w.dtype))
        offs.append((off, pad_w, wd))
        off += pad_w
        src += wd
    return jnp.concatenate(pieces, axis=1), offs


def project(x, w_pad, outs, tm=512, name="project"):
    n, d = x.shape
    tm = _row_tile(n, tm)
    cols = tuple(o[0] for o in outs)
    scales = tuple(o[2] if len(o) > 2 else None for o in outs)

    def tail(o):
        heads = o[3] if len(o) > 3 else None
        return (o[0][2],) if heads is None else (heads, o[0][2] // heads)

    return pl.pallas_call(
        functools.partial(_proj_kernel, cols=cols, scales=scales),
        grid=(n // tm,),
        in_specs=[pl.BlockSpec((tm, d), lambda i: (i, 0)),
                  _resident(w_pad.shape, lambda i: (0, 0))],
        out_specs=[pl.BlockSpec((tm,) + tail(o), lambda i, nd=len(tail(o)): (i,) + (0,) * nd) for o in outs],
        out_shape=[jax.ShapeDtypeStruct((n,) + tail(o), o[1]) for o in outs],
        compiler_params=_params("parallel"),
        name=name,
    )(x, w_pad)


def _t5_bucket_np(rel):
    half = T5_BUCKETS // 2
    max_exact = half // 2
    n = np.abs(rel)
    nf = np.maximum(n, 1).astype(np.float32)
    large = max_exact + (np.log(nf / np.float32(max_exact)) / np.float32(math.log(T5_MAX_DIST / max_exact))
                         * np.float32(half - max_exact)).astype(np.int32)
    large = np.minimum(large, half - 1)
    return np.where(rel > 0, half, 0) + np.where(n < max_exact, n, large)


KEY_TILE = 2 * LANES


def _toeplitz(w, t_rows, s_cols):
    n = t_rows + s_cols
    lead = w.shape[:-1]
    wp = jnp.concatenate([w, jnp.zeros(lead + (1,), w.dtype)], axis=-1)
    flat = jnp.broadcast_to(wp[..., None, :], lead + (t_rows, n)).reshape(lead + (t_rows * n,))
    skew = flat[..., :t_rows * (n - 1)].reshape(lead + (t_rows, n - 1))
    return skew[..., t_rows - 1:t_rows - 1 + s_cols]


def _t5_bias_tiles(t5_table):
    assert T5_MAX_DIST <= KEY_TILE
    rel = np.arange(3 * KEY_TILE - 1) - (KEY_TILE - 1) - KEY_TILE
    far = t5_table[int(_t5_bucket_np(np.array(-T5_MAX_DIST)))]
    by_rel = (t5_table[_t5_bucket_np(rel)] - far).T
    strip = _toeplitz(by_rel, KEY_TILE, 2 * KEY_TILE)
    return jnp.stack([strip[:, :, KEY_TILE:], strip[:, :, :KEY_TILE]])


def _dsa_kernel(q_ref, qit_ref, wit_ref, kt_ref, v_ref, ki_ref, bias_ref, o_ref,
                keyt_scr, mask_scr, lg_scr, m_scr, acc_scr, *, tq, q_off, l_true, n_sel, lp):
    j = pl.program_id(1)
    q0 = q_off + j * tq
    kend = jnp.minimum(((q0 + tq - 1) // CHUNK + 1) * CHUNK, l_true)
    n_tiles = (kend + KEY_TILE - 1) // KEY_TILE
    n_far = jnp.maximum(q0 // KEY_TILE - 1, 0)
    int_min = jnp.int32(-2 ** 31)
    wide = 2 * KEY_TILE

    def tile_off(kt):
        return pl.multiple_of(kt * KEY_TILE, KEY_TILE)

    def slabs(lo, hi, fn):
        def pair(i, carry):
            fn(tile_off(lo + 2 * i), wide)
            return carry
        lax.fori_loop(0, (hi - lo) // 2, pair, 0)

        @pl.when((hi - lo) % 2 == 1)
        def _():
            fn(tile_off(hi - 1), KEY_TILE)

    wi = wit_ref[...] * (IDX_HEADS ** -0.5) * (IDX_DIM ** -0.5)

    def positions(off, n_keys):
        kpos = off + lax.broadcasted_iota(I32, (n_keys, tq), 0)
        q_chunk = (q0 + lax.broadcasted_iota(I32, (n_keys, tq), 1)) // CHUNK
        return kpos, ((kpos // CHUNK) <= q_chunk) & (kpos < l_true)

    def to_key(x):
        bits = lax.bitcast_convert_type(x, I32)
        return jnp.where(bits < 0, (bits ^ jnp.int32(0x7FFFFFFF)) + 1, bits)

    last_off = tile_off(n_tiles - 1)

    def score_slab(off, width, check_admissible=False):
        ki_t = ki_ref[pl.ds(off, width), :]
        accs = [jnp.zeros((width, tq), F32)] * 2
        for h in range(IDX_HEADS):
            s = jnp.dot(ki_t, qit_ref[h * IDX_DIM:(h + 1) * IDX_DIM, :], preferred_element_type=F32)
            accs[h % 2] = accs[h % 2] + jnp.maximum(s, 0.0) * wi[h:h + 1, :]
        sc = accs[0] + accs[1]
        if check_admissible:
            sc = jnp.where(positions(off, width)[1], sc, -jnp.inf)
        keyt_scr[pl.ds(off, width), :] = to_key(sc)

    slabs(0, n_tiles - 1, score_slab)
    score_slab(last_off, KEY_TILE, check_admissible=True)

    @pl.when(n_tiles % 2 == 1)
    def _():
        keyt_scr[pl.ds(tile_off(n_tiles), KEY_TILE), :] = to_key(jnp.full((KEY_TILE, tq), -jnp.inf, F32))

    sublanes = 8

    def count(pred):
        def body(ct, acc):
            off = pl.multiple_of(ct * wide, wide)
            parts = []
            for r0 in range(0, wide, LANES):
                kpos = off + r0 + lax.broadcasted_iota(I32, (LANES, tq), 0)
                hit = pred(keyt_scr[pl.ds(pl.multiple_of(off + r0, LANES), LANES), :], kpos).astype(I32)
                parts.append(jnp.sum(hit.reshape(LANES // sublanes, sublanes, tq), axis=0))
            return acc + ((parts[0] + parts[1]) + (parts[2] + parts[3]))
        acc = lax.fori_loop(0, (n_tiles + 1) // 2, body, jnp.zeros((sublanes, tq), I32))
        return jnp.sum(acc.astype(F32), axis=0, keepdims=True)

    check_every = 4

    def search_body(state):
        i, thr_u, settled, _ = state

        def one_bit(b, st):
            thr_u, settled = st
            cand_u = thr_u | lax.shift_left(jnp.int32(1), 31 - (i + b))
            cand = cand_u ^ int_min
            cnt = count(lambda key, kpos: key >= cand)
            thr_u = jnp.where((cnt >= n_sel) & (settled == 0), cand_u, thr_u)
            return thr_u, jnp.where(cnt == n_sel, 1, settled)

        thr_u, settled = lax.fori_loop(0, check_every, one_bit, (thr_u, settled))
        return i + check_every, thr_u, settled, jnp.min(settled.astype(F32))

    zeros = jnp.zeros((1, tq), I32)
    _, thr_u, _, _ = lax.while_loop(lambda s: (s[0] < 32) & (s[3] == 0.0), search_body,
                                    (jnp.int32(0), zeros, zeros, jnp.float32(0.0)))
    thr = thr_u ^ int_min
    need = n_sel - count(lambda key, kpos: key > thr)
    n_eq = count(lambda key, kpos: key == thr)
    has_tie = jnp.max(n_eq - need) > 0.0

    nbits = int(lp).bit_length()

    def tie_cut():
        def cut_body(i, cut):
            cand = cut | lax.shift_left(jnp.int32(1), nbits - 1 - i)
            cnt = count(lambda key, kpos: (key == thr) & (kpos < cand))
            return jnp.where(cnt <= need, cand, cut)
        return lax.fori_loop(0, nbits, cut_body, jnp.zeros((1, tq), I32))

    cut = lax.cond(has_tie, tie_cut, lambda: jnp.full((1, tq), 2 ** 30, I32))

    eye = (lax.broadcasted_iota(I32, (tq, tq), 0) == lax.broadcasted_iota(I32, (tq, tq), 1)).astype(BF16)

    def mask_tile(off, tie, check_admissible=False):
        key = keyt_scr[pl.ds(off, KEY_TILE), :]
        kpos, adm = positions(off, KEY_TILE)
        sel = ((key > thr) | ((key == thr) & (kpos < cut))) if tie else (key >= thr)
        if check_admissible:
            sel = sel & adm
        sel_q = lax.dot_general(eye, jnp.where(sel, 1.0, 0.0).astype(BF16), NT_DIMS, preferred_element_type=F32)
        mask_scr[:, pl.ds(off, KEY_TILE)] = jnp.where(sel_q > 0.5, 0.0, NEG_BIG)

    def mask_all(tie):
        def slab(off, width):
            for u in range(width // KEY_TILE):
                mask_tile(pl.multiple_of(off + u * KEY_TILE, KEY_TILE), tie)

        def run():
            slabs(0, n_tiles - 1, slab)
            mask_tile(last_off, tie, check_admissible=True)
        return run

    lax.cond(has_tie, mask_all(True), mask_all(False))

    grp = A_HEADS // A_KV_HEADS
    rows = grp * tq
    head_cols = lambda h: slice(h * A_HEAD_DIM, (h + 1) * A_HEAD_DIM)
    for g in range(A_KV_HEADS):
        qg = jnp.concatenate([q_ref[:, head_cols(g * grp + i)] for i in range(grp)], axis=0)
        m_scr[...] = jnp.full(m_scr.shape, NEG_BIG, F32)

        def qk_slab(off, width, near=False, qg=qg, g=g):
            lg = jnp.dot(qg, kt_ref[head_cols(g), pl.ds(off, width)], preferred_element_type=F32)
            lg = lg.reshape(grp, tq, width) + mask_scr[:, pl.ds(off, width)][None]
            if near:
                back = jnp.clip((q0 - off) // KEY_TILE, 0, 1)
                lg = lg + bias_ref[back, g * grp:(g + 1) * grp, 0:tq, :]
            lg = lg.reshape(rows, width)
            lg_scr[:, pl.ds(off, width)] = lg
            mx = lg[:, 0:LANES]
            for u in range(1, width // LANES):
                mx = jnp.maximum(mx, lg[:, u * LANES:(u + 1) * LANES])
            m_scr[...] = jnp.maximum(m_scr[...], mx)

        def near_body(kt, carry, qk_slab=qk_slab):
            qk_slab(tile_off(kt), KEY_TILE, near=True)
            return carry

        slabs(0, n_far, qk_slab)
        lax.fori_loop(n_far, n_tiles, near_body, 0)
        m_scr[...] = jnp.broadcast_to(jnp.max(m_scr[...], axis=-1, keepdims=True), m_scr.shape)
        acc_scr[...] = jnp.zeros(acc_scr.shape, F32)

        def pv_slab(off, width, g=g):
            lg = lg_scr[:, pl.ds(off, width)]
            m_b = m_scr[...]
            ps = [jnp.exp2(lg[:, u * LANES:(u + 1) * LANES] - m_b) for u in range(width // LANES)]
            p = jnp.concatenate(ps, axis=1).astype(BF16)
            v_ones = v_ref[pl.ds(off, width), 2 * g * A_HEAD_DIM:2 * (g + 1) * A_HEAD_DIM]
            acc_scr[...] += jnp.dot(p, v_ones, preferred_element_type=F32)

        slabs(0, n_tiles, pv_slab)
        out = acc_scr[:, 0:A_HEAD_DIM] / acc_scr[:, A_HEAD_DIM:2 * A_HEAD_DIM]
        for i in range(grp):
            o_ref[:, head_cols(g * grp + i)] = out[i * tq:(i + 1) * tq].astype(o_ref.dtype)


def dsa_attention(q, qi_t, wi_t, k_t, v, ki, bias, *, tq, q_off, l_true, n_sel):
    bsz, t_len, _ = q.shape
    lp = v.shape[1]
    assert lp % (2 * KEY_TILE) == 0 and t_len % tq == 0 and q_off % KEY_TILE == 0
    assert tq == KEY_TILE or (t_len == tq and tq < KEY_TILE)
    grp = A_HEADS // A_KV_HEADS
    qspec = lambda w: pl.BlockSpec((None, tq, w), lambda b, j: (b, j, 0))
    qtspec = lambda h: pl.BlockSpec((None, h, tq), lambda b, j: (b, 0, j))
    whole = lambda a: pl.BlockSpec((None,) + a.shape[1:], lambda b, j: (b, 0, 0))
    return pl.pallas_call(
        functools.partial(_dsa_kernel, tq=tq, q_off=q_off, l_true=l_true, n_sel=n_sel, lp=lp),
        grid=(bsz, t_len // tq),
        in_specs=[qspec(q.shape[2]), qtspec(qi_t.shape[1]), qtspec(wi_t.shape[1]),
                  whole(k_t), whole(v), whole(ki),
                  _resident(bias.shape, lambda b, j: (0, 0, 0, 0))],
        out_specs=qspec(q.shape[2]),
        out_shape=jax.ShapeDtypeStruct(q.shape, BF16),
        scratch_shapes=[pltpu.VMEM((lp, tq), I32), pltpu.VMEM((tq, lp), F32), pltpu.VMEM((grp * tq, lp), F32),
                        pltpu.VMEM((grp * tq, LANES), F32), pltpu.VMEM((grp * tq, 2 * A_HEAD_DIM), F32)],
        compiler_params=_params("parallel", "arbitrary"),
        name="dsa_attention",
    )(q, qi_t, wi_t, k_t, v, ki, bias)


BAND_ROWS = 4 * CHUNK


def _band_bias(rel_table, chunk, grp):
    win = BAND_WINDOW + grp
    t = np.arange(grp)[:, None]
    s = np.arange(win)[None, :]
    k = np.arange(grp + win - 1)
    idx = np.clip(BAND_WINDOW + (grp - 1) - k, -BAND_MAX_REL, BAND_MAX_REL) + BAND_MAX_REL
    bias = _toeplitz(rel_table[:, idx], grp, win)
    lo = (t // chunk) * chunk
    allowed = (s >= lo) & (s < lo + BAND_WINDOW + chunk)
    return jnp.where(allowed[None], bias, NEG_BIG)


def _band_kernel(q_ref, kp_ref, kc_ref, vp_ref, vc_ref, bias_ref, o_ref, kcat, vcat, *, grp, first_has_no_past):
    i = pl.program_id(1)
    prev = kp_ref.shape[0]
    tq = q_ref.shape[0]
    win = prev + grp
    kcat[0:prev, :] = kp_ref[...]
    kcat[prev:prev + tq, :] = kc_ref[...]
    vcat[0:prev, :] = vp_ref[...]
    vcat[prev:prev + tq, :] = vc_ref[...]
    lane = lax.broadcasted_iota(I32, (grp, win), 1)
    pair_w = 2 * BAND_HEAD_DIM
    assert pair_w == LANES
    first_half = lax.broadcasted_iota(I32, (grp, pair_w), 1) < BAND_HEAD_DIM

    def attend(mask_missing_past):
        for r0 in range(0, tq, grp):
            if mask_missing_past:
                kpos_ok = (r0 - prev + lane) >= 0
            for hp in range(BAND_HEADS // 2):
                cols = slice(hp * pair_w, (hp + 1) * pair_w)
                q_pair = q_ref[r0:r0 + grp, cols]
                k_pair = kcat[r0:r0 + win, cols]
                v_pair = vcat[r0:r0 + win, cols]
                outs = []
                for side in range(2):
                    q_one = jnp.where(first_half == (side == 0), q_pair, jnp.zeros_like(q_pair))
                    lg = lax.dot_general(q_one, k_pair, NT_DIMS, preferred_element_type=F32)
                    lg = lg + bias_ref[2 * hp + side]
                    if mask_missing_past:
                        lg = jnp.where(kpos_ok, lg, NEG_BIG)
                    p = jnp.exp(lg - jnp.max(lg, axis=-1, keepdims=True))
                    den = jnp.sum(p, axis=-1, keepdims=True)
                    outs.append(jnp.dot(p.astype(BF16), v_pair, preferred_element_type=F32) / den)
                o_ref[r0:r0 + grp, cols] = jnp.where(first_half, outs[0], outs[1]).astype(o_ref.dtype)

    if first_has_no_past:
        lax.cond(i == 0, lambda: attend(True), lambda: attend(False))
    else:
        attend(False)


def band_attention_prompt(q, k, v, bias, *, tq=512, grp=128):
    bsz, length, w = q.shape
    assert tq == BAND_WINDOW and length % tq == 0
    cur = pl.BlockSpec((None, tq, w), lambda b, i: (b, i, 0))
    prv = pl.BlockSpec((None, tq, w), lambda b, i: (b, jnp.maximum(i - 1, 0), 0))
    return pl.pallas_call(
        functools.partial(_band_kernel, grp=grp, first_has_no_past=True),
        grid=(bsz, length // tq),
        in_specs=[cur, prv, cur, prv, cur, _resident(bias.shape, lambda b, i: (0, 0, 0))],
        out_specs=cur,
        out_shape=jax.ShapeDtypeStruct(q.shape, BF16),
        scratch_shapes=[pltpu.VMEM((2 * tq, w), BF16), pltpu.VMEM((2 * tq, w), BF16)],
        compiler_params=_params("parallel", "arbitrary"),
        name="band_attention",
    )(q, k, k, v, v, bias)


def band_attention_sample(q, k_past, k_new, v_past, v_new, bias):
    bsz, t_len, w = q.shape
    past = k_past.shape[1]
    assert past == BAND_WINDOW
    new = pl.BlockSpec((None, t_len, w), lambda b, i: (b, 0, 0))
    old = pl.BlockSpec((None, past, w), lambda b, i: (b, 0, 0))
    return pl.pallas_call(
        functools.partial(_band_kernel, grp=t_len, first_has_no_past=False),
        grid=(bsz, 1),
        in_specs=[new, old, new, old, new, _resident(bias.shape, lambda b, i: (0, 0, 0))],
        out_specs=new,
        out_shape=jax.ShapeDtypeStruct(q.shape, BF16),
        scratch_shapes=[pltpu.VMEM((past + t_len, w), BF16), pltpu.VMEM((past + t_len, w), BF16)],
        compiler_params=_params("parallel", "arbitrary"),
        name="band_attention_sample",
    )(q, k_past, k_new, v_past, v_new, bias)


CARRY_ROWS = 8
SSD_ROWS = 128
MLSTM_ROWS = 256


def _conv_carry_init(conv0_ref, w_ref, carry_scr, width):
    for k in range(1, width):
        z = None
        for i in range(k, width):
            term = w_ref[width - 1 - i:width - i, :] * conv0_ref[width - 2 - (i - k):width - 1 - (i - k), :]
            z = term if z is None else z + term
        carry_scr[k - 1:k, :] = z


def _causal_conv_chunk(x_ref, carry_scr, w_ref, b_ref, tc, width):
    x = x_ref[...]
    first_row = lax.broadcasted_iota(I32, x.shape, 0) == 0
    z = None
    for k in range(width - 1, 0, -1):
        y = x * w_ref[width - 1 - k:width - k, :]
        z = y if z is None else y + delayed
        delayed = jnp.where(first_row, carry_scr[k - 1:k, :], pltpu.roll(z, 1, 0))
        carry_scr[k - 1:k, :] = z[tc - 1:tc, :]
    return b_ref[...] + x * w_ref[width - 1:width, :] + delayed


def _tri(tc):
    r = lax.broadcasted_iota(I32, (tc, tc), 0)
    c = lax.broadcasted_iota(I32, (tc, tc), 1)
    return r >= c


def _eye(n):
    r = lax.broadcasted_iota(I32, (n, n), 0)
    c = lax.broadcasted_iota(I32, (n, n), 1)
    return (r == c).astype(F32)


def _cumsum_rows(x, causal):
    return jnp.dot(causal.astype(F32), x, precision=lax.Precision.HIGHEST, preferred_element_type=F32)


def _transpose_f32(x):
    return lax.dot_general(_eye(x.shape[1]), x, NT_DIMS, precision=lax.Precision.HIGHEST,
                           preferred_element_type=F32)


def _ssd_kernel(xbc_ref, z_ref, dt_ref, conv0_ref, h0_ref, cw_ref, cb_ref, dtb_ref, alog_ref, dskip_ref,
                ng_ref, y_ref, hs_ref, carry_scr, y_scr, *, tc, d_inner, n_heads):
    c = pl.program_id(1)

    @pl.when(c == 0)
    def _():
        hs_ref[...] = h0_ref[...]
        _conv_carry_init(conv0_ref, cw_ref, carry_scr, SSD_CONV)

    xs = _silu(_causal_conv_chunk(xbc_ref, carry_scr, cw_ref, cb_ref, tc, SSD_CONV))
    gn = SSD_GROUPS * SSD_D_STATE
    hg = n_heads // SSD_GROUPS
    causal = _tri(tc)
    x_dt = dt_ref[...] + dtb_ref[...]
    dt = jnp.maximum(x_dt, 0.0) + jnp.log1p(jnp.exp(-jnp.abs(x_dt)))
    a_head = -jnp.exp(alog_ref[...])
    cum = _cumsum_rows(dt * a_head, causal)
    cum_t = _transpose_f32(cum)
    cum_last = cum[tc - 1:tc, :]
    pair_w = 2 * SSD_HEAD_DIM
    assert pair_w == LANES and hg % 2 == 0
    first_lanes = lax.broadcasted_iota(I32, (tc, pair_w), 1) < SSD_HEAD_DIM
    first_rows = lax.broadcasted_iota(I32, (pair_w, SSD_D_STATE), 0) < SSD_HEAD_DIM
    pick = lambda a, b: jnp.where(first_lanes, a, b)
    for g in range(SSD_GROUPS):
        bm = xs[:, d_inner + g * SSD_D_STATE:d_inner + (g + 1) * SSD_D_STATE].astype(BF16)
        cm = xs[:, d_inner + gn + g * SSD_D_STATE:d_inner + gn + (g + 1) * SSD_D_STATE].astype(BF16)
        cb = lax.dot_general(cm, bm, NT_DIMS, preferred_element_type=F32)
        for pp in range(hg // 2):
            p = g * (hg // 2) + pp
            ha, hb = 2 * p, 2 * p + 1
            cols = slice(p * pair_w, (p + 1) * pair_w)
            col_a, col_b = cum[:, ha:ha + 1], cum[:, hb:hb + 1]
            xh = xs[:, cols]
            xdt = xh * pick(dt[:, ha:ha + 1], dt[:, hb:hb + 1])
            xdt_b = xdt.astype(BF16)
            ys = []
            for h, col in ((ha, col_a), (hb, col_b)):
                decay = jnp.exp(jnp.where(causal, col - cum_t[h:h + 1, :], -jnp.inf))
                ys.append(jnp.dot((cb * decay).astype(BF16), xdt_b, preferred_element_type=F32))
            h0 = hs_ref[p]
            inter = lax.dot_general(cm, h0.astype(BF16), NT_DIMS, preferred_element_type=F32)
            y = pick(ys[0], ys[1]) + pick(jnp.exp(col_a), jnp.exp(col_b)) * inter
            y_scr[:, cols] = y + dskip_ref[:, cols] * xh
            last_a, last_b = cum_last[:, ha:ha + 1], cum_last[:, hb:hb + 1]
            w = (pick(jnp.exp(last_a - col_a), jnp.exp(last_b - col_b)) * xdt).astype(BF16)
            keep = jnp.where(first_rows, jnp.exp(last_a), jnp.exp(last_b))
            hs_ref[p] = keep * h0 + lax.dot_general(w, bm, TN_DIMS, preferred_element_type=F32)
    yz = y_scr[...] * _silu(z_ref[...])
    gw = d_inner // SSD_GROUPS
    for g in range(SSD_GROUPS):
        cols = slice(g * gw, (g + 1) * gw)
        seg = yz[:, cols]
        ms = jnp.mean(seg * seg, axis=-1, keepdims=True)
        y_ref[:, cols] = (seg * lax.rsqrt(ms + NORM_EPS) * ng_ref[:, cols]).astype(y_ref.dtype)


def ssd_scan(xbc, z, dt, conv0, h0, conv_w, conv_b, dt_bias, a_log, d_skip_cols, norm_g, *, tc):
    bsz, length, cc = xbc.shape
    d_inner = z.shape[2]
    n_heads = dt.shape[2]
    assert length % tc == 0
    state_shape = h0.shape
    h0 = h0.reshape(bsz, n_heads // 2, 2 * state_shape[2], state_shape[3])
    seq = lambda w: pl.BlockSpec((None, tc, w), lambda b, c: (b, c, 0))
    per_b3 = lambda s: pl.BlockSpec((None,) + s, lambda b, c: (b,) + (0,) * len(s))
    row = lambda w: _resident((1, w), lambda b, c: (0, 0))
    y, h_new = pl.pallas_call(
        functools.partial(_ssd_kernel, tc=tc, d_inner=d_inner, n_heads=n_heads),
        grid=(bsz, length // tc),
        in_specs=[seq(cc), seq(d_inner), seq(n_heads), per_b3(conv0.shape[1:]), per_b3(h0.shape[1:]),
                  _resident(conv_w.shape, lambda b, c: (0, 0)), row(cc), row(n_heads), row(n_heads),
                  row(d_inner), row(d_inner)],
        out_specs=[seq(d_inner), per_b3(h0.shape[1:])],
        out_shape=[jax.ShapeDtypeStruct((bsz, length, d_inner), BF16), jax.ShapeDtypeStruct(h0.shape, F32)],
        scratch_shapes=[pltpu.VMEM((CARRY_ROWS, cc), F32), pltpu.VMEM((tc, d_inner), F32)],
        compiler_params=_params("parallel", "arbitrary"),
        name="ssd_scan",
    )(xbc, z, dt, conv0, h0, conv_w, conv_b, dt_bias, a_log, d_skip_cols, norm_g)
    return y, h_new.reshape(state_shape)


def _mlstm_kernel(xc_ref, v_ref, o_ref, gates_ref, conv0_ref, c0_ref, n0_ref, m0_ref, cw_ref, cb_ref, wq_ref,
                  wk_ref, gb_ref, ng_ref, h_ref, cs_ref, ns_ref, ms_ref, carry_scr, *, tc, d_inner):
    c = pl.program_id(1)

    @pl.when(c == 0)
    def _():
        cs_ref[...] = c0_ref[...]
        ns_ref[...] = n0_ref[...]
        ms_ref[...] = m0_ref[...]
        _conv_carry_init(conv0_ref, cw_ref, carry_scr, MLSTM_CONV)

    xa = _silu(_causal_conv_chunk(xc_ref, carry_scr, cw_ref, cb_ref, tc, MLSTM_CONV)).astype(BF16)
    blk = wq_ref.shape[1]
    dh = d_inner // MLSTM_HEADS
    q = jnp.concatenate([jnp.dot(xa[:, j * blk:(j + 1) * blk], wq_ref[j], preferred_element_type=F32)
                         for j in range(d_inner // blk)], axis=1)
    k = jnp.concatenate([jnp.dot(xa[:, j * blk:(j + 1) * blk], wk_ref[j], preferred_element_type=F32)
                         for j in range(d_inner // blk)], axis=1) * dh ** -0.5
    causal = _tri(tc)
    gates = gates_ref[...] + gb_ref[...]
    log_f = jnp.minimum(gates, 0.0) - jnp.log1p(jnp.exp(-jnp.abs(gates)))
    f_cum = _cumsum_rows(log_f, causal)
    f_cum_t = _transpose_f32(f_cum)
    gates_t = _transpose_f32(gates)
    for hh in range(MLSTM_HEADS):
        cols = slice(hh * dh, (hh + 1) * dh)
        fh = MLSTM_HEADS + hh
        qh = q[:, cols].astype(BF16)
        kh = k[:, cols]
        vh = v_ref[:, cols]
        fc = f_cum[:, fh:fh + 1]
        i_col = gates[:, hh:hh + 1]
        m0 = ms_ref[0:1, hh:hh + 1]
        d_log = jnp.where(causal, fc - f_cum_t[fh:fh + 1, :] + gates_t[hh:hh + 1, :], -jnp.inf)
        inter = fc + m0
        m = jnp.maximum(jnp.max(d_log, axis=-1, keepdims=True), inter)
        s = lax.dot_general(qh, kh.astype(BF16), NT_DIMS, preferred_element_type=F32) * jnp.exp(d_log - m)
        w_inter = jnp.exp(inter - m)
        c0 = cs_ref[hh]
        n0 = ns_ref[hh:hh + 1, :]
        num = (jnp.dot(s.astype(BF16), vh, preferred_element_type=F32)
               + w_inter * jnp.dot(qh, c0.astype(BF16), preferred_element_type=F32))
        den = (jnp.sum(s, axis=-1, keepdims=True)
               + w_inter * jnp.sum(q[:, cols] * n0, axis=-1, keepdims=True))
        h = num / jnp.maximum(jnp.abs(den), jnp.exp(-m))
        m_end = m[tc - 1:tc, :]
        f_last = fc[tc - 1:tc, :]
        w_end = jnp.exp(f_last - fc + i_col - m_end)
        decay = jnp.exp(f_last + m0 - m_end)
        wk = w_end * kh
        cs_ref[hh] = decay * c0 + lax.dot_general(wk.astype(BF16), vh, TN_DIMS, preferred_element_type=F32)
        ns_ref[hh:hh + 1, :] = decay * n0 + jnp.sum(wk, axis=0, keepdims=True)
        ms_ref[0:1, hh:hh + 1] = m_end
        h = jax.nn.sigmoid(o_ref[:, cols]) * h
        h = h - jnp.mean(h, axis=-1, keepdims=True)
        h = h * lax.rsqrt(jnp.mean(h * h, axis=-1, keepdims=True) + NORM_EPS)
        h_ref[:, cols] = (h * ng_ref[:, cols]).astype(h_ref.dtype)


def mlstm_scan(xc, v, o, gates, conv0, c0, n0, m0, conv_w, conv_b, wq_bd, wk_bd, gate_b, norm_g, *, tc):
    bsz, length, d_inner = xc.shape
    assert length % tc == 0
    seq = lambda w: pl.BlockSpec((None, tc, w), lambda b, c: (b, c, 0))
    per_b = lambda s: pl.BlockSpec((None,) + s, lambda b, c: (b,) + (0,) * len(s))
    row = lambda w: _resident((1, w), lambda b, c: (0, 0))
    return pl.pallas_call(
        functools.partial(_mlstm_kernel, tc=tc, d_inner=d_inner),
        grid=(bsz, length // tc),
        in_specs=[seq(d_inner), seq(d_inner), seq(d_inner), seq(gates.shape[2]), per_b(conv0.shape[1:]),
                  per_b(c0.shape[1:]), per_b(n0.shape[1:]), per_b(m0.shape[1:]),
                  _resident(conv_w.shape, lambda b, c: (0, 0)), row(d_inner),
                  _resident(wq_bd.shape, lambda b, c: (0, 0, 0)), _resident(wk_bd.shape, lambda b, c: (0, 0, 0)),
                  row(gates.shape[2]), row(d_inner)],
        out_specs=[seq(d_inner), per_b(c0.shape[1:]), per_b(n0.shape[1:]), per_b(m0.shape[1:])],
        out_shape=[jax.ShapeDtypeStruct((bsz, length, d_inner), BF16), jax.ShapeDtypeStruct(c0.shape, F32),
                   jax.ShapeDtypeStruct(n0.shape, F32), jax.ShapeDtypeStruct(m0.shape, F32)],
        scratch_shapes=[pltpu.VMEM((CARRY_ROWS, d_inner), F32)],
        compiler_params=_params("parallel", "arbitrary"),
        name="mlstm_scan",
    )(xc, v, o, gates, conv0, c0, n0, m0, conv_w, conv_b, wq_bd, wk_bd, gate_b, norm_g)


def _block_diag(w_blk, blk):
    n, c, d = w_blk.shape
    per = blk // c
    eye = jnp.eye(per, dtype=w_blk.dtype)
    tiles = w_blk.reshape(n // per, per, c, d)
    return jnp.einsum("jpcd,pq->jpcqd", tiles, eye).reshape(n // per, per * c, per * d)


def _mixer_a(xp, xs, shapes, cache_k, cache_v, cache_ki, w_in, t5_table):
    (bp, lp_), (bs, ls) = shapes
    hd, kvd, idd = A_HEADS * A_HEAD_DIM, A_KV_HEADS * A_HEAD_DIM, IDX_HEADS * IDX_DIM
    w_pad, cols = _pad_cols(w_in, (hd, kvd, kvd, idd, IDX_DIM, IDX_HEADS))
    log2e = math.log2(math.e)
    outs = [(cols[0], BF16, A_HEAD_DIM ** -0.5 * log2e), (cols[1], F32, None, A_KV_HEADS), (cols[1], BF16),
            (cols[2], F32, None, A_KV_HEADS), (cols[2], BF16), (cols[3], BF16), (cols[4], F32), (cols[4], BF16),
            (cols[5], F32)]
    w_pad = w_pad.astype(BF16)
    bias = _t5_bias_tiles(t5_table) * log2e
    on_lanes = lambda a: jnp.swapaxes(a, 1, 2)

    def with_ones(vv):
        b_, l_, _ = vv.shape
        v4 = vv.reshape(b_, l_, A_KV_HEADS, A_HEAD_DIM)
        return jnp.concatenate([v4, jnp.ones_like(v4)], axis=-1).reshape(b_, l_, 2 * kvd)

    q, k, kb, v, vb, qi, ki, kib, wi = project(xp, w_pad, outs, name="project_a")
    r3 = lambda a, b_, l: a.reshape(b_, l, a.shape[-1])
    att_p = dsa_attention(r3(q, bp, lp_), on_lanes(r3(qi, bp, lp_)), on_lanes(r3(wi, bp, lp_)),
                          on_lanes(r3(kb, bp, lp_)), with_ones(r3(vb, bp, lp_)), r3(kib, bp, lp_), bias, tq=KEY_TILE,
                          q_off=0,
                          l_true=lp_,
                          n_sel=min(IDX_TOPK_MAX, lp_ // 4))
    outs_p = (k.reshape(bp, lp_, A_KV_HEADS, A_HEAD_DIM), v.reshape(bp, lp_, A_KV_HEADS, A_HEAD_DIM),
              ki.reshape(bp, lp_, IDX_DIM))

    q, k, kb, v, vb, qi, ki, kib, wi = project(xs, w_pad, outs, name="project_a")
    past = cache_k.shape[1]
    total = past + ls
    lpad = -(-total // (2 * KEY_TILE)) * (2 * KEY_TILE)

    def with_past(cache, new):
        parts = [cache.reshape(bs, past, -1).astype(BF16), r3(new, bs, ls)]
        if lpad != total:
            parts.append(jnp.zeros((bs, lpad - total, new.shape[-1]), BF16))
        return jnp.concatenate(parts, axis=1)

    att_s = dsa_attention(r3(q, bs, ls), on_lanes(r3(qi, bs, ls)), on_lanes(r3(wi, bs, ls)),
                          on_lanes(with_past(cache_k, kb)), with_ones(with_past(cache_v, vb)),
                          with_past(cache_ki, kib), bias,
                          tq=ls, q_off=past,
                          l_true=total, n_sel=min(IDX_TOPK_MAX, total // 4))
    outs_s = (k.reshape(bs, ls, A_KV_HEADS, A_HEAD_DIM), v.reshape(bs, ls, A_KV_HEADS, A_HEAD_DIM),
              ki.reshape(bs, ls, IDX_DIM))
    return att_p.reshape(bp * lp_, hd), att_s.reshape(bs * ls, hd), outs_p, outs_s


def _mixer_b(xp, xs, shapes, cache_k, cache_v, w_in, rel_table):
    (bp, lp_), (bs, ls) = shapes
    hd = BAND_HEADS * BAND_HEAD_DIM
    w_pad, cols = _pad_cols(w_in, (hd, hd, hd))
    attn_outs = [(cols[0], BF16, BAND_HEAD_DIM ** -0.5), (cols[1], BF16), (cols[2], BF16)]
    kv_outs = [(cols[1], F32, None, BAND_HEADS), (cols[2], F32, None, BAND_HEADS)]
    w_pad = w_pad.astype(BF16)

    q, kb, vb = project(xp, w_pad, attn_outs, name="project_b")
    r3 = lambda a, b_, l: a.reshape(b_, l, hd)
    att_p = band_attention_prompt(r3(q, bp, lp_), r3(kb, bp, lp_), r3(vb, bp, lp_),
                                  _band_bias(rel_table, CHUNK, BAND_ROWS), grp=BAND_ROWS)
    keep = min(BAND_WINDOW, lp_)
    x_keep = xp.reshape(bp, lp_, -1)[:, lp_ - keep:].reshape(bp * keep, -1)
    k, v = project(x_keep, w_pad, kv_outs, name="project_b_keep")
    heads = lambda a, b_, l: a.reshape(b_, l, BAND_HEADS, BAND_HEAD_DIM)
    outs_p = (heads(k, bp, keep), heads(v, bp, keep))

    q, kb, vb = project(xs, w_pad, attn_outs, name="project_b")
    k, v = project(xs, w_pad, kv_outs, name="project_b_keep")
    past = cache_k.shape[1]
    att_s = band_attention_sample(r3(q, bs, ls), cache_k.reshape(bs, past, hd).astype(BF16), r3(kb, bs, ls),
                                  cache_v.reshape(bs, past, hd).astype(BF16), r3(vb, bs, ls),
                                  _band_bias(rel_table, ls, ls))
    outs_s = (jnp.concatenate([cache_k, heads(k, bs, ls)], axis=1)[:, ls:],
              jnp.concatenate([cache_v, heads(v, bs, ls)], axis=1)[:, ls:])
    return att_p.reshape(bp * lp_, hd), att_s.reshape(bs * ls, hd), outs_p, outs_s


def _mixer_c(xp, xs, shapes, ssm0, conv0, w_in, conv_w, conv_b, dt_bias, a_log, d_skip, norm_g):
    n_heads = a_log.shape[0]
    d_inner = n_heads * SSD_HEAD_DIM
    conv_dim = conv_w.shape[1]
    w_pad, cols = _pad_cols(w_in, (d_inner, conv_dim, n_heads))
    outs = [(cols[0], F32), (cols[1], F32), (cols[2], F32)]
    w_pad = w_pad.astype(BF16)
    row = lambda a: a.reshape(1, -1)
    d_skip_cols = jnp.repeat(d_skip, SSD_HEAD_DIM).reshape(1, d_inner)
    res = []
    for x, (b_, l), h0, c0 in ((xp, shapes[0], None, None), (xs, shapes[1], ssm0, conv0)):
        if h0 is None:
            h0 = jnp.zeros((b_, n_heads, SSD_HEAD_DIM, SSD_D_STATE), F32)
            c0 = jnp.zeros((b_, SSD_CONV - 1, conv_dim), F32)
        z, xbc, dt = project(x, w_pad, outs, name="project_c")
        xbc3 = xbc.reshape(b_, l, conv_dim)
        y, h_new = ssd_scan(xbc3, z.reshape(b_, l, d_inner), dt.reshape(b_, l, n_heads), c0, h0, conv_w,
                            row(conv_b), row(dt_bias), row(a_log), d_skip_cols, row(norm_g), tc=min(SSD_ROWS, l))
        conv_new = jnp.concatenate([c0, xbc3], axis=1)[:, l:]
        res.append((y.reshape(b_ * l, d_inner), (h_new, conv_new)))
    return res[0][0], res[1][0], res[0][1], res[1][1]


def _mixer_d(xp, xs, shapes, c0s, n0s, m0s, conv0s, w_in, conv_w, conv_b, wq_blk, wk_blk, gate_b, norm_g):
    d_inner = conv_w.shape[1]
    dh = d_inner // MLSTM_HEADS
    w_pad, cols = _pad_cols(w_in, (d_inner, d_inner, d_inner, 2 * MLSTM_HEADS))
    gcol = (cols[3][0], cols[3][1], cols[3][1])
    outs = [(cols[0], F32), (cols[1], BF16), (cols[2], F32), (gcol, F32)]
    w_pad = w_pad.astype(BF16)
    blk = 2 * LANES
    wq_bd = _block_diag(wq_blk, blk).astype(BF16)
    wk_bd = _block_diag(wk_blk, blk).astype(BF16)
    gate_b_pad = jnp.zeros((1, gcol[1]), F32).at[0, :2 * MLSTM_HEADS].set(gate_b)
    row = lambda a: a.reshape(1, -1)
    res = []
    for x, (b_, l), st in ((xp, shapes[0], None), (xs, shapes[1], (c0s, n0s, m0s, conv0s))):
        if st is None:
            st = (jnp.zeros((b_, MLSTM_HEADS, dh, dh), F32), jnp.zeros((b_, MLSTM_HEADS, dh), F32),
                  jnp.zeros((b_, MLSTM_HEADS), F32), jnp.zeros((b_, MLSTM_CONV - 1, d_inner), F32))
        c0, n0, m0, conv0 = st
        xc, v, o, gates = project(x, w_pad, outs, name="project_d")
        r3 = lambda a: a.reshape(b_, l, a.shape[-1])
        h, c_new, n_new, m_new = mlstm_scan(r3(xc), r3(v), r3(o), r3(gates), conv0, c0, n0,
                                            m0.reshape(b_, 1, MLSTM_HEADS), conv_w, row(conv_b), wq_bd, wk_bd,
                                            gate_b_pad, row(norm_g), tc=min(MLSTM_ROWS, l))
        conv_new = jnp.concatenate([conv0, r3(xc)], axis=1)[:, l:]
        res.append((h.reshape(b_ * l, d_inner), (c_new, n_new, m_new.reshape(b_, MLSTM_HEADS), conv_new)))
    return res[0][0], res[1][0], res[0][1], res[1][1]


def kernel(x_prompt, x_sample, cache_a_k, cache_a_v, cache_a_kidx, cache_b_k, cache_b_v, state_c_ssm, state_c_conv, state_d_c, state_d_n, state_d_m, state_d_conv, a_w_in, a_w_out, t5_table, b_w_in, b_w_out, b_rel_table, c_w_in, c_conv_w, c_conv_b, c_dt_bias, c_a_log, c_d_skip, c_norm_g, c_w_out, d_w_in, d_conv_w, d_conv_b, d_wq_blk, d_wk_blk, d_gate_b, d_norm_g, d_w_out, ffn1_wg, ffn1_wu, ffn1_wd, ffn2_wg, ffn2_wu, ffn2_wd, ln_g, ln_b):
    bp, lp_, d = x_prompt.shape
    bs, ls, _ = x_sample.shape
    depth = ffn1_wg.shape[0]
    alpha = (2.0 * depth) ** 0.25
    shapes = ((bp, lp_), (bs, ls))
    xp = x_prompt.reshape(bp * lp_, d)
    xs = x_sample.reshape(bs * ls, d)
    ffn_w = [[w.astype(BF16) for w in ws] for ws in ((ffn1_wg, ffn1_wu, ffn1_wd), (ffn2_wg, ffn2_wu, ffn2_wd))]
    w_out = [w.astype(BF16) for w in (a_w_out, b_w_out, c_w_out, d_w_out)]
    row = lambda a: a.reshape(1, -1)
    extra = {}
    for i in range(depth):
        g, b = ln_g[i], ln_b[i]
        xp = ffn_postnorm(xp, *ffn_w[0], i, row(g[0]), row(b[0]), alpha)
        xs = ffn_postnorm(xs, *ffn_w[0], i, row(g[0]), row(b[0]), alpha)
        kind = i % 4
        if kind == 0:
            mp, ms, op, os_ = _mixer_a(xp, xs, shapes, cache_a_k, cache_a_v, cache_a_kidx, a_w_in, t5_table)
        elif kind == 1:
            mp, ms, op, os_ = _mixer_b(xp, xs, shapes, cache_b_k, cache_b_v, b_w_in, b_rel_table)
        elif kind == 2:
            mp, ms, op, os_ = _mixer_c(xp, xs, shapes, state_c_ssm, state_c_conv, c_w_in, c_conv_w, c_conv_b,
                                       c_dt_bias, c_a_log, c_d_skip, c_norm_g)
        else:
            mp, ms, op, os_ = _mixer_d(xp, xs, shapes, state_d_c, state_d_n, state_d_m, state_d_conv, d_w_in,
                                       d_conv_w, d_conv_b, d_wq_blk, d_wk_blk, d_gate_b, d_norm_g)
        extra[kind] = (op, os_)
        xp = mixout_ffn(xp, mp, w_out[kind], row(g[1]), row(b[1]), *ffn_w[1], i, row(g[2]), row(b[2]), alpha)
        xs = mixout_ffn(xs, ms, w_out[kind], row(g[1]), row(b[1]), *ffn_w[1], i, row(g[2]), row(b[2]), alpha)
    prompt_side = tuple(t for kind in range(4) for t in extra[kind][0])
    sample_side = tuple(t for kind in range(4) for t in extra[kind][1])
    return (xp.reshape(bp, lp_, d), xs.reshape(bs, ls, d)) + prompt_side + sample_side
```

```python
import functools
import math

import numpy as np
import jax
import jax.numpy as jnp
from jax import lax
from jax.experimental import pallas as pl
from jax.experimental.pallas import tpu as pltpu

F32 = jnp.float32
BF16 = jnp.bfloat16
I32 = jnp.int32

CHUNK = 64
NORM_EPS = 1e-5
A_HEADS, A_KV_HEADS, A_HEAD_DIM = 8, 2, 128
IDX_HEADS, IDX_DIM, IDX_TOPK_MAX = 8, 64, 256
T5_BUCKETS, T5_MAX_DIST = 32, 128
BAND_HEADS, BAND_HEAD_DIM, BAND_LEFT_CHUNKS, BAND_MAX_REL = 16, 64, 8, 128
BAND_WINDOW = BAND_LEFT_CHUNKS * CHUNK
SSD_HEAD_DIM, SSD_GROUPS, SSD_D_STATE, SSD_CONV = 64, 8, 128, 4
MLSTM_HEADS, MLSTM_CONV, MLSTM_QK_BLOCK = 4, 4, 4

LANES = 128
NEG_BIG = -1e30
VMEM_LIMIT = 56 * 1024 * 1024

NT_DIMS = (((1,), (1,)), ((), ()))
TN_DIMS = (((0,), (0,)), ((), ()))


def _params(*sem):
    return pltpu.CompilerParams(dimension_semantics=sem, vmem_limit_bytes=VMEM_LIMIT)


def _resident(shape, index_map):
    return pl.BlockSpec(shape, index_map, pipeline_mode=pl.Buffered(1))


def _layer_norm(y, g, b):
    mu = jnp.mean(y, axis=-1, keepdims=True)
    yc = y - mu
    var = jnp.mean(yc * yc, axis=-1, keepdims=True)
    return yc * lax.rsqrt(var + NORM_EPS) * g + b


def _silu(x):
    return x * jax.nn.sigmoid(x)


def _row_tile(n, want):
    t = min(n, want)
    assert n % t == 0, (n, t)
    return t


def _ffn_block(x, wg_ref, wu_ref, wd_ref, g_ref, b_ref, alpha, f_cuts):
    xb = x.astype(BF16)
    acc = jnp.zeros(x.shape, F32)
    for lo, hi in zip(f_cuts[:-1], f_cuts[1:]):
        sl = slice(lo, hi)
        gate = jnp.dot(xb, wg_ref[:, sl], preferred_element_type=F32)
        up = jnp.dot(xb, wu_ref[:, sl], preferred_element_type=F32)
        h = (_silu(gate) * up).astype(BF16)
        acc = acc + jnp.dot(h, wd_ref[sl, :], preferred_element_type=F32)
    return _layer_norm(alpha * x + 0.5 * acc, g_ref[...], b_ref[...])


def _ffn_kernel(x_ref, wg_ref, wu_ref, wd_ref, g_ref, b_ref, o_ref, *, alpha, f_cuts, row_parts):
    rows = x_ref.shape[0] // row_parts
    for r in range(row_parts):
        rs = slice(r * rows, (r + 1) * rows)
        o_ref[rs, :] = _ffn_block(x_ref[rs, :], wg_ref, wu_ref, wd_ref, g_ref, b_ref, alpha, f_cuts)


def _mixout_ffn_kernel(x_ref, m_ref, wo_ref, g1_ref, b1_ref, wg_ref, wu_ref, wd_ref, g2_ref, b2_ref, o_ref, *,
                       alpha, f_cuts, row_parts):
    rows = x_ref.shape[0] // row_parts
    for r in range(row_parts):
        rs = slice(r * rows, (r + 1) * rows)
        sub = jnp.dot(m_ref[rs, :], wo_ref[...], preferred_element_type=F32)
        x1 = _layer_norm(alpha * x_ref[rs, :] + sub, g1_ref[...], b1_ref[...])
        o_ref[rs, :] = _ffn_block(x1, wg_ref, wu_ref, wd_ref, g2_ref, b2_ref, alpha, f_cuts)


MXU_WIDTH = 256
FFN_ROWS = 1024
FFN_ROW_PARTS = 4


def _ffn_cuts(d_ff, n_chunks):
    tiles = -(-d_ff // MXU_WIDTH)
    cuts = [min(d_ff, MXU_WIDTH * (-(-tiles * c // n_chunks))) for c in range(n_chunks + 1)]
    return tuple(cuts)


def ffn_postnorm(x, wg, wu, wd, layer, g, b, alpha, tm=FFN_ROWS, n_chunks=2, row_parts=FFN_ROW_PARTS):
    n, d = x.shape
    d_ff = wg.shape[2]
    tm = _row_tile(n, tm)
    return pl.pallas_call(
        functools.partial(_ffn_kernel, alpha=alpha, f_cuts=_ffn_cuts(d_ff, n_chunks), row_parts=row_parts),
        grid=(n // tm,),
        in_specs=[
            pl.BlockSpec((tm, d), lambda i: (i, 0)),
            _resident((None, d, d_ff), lambda i: (layer, 0, 0)),
            _resident((None, d, d_ff), lambda i: (layer, 0, 0)),
            _resident((None, d_ff, d), lambda i: (layer, 0, 0)),
            _resident((1, d), lambda i: (0, 0)),
            _resident((1, d), lambda i: (0, 0)),
        ],
        out_specs=pl.BlockSpec((tm, d), lambda i: (i, 0)),
        out_shape=jax.ShapeDtypeStruct((n, d), F32),
        compiler_params=_params("parallel"),
        name="ffn_postnorm",
    )(x, wg, wu, wd, g, b)


def mixout_ffn(x, m, w_out, g1, b1, wg, wu, wd, layer, g2, b2, alpha, tm=FFN_ROWS, n_chunks=2,
               row_parts=FFN_ROW_PARTS):
    n, d = x.shape
    k = m.shape[1]
    d_ff = wg.shape[2]
    tm = _row_tile(n, tm)
    vec = lambda: _resident((1, d), lambda i: (0, 0))
    return pl.pallas_call(
        functools.partial(_mixout_ffn_kernel, alpha=alpha, f_cuts=_ffn_cuts(d_ff, n_chunks), row_parts=row_parts),
        grid=(n // tm,),
        in_specs=[
            pl.BlockSpec((tm, d), lambda i: (i, 0)),
            pl.BlockSpec((tm, k), lambda i: (i, 0)),
            _resident((k, d), lambda i: (0, 0)), vec(), vec(),
            _resident((None, d, d_ff), lambda i: (layer, 0, 0)),
            _resident((None, d, d_ff), lambda i: (layer, 0, 0)),
            _resident((None, d_ff, d), lambda i: (layer, 0, 0)), vec(), vec(),
        ],
        out_specs=pl.BlockSpec((tm, d), lambda i: (i, 0)),
        out_shape=jax.ShapeDtypeStruct((n, d), F32),
        compiler_params=_params("parallel"),
        name="mixout_ffn",
    )(x, m, w_out, g1, b1, wg, wu, wd, g2, b2)


def _proj_kernel(x_ref, w_ref, *o_refs, cols, scales):
    xb = x_ref[...].astype(BF16)
    done = {}
    for (off, pad_w, true_w), scale, o_ref in zip(cols, scales, o_refs):
        if (off, pad_w) not in done:
            done[(off, pad_w)] = jnp.dot(xb, w_ref[:, off:off + pad_w], preferred_element_type=F32)
        y = done[(off, pad_w)][:, :true_w]
        y = (y if scale is None else y * scale).astype(o_ref.dtype)
        if len(o_ref.shape) == 3:
            head_w = o_ref.shape[2]
            for p in range(o_ref.shape[1]):
                o_ref[:, p, :] = y[:, p * head_w:(p + 1) * head_w]
        else:
            o_ref[...] = y


def _pad_cols(w, widths):
    pieces, offs, off, src = [], [], 0, 0
    for wd in widths:
        pad_w = -(-wd // LANES) * LANES
        pieces.append(w[:, src:src + wd])
        if pad_w != wd:
            pieces.append(jnp.zeros((w.shape[0], pad_w - wd), w.dtype))
        offs.append((off, pad_w, wd))
        off += pad_w
        src += wd
    return jnp.concatenate(pieces, axis=1), offs


def project(x, w_pad, outs, tm=512, name="project"):
    n, d = x.shape
    tm = _row_tile(n, tm)
    cols = tuple(o[0] for o in outs)
    scales = tuple(o[2] if len(o) > 2 else None for o in outs)

    def tail(o):
        heads = o[3] if len(o) > 3 else None
        return (o[0][2],) if heads is None else (heads, o[0][2] // heads)

    return pl.pallas_call(
        functools.partial(_proj_kernel, cols=cols, scales=scales),
        grid=(n // tm,),
        in_specs=[pl.BlockSpec((tm, d), lambda i: (i, 0)),
                  _resident(w_pad.shape, lambda i: (0, 0))],
        out_specs=[pl.BlockSpec((tm,) + tail(o), lambda i, nd=len(tail(o)): (i,) + (0,) * nd) for o in outs],
        out_shape=[jax.ShapeDtypeStruct((n,) + tail(o), o[1]) for o in outs],
        compiler_params=_params("parallel"),
        name=name,
    )(x, w_pad)


def _t5_bucket_np(rel):
    half = T5_BUCKETS // 2
    max_exact = half // 2
    n = np.abs(rel)
    nf = np.maximum(n, 1).astype(np.float32)
    large = max_exact + (np.log(nf / np.float32(max_exact)) / np.float32(math.log(T5_MAX_DIST / max_exact))
                         * np.float32(half - max_exact)).astype(np.int32)
    large = np.minimum(large, half - 1)
    return np.where(rel > 0, half, 0) + np.where(n < max_exact, n, large)


KEY_TILE = 2 * LANES


def _toeplitz(w, t_rows, s_cols):
    n = t_rows + s_cols
    lead = w.shape[:-1]
    wp = jnp.concatenate([w, jnp.zeros(lead + (1,), w.dtype)], axis=-1)
    flat = jnp.broadcast_to(wp[..., None, :], lead + (t_rows, n)).reshape(lead + (t_rows * n,))
    skew = flat[..., :t_rows * (n - 1)].reshape(lead + (t_rows, n - 1))
    return skew[..., t_rows - 1:t_rows - 1 + s_cols]


def _t5_bias_tiles(t5_table):
    assert T5_MAX_DIST <= KEY_TILE
    rel = np.arange(3 * KEY_TILE - 1) - (KEY_TILE - 1) - KEY_TILE
    far = t5_table[int(_t5_bucket_np(np.array(-T5_MAX_DIST)))]
    by_rel = (t5_table[_t5_bucket_np(rel)] - far).T
    strip = _toeplitz(by_rel, KEY_TILE, 2 * KEY_TILE)
    return jnp.stack([strip[:, :, KEY_TILE:], strip[:, :, :KEY_TILE]])


def _dsa_kernel(q_ref, qit_ref, wit_ref, kt_ref, v_ref, ki_ref, bias_ref, o_ref,
                keyt_scr, mask_scr, lg_scr, m_scr, acc_scr, *, tq, q_off, l_true, n_sel, lp):
    j = pl.program_id(1)
    q0 = q_off + j * tq
    kend = jnp.minimum(((q0 + tq - 1) // CHUNK + 1) * CHUNK, l_true)
    n_tiles = (kend + KEY_TILE - 1) // KEY_TILE
    n_far = jnp.maximum(q0 // KEY_TILE - 1, 0)
    int_min = jnp.int32(-2 ** 31)
    wide = 2 * KEY_TILE

    def tile_off(kt):
        return pl.multiple_of(kt * KEY_TILE, KEY_TILE)

    def slabs(lo, hi, fn):
        def pair(i, carry):
            fn(tile_off(lo + 2 * i), wide)
            return carry
        lax.fori_loop(0, (hi - lo) // 2, pair, 0)

        @pl.when((hi - lo) % 2 == 1)
        def _():
            fn(tile_off(hi - 1), KEY_TILE)

    wi = wit_ref[...] * (IDX_HEADS ** -0.5) * (IDX_DIM ** -0.5)

    def positions(off, n_keys):
        kpos = off + lax.broadcasted_iota(I32, (n_keys, tq), 0)
        q_chunk = (q0 + lax.broadcasted_iota(I32, (n_keys, tq), 1)) // CHUNK
        return kpos, ((kpos // CHUNK) <= q_chunk) & (kpos < l_true)

    def to_key(x):
        bits = lax.bitcast_convert_type(x, I32)
        return jnp.where(bits < 0, (bits ^ jnp.int32(0x7FFFFFFF)) + 1, bits)

    last_off = tile_off(n_tiles - 1)

    def score_slab(off, width, check_admissible=False):
        ki_t = ki_ref[pl.ds(off, width), :]
        accs = [jnp.zeros((width, tq), F32)] * 2
        for h in range(IDX_HEADS):
            s = jnp.dot(ki_t, qit_ref[h * IDX_DIM:(h + 1) * IDX_DIM, :], preferred_element_type=F32)
            accs[h % 2] = accs[h % 2] + jnp.maximum(s, 0.0) * wi[h:h + 1, :]
        sc = accs[0] + accs[1]
        if check_admissible:
            sc = jnp.where(positions(off, width)[1], sc, -jnp.inf)
        keyt_scr[pl.ds(off, width), :] = to_key(sc)

    slabs(0, n_tiles - 1, score_slab)
    score_slab(last_off, KEY_TILE, check_admissible=True)

    @pl.when(n_tiles % 2 == 1)
    def _():
        keyt_scr[pl.ds(tile_off(n_tiles), KEY_TILE), :] = to_key(jnp.full((KEY_TILE, tq), -jnp.inf, F32))

    sublanes = 8

    def count(pred):
        def body(ct, acc):
            off = pl.multiple_of(ct * wide, wide)
            parts = []
            for r0 in range(0, wide, LANES):
                kpos = off + r0 + lax.broadcasted_iota(I32, (LANES, tq), 0)
                hit = pred(keyt_scr[pl.ds(pl.multiple_of(off + r0, LANES), LANES), :], kpos).astype(I32)
                parts.append(jnp.sum(hit.reshape(LANES // sublanes, sublanes, tq), axis=0))
            return acc + ((parts[0] + parts[1]) + (parts[2] + parts[3]))
        acc = lax.fori_loop(0, (n_tiles + 1) // 2, body, jnp.zeros((sublanes, tq), I32))
        return jnp.sum(acc.astype(F32), axis=0, keepdims=True)

    check_every = 4

    def search_body(state):
        i, thr_u, settled, _ = state

        def one_bit(b, st):
            thr_u, settled = st
            cand_u = thr_u | lax.shift_left(jnp.int32(1), 31 - (i + b))
            cand = cand_u ^ int_min
            cnt = count(lambda key, kpos: key >= cand)
            thr_u = jnp.where((cnt >= n_sel) & (settled == 0), cand_u, thr_u)
            return thr_u, jnp.where(cnt == n_sel, 1, settled)

        thr_u, settled = lax.fori_loop(0, check_every, one_bit, (thr_u, settled))
        return i + check_every, thr_u, settled, jnp.min(settled.astype(F32))

    zeros = jnp.zeros((1, tq), I32)
    _, thr_u, _, _ = lax.while_loop(lambda s: (s[0] < 32) & (s[3] == 0.0), search_body,
                                    (jnp.int32(0), zeros, zeros, jnp.float32(0.0)))
    thr = thr_u ^ int_min
    need = n_sel - count(lambda key, kpos: key > thr)
    n_eq = count(lambda key, kpos: key == thr)
    has_tie = jnp.max(n_eq - need) > 0.0

    nbits = int(lp).bit_length()

    def tie_cut():
        def cut_body(i, cut):
            cand = cut | lax.shift_left(jnp.int32(1), nbits - 1 - i)
            cnt = count(lambda key, kpos: (key == thr) & (kpos < cand))
            return jnp.where(cnt <= need, cand, cut)
        return lax.fori_loop(0, nbits, cut_body, jnp.zeros((1, tq), I32))

    cut = lax.cond(has_tie, tie_cut, lambda: jnp.full((1, tq), 2 ** 30, I32))

    eye = (lax.broadcasted_iota(I32, (tq, tq), 0) == lax.broadcasted_iota(I32, (tq, tq), 1)).astype(BF16)

    def mask_tile(off, tie, check_admissible=False):
        key = keyt_scr[pl.ds(off, KEY_TILE), :]
        kpos, adm = positions(off, KEY_TILE)
        sel = ((key > thr) | ((key == thr) & (kpos < cut))) if tie else (key >= thr)
        if check_admissible:
            sel = sel & adm
        sel_q = lax.dot_general(eye, jnp.where(sel, 1.0, 0.0).astype(BF16), NT_DIMS, preferred_element_type=F32)
        mask_scr[:, pl.ds(off, KEY_TILE)] = jnp.where(sel_q > 0.5, 0.0, NEG_BIG)

    def mask_all(tie):
        def slab(off, width):
            for u in range(width // KEY_TILE):
                mask_tile(pl.multiple_of(off + u * KEY_TILE, KEY_TILE), tie)

        def run():
            slabs(0, n_tiles - 1, slab)
            mask_tile(last_off, tie, check_admissible=True)
        return run

    lax.cond(has_tie, mask_all(True), mask_all(False))

    grp = A_HEADS // A_KV_HEADS
    rows = grp * tq
    head_cols = lambda h: slice(h * A_HEAD_DIM, (h + 1) * A_HEAD_DIM)
    for g in range(A_KV_HEADS):
        qg = jnp.concatenate([q_ref[:, head_cols(g * grp + i)] for i in range(grp)], axis=0)
        m_scr[...] = jnp.full(m_scr.shape, NEG_BIG, F32)

        def qk_slab(off, width, near=False, qg=qg, g=g):
            lg = jnp.dot(qg, kt_ref[head_cols(g), pl.ds(off, width)], preferred_element_type=F32)
            lg = lg.reshape(grp, tq, width) + mask_scr[:, pl.ds(off, width)][None]
            if near:
                back = jnp.clip((q0 - off) // KEY_TILE, 0, 1)
                lg = lg + bias_ref[back, g * grp:(g + 1) * grp, 0:tq, :]
            lg = lg.reshape(rows, width)
            lg_scr[:, pl.ds(off, width)] = lg
            mx = lg[:, 0:LANES]
            for u in range(1, width // LANES):
                mx = jnp.maximum(mx, lg[:, u * LANES:(u + 1) * LANES])
            m_scr[...] = jnp.maximum(m_scr[...], mx)

        def near_body(kt, carry, qk_slab=qk_slab):
            qk_slab(tile_off(kt), KEY_TILE, near=True)
            return carry

        slabs(0, n_far, qk_slab)
        lax.fori_loop(n_far, n_tiles, near_body, 0)
        m_scr[...] = jnp.broadcast_to(jnp.max(m_scr[...], axis=-1, keepdims=True), m_scr.shape)
        acc_scr[...] = jnp.zeros(acc_scr.shape, F32)

        def pv_slab(off, width, g=g):
            lg = lg_scr[:, pl.ds(off, width)]
            m_b = m_scr[...]
            ps = [jnp.exp2(lg[:, u * LANES:(u + 1) * LANES] - m_b) for u in range(width // LANES)]
            p = jnp.concatenate(ps, axis=1).astype(BF16)
            v_ones = v_ref[pl.ds(off, width), 2 * g * A_HEAD_DIM:2 * (g + 1) * A_HEAD_DIM]
            acc_scr[...] += jnp.dot(p, v_ones, preferred_element_type=F32)

        slabs(0, n_tiles, pv_slab)
        out = acc_scr[:, 0:A_HEAD_DIM] / acc_scr[:, A_HEAD_DIM:2 * A_HEAD_DIM]
        for i in range(grp):
            o_ref[:, head_cols(g * grp + i)] = out[i * tq:(i + 1) * tq].astype(o_ref.dtype)


def dsa_attention(q, qi_t, wi_t, k_t, v, ki, bias, *, tq, q_off, l_true, n_sel):
    bsz, t_len, _ = q.shape
    lp = v.shape[1]
    assert lp % (2 * KEY_TILE) == 0 and t_len % tq == 0 and q_off % KEY_TILE == 0
    assert tq == KEY_TILE or (t_len == tq and tq < KEY_TILE)
    grp = A_HEADS // A_KV_HEADS
    qspec = lambda w: pl.BlockSpec((None, tq, w), lambda b, j: (b, j, 0))
    qtspec = lambda h: pl.BlockSpec((None, h, tq), lambda b, j: (b, 0, j))
    whole = lambda a: pl.BlockSpec((None,) + a.shape[1:], lambda b, j: (b, 0, 0))
    return pl.pallas_call(
        functools.partial(_dsa_kernel, tq=tq, q_off=q_off, l_true=l_true, n_sel=n_sel, lp=lp),
        grid=(bsz, t_len // tq),
        in_specs=[qspec(q.shape[2]), qtspec(qi_t.shape[1]), qtspec(wi_t.shape[1]),
                  whole(k_t), whole(v), whole(ki),
                  _resident(bias.shape, lambda b, j: (0, 0, 0, 0))],
        out_specs=qspec(q.shape[2]),
        out_shape=jax.ShapeDtypeStruct(q.shape, BF16),
        scratch_shapes=[pltpu.VMEM((lp, tq), I32), pltpu.VMEM((tq, lp), F32), pltpu.VMEM((grp * tq, lp), F32),
                        pltpu.VMEM((grp * tq, LANES), F32), pltpu.VMEM((grp * tq, 2 * A_HEAD_DIM), F32)],
        compiler_params=_params("parallel", "arbitrary"),
        name="dsa_attention",
    )(q, qi_t, wi_t, k_t, v, ki, bias)


BAND_ROWS = 4 * CHUNK


def _band_bias(rel_table, chunk, grp):
    win = BAND_WINDOW + grp
    t = np.arange(grp)[:, None]
    s = np.arange(win)[None, :]
    k = np.arange(grp + win - 1)
    idx = np.clip(BAND_WINDOW + (grp - 1) - k, -BAND_MAX_REL, BAND_MAX_REL) + BAND_MAX_REL
    bias = _toeplitz(rel_table[:, idx], grp, win)
    lo = (t // chunk) * chunk
    allowed = (s >= lo) & (s < lo + BAND_WINDOW + chunk)
    return jnp.where(allowed[None], bias, NEG_BIG)


def _band_kernel(q_ref, kp_ref, kc_ref, vp_ref, vc_ref, bias_ref, o_ref, kcat, vcat, *, grp, first_has_no_past):
    i = pl.program_id(1)
    prev = kp_ref.shape[0]
    tq = q_ref.shape[0]
    win = prev + grp
    kcat[0:prev, :] = kp_ref[...]
    kcat[prev:prev + tq, :] = kc_ref[...]
    vcat[0:prev, :] = vp_ref[...]
    vcat[prev:prev + tq, :] = vc_ref[...]
    lane = lax.broadcasted_iota(I32, (grp, win), 1)
    pair_w = 2 * BAND_HEAD_DIM
    assert pair_w == LANES
    first_half = lax.broadcasted_iota(I32, (grp, pair_w), 1) < BAND_HEAD_DIM

    def attend(mask_missing_past):
        for r0 in range(0, tq, grp):
            if mask_missing_past:
                kpos_ok = (r0 - prev + lane) >= 0
            for hp in range(BAND_HEADS // 2):
                cols = slice(hp * pair_w, (hp + 1) * pair_w)
                q_pair = q_ref[r0:r0 + grp, cols]
                k_pair = kcat[r0:r0 + win, cols]
                v_pair = vcat[r0:r0 + win, cols]
                outs = []
                for side in range(2):
                    q_one = jnp.where(first_half == (side == 0), q_pair, jnp.zeros_like(q_pair))
                    lg = lax.dot_general(q_one, k_pair, NT_DIMS, preferred_element_type=F32)
                    lg = lg + bias_ref[2 * hp + side]
                    if mask_missing_past:
                        lg = jnp.where(kpos_ok, lg, NEG_BIG)
                    p = jnp.exp(lg - jnp.max(lg, axis=-1, keepdims=True))
                    den = jnp.sum(p, axis=-1, keepdims=True)
                    outs.append(jnp.dot(p.astype(BF16), v_pair, preferred_element_type=F32) / den)
                o_ref[r0:r0 + grp, cols] = jnp.where(first_half, outs[0], outs[1]).astype(o_ref.dtype)

    if first_has_no_past:
        lax.cond(i == 0, lambda: attend(True), lambda: attend(False))
    else:
        attend(False)


def band_attention_prompt(q, k, v, bias, *, tq=512, grp=128):
    bsz, length, w = q.shape
    assert tq == BAND_WINDOW and length % tq == 0
    cur = pl.BlockSpec((None, tq, w), lambda b, i: (b, i, 0))
    prv = pl.BlockSpec((None, tq, w), lambda b, i: (b, jnp.maximum(i - 1, 0), 0))
    return pl.pallas_call(
        functools.partial(_band_kernel, grp=grp, first_has_no_past=True),
        grid=(bsz, length // tq),
        in_specs=[cur, prv, cur, prv, cur, _resident(bias.shape, lambda b, i: (0, 0, 0))],
        out_specs=cur,
        out_shape=jax.ShapeDtypeStruct(q.shape, BF16),
        scratch_shapes=[pltpu.VMEM((2 * tq, w), BF16), pltpu.VMEM((2 * tq, w), BF16)],
        compiler_params=_params("parallel", "arbitrary"),
        name="band_attention",
    )(q, k, k, v, v, bias)


def band_attention_sample(q, k_past, k_new, v_past, v_new, bias):
    bsz, t_len, w = q.shape
    past = k_past.shape[1]
    assert past == BAND_WINDOW
    new = pl.BlockSpec((None, t_len, w), lambda b, i: (b, 0, 0))
    old = pl.BlockSpec((None, past, w), lambda b, i: (b, 0, 0))
    return pl.pallas_call(
        functools.partial(_band_kernel, grp=t_len, first_has_no_past=False),
        grid=(bsz, 1),
        in_specs=[new, old, new, old, new, _resident(bias.shape, lambda b, i: (0, 0, 0))],
        out_specs=new,
        out_shape=jax.ShapeDtypeStruct(q.shape, BF16),
        scratch_shapes=[pltpu.VMEM((past + t_len, w), BF16), pltpu.VMEM((past + t_len, w), BF16)],
        compiler_params=_params("parallel", "arbitrary"),
        name="band_attention_sample",
    )(q, k_past, k_new, v_past, v_new, bias)


CARRY_ROWS = 8
SSD_ROWS = 128
MLSTM_ROWS = 256


def _conv_carry_init(conv0_ref, w_ref, carry_scr, width):
    for k in range(1, width):
        z = None
        for i in range(k, width):
            term = w_ref[width - 1 - i:width - i, :] * conv0_ref[width - 2 - (i - k):width - 1 - (i - k), :]
            z = term if z is None else z + term
        carry_scr[k - 1:k, :] = z


def _causal_conv_chunk(x_ref, carry_scr, w_ref, b_ref, tc, width):
    x = x_ref[...]
    first_row = lax.broadcasted_iota(I32, x.shape, 0) == 0
    z = None
    for k in range(width - 1, 0, -1):
        y = x * w_ref[width - 1 - k:width - k, :]
        z = y if z is None else y + delayed
        delayed = jnp.where(first_row, carry_scr[k - 1:k, :], pltpu.roll(z, 1, 0))
        carry_scr[k - 1:k, :] = z[tc - 1:tc, :]
    return b_ref[...] + x * w_ref[width - 1:width, :] + delayed


def _tri(tc):
    r = lax.broadcasted_iota(I32, (tc, tc), 0)
    c = lax.broadcasted_iota(I32, (tc, tc), 1)
    return r >= c


def _eye(n):
    r = lax.broadcasted_iota(I32, (n, n), 0)
    c = lax.broadcasted_iota(I32, (n, n), 1)
    return (r == c).astype(F32)


def _cumsum_rows(x, causal):
    return jnp.dot(causal.astype(F32), x, precision=lax.Precision.HIGHEST, preferred_element_type=F32)


def _transpose_f32(x):
    return lax.dot_general(_eye(x.shape[1]), x, NT_DIMS, precision=lax.Precision.HIGHEST,
                           preferred_element_type=F32)


def _ssd_kernel(xbc_ref, z_ref, dt_ref, conv0_ref, h0_ref, cw_ref, cb_ref, dtb_ref, alog_ref, dskip_ref,
                ng_ref, y_ref, hs_ref, carry_scr, y_scr, *, tc, d_inner, n_heads):
    c = pl.program_id(1)

    @pl.when(c == 0)
    def _():
        hs_ref[...] = h0_ref[...]
        _conv_carry_init(conv0_ref, cw_ref, carry_scr, SSD_CONV)

    xs = _silu(_causal_conv_chunk(xbc_ref, carry_scr, cw_ref, cb_ref, tc, SSD_CONV))
    gn = SSD_GROUPS * SSD_D_STATE
    hg = n_heads // SSD_GROUPS
    causal = _tri(tc)
    x_dt = dt_ref[...] + dtb_ref[...]
    dt = jnp.maximum(x_dt, 0.0) + jnp.log1p(jnp.exp(-jnp.abs(x_dt)))
    a_head = -jnp.exp(alog_ref[...])
    cum = _cumsum_rows(dt * a_head, causal)
    cum_t = _transpose_f32(cum)
    cum_last = cum[tc - 1:tc, :]
    pair_w = 2 * SSD_HEAD_DIM
    assert pair_w == LANES and hg % 2 == 0
    first_lanes = lax.broadcasted_iota(I32, (tc, pair_w), 1) < SSD_HEAD_DIM
    first_rows = lax.broadcasted_iota(I32, (pair_w, SSD_D_STATE), 0) < SSD_HEAD_DIM
    pick = lambda a, b: jnp.where(first_lanes, a, b)
    for g in range(SSD_GROUPS):
        bm = xs[:, d_inner + g * SSD_D_STATE:d_inner + (g + 1) * SSD_D_STATE].astype(BF16)
        cm = xs[:, d_inner + gn + g * SSD_D_STATE:d_inner + gn + (g + 1) * SSD_D_STATE].astype(BF16)
        cb = lax.dot_general(cm, bm, NT_DIMS, preferred_element_type=F32)
        for pp in range(hg // 2):
            p = g * (hg // 2) + pp
            ha, hb = 2 * p, 2 * p + 1
            cols = slice(p * pair_w, (p + 1) * pair_w)
            col_a, col_b = cum[:, ha:ha + 1], cum[:, hb:hb + 1]
            xh = xs[:, cols]
            xdt = xh * pick(dt[:, ha:ha + 1], dt[:, hb:hb + 1])
            xdt_b = xdt.astype(BF16)
            ys = []
            for h, col in ((ha, col_a), (hb, col_b)):
                decay = jnp.exp(jnp.where(causal, col - cum_t[h:h + 1, :], -jnp.inf))
                ys.append(jnp.dot((cb * decay).astype(BF16), xdt_b, preferred_element_type=F32))
            h0 = hs_ref[p]
            inter = lax.dot_general(cm, h0.astype(BF16), NT_DIMS, preferred_element_type=F32)
            y = pick(ys[0], ys[1]) + pick(jnp.exp(col_a), jnp.exp(col_b)) * inter
            y_scr[:, cols] = y + dskip_ref[:, cols] * xh
            last_a, last_b = cum_last[:, ha:ha + 1], cum_last[:, hb:hb + 1]
            w = (pick(jnp.exp(last_a - col_a), jnp.exp(last_b - col_b)) * xdt).astype(BF16)
            keep = jnp.where(first_rows, jnp.exp(last_a), jnp.exp(last_b))
            hs_ref[p] = keep * h0 + lax.dot_general(w, bm, TN_DIMS, preferred_element_type=F32)
    yz = y_scr[...] * _silu(z_ref[...])
    gw = d_inner // SSD_GROUPS
    for g in range(SSD_GROUPS):
        cols = slice(g * gw, (g + 1) * gw)
        seg = yz[:, cols]
        ms = jnp.mean(seg * seg, axis=-1, keepdims=True)
        y_ref[:, cols] = (seg * lax.rsqrt(ms + NORM_EPS) * ng_ref[:, cols]).astype(y_ref.dtype)


def ssd_scan(xbc, z, dt, conv0, h0, conv_w, conv_b, dt_bias, a_log, d_skip_cols, norm_g, *, tc):
    bsz, length, cc = xbc.shape
    d_inner = z.shape[2]
    n_heads = dt.shape[2]
    assert length % tc == 0
    state_shape = h0.shape
    h0 = h0.reshape(bsz, n_heads // 2, 2 * state_shape[2], state_shape[3])
    seq = lambda w: pl.BlockSpec((None, tc, w), lambda b, c: (b, c, 0))
    per_b3 = lambda s: pl.BlockSpec((None,) + s, lambda b, c: (b,) + (0,) * len(s))
    row = lambda w: _resident((1, w), lambda b, c: (0, 0))
    y, h_new = pl.pallas_call(
        functools.partial(_ssd_kernel, tc=tc, d_inner=d_inner, n_heads=n_heads),
        grid=(bsz, length // tc),
        in_specs=[seq(cc), seq(d_inner), seq(n_heads), per_b3(conv0.shape[1:]), per_b3(h0.shape[1:]),
                  _resident(conv_w.shape, lambda b, c: (0, 0)), row(cc), row(n_heads), row(n_heads),
                  row(d_inner), row(d_inner)],
        out_specs=[seq(d_inner), per_b3(h0.shape[1:])],
        out_shape=[jax.ShapeDtypeStruct((bsz, length, d_inner), BF16), jax.ShapeDtypeStruct(h0.shape, F32)],
        scratch_shapes=[pltpu.VMEM((CARRY_ROWS, cc), F32), pltpu.VMEM((tc, d_inner), F32)],
        compiler_params=_params("parallel", "arbitrary"),
        name="ssd_scan",
    )(xbc, z, dt, conv0, h0, conv_w, conv_b, dt_bias, a_log, d_skip_cols, norm_g)
    return y, h_new.reshape(state_shape)


def _mlstm_kernel(xc_ref, v_ref, o_ref, gates_ref, conv0_ref, c0_ref, n0_ref, m0_ref, cw_ref, cb_ref, wq_ref,
                  wk_ref, gb_ref, ng_ref, h_ref, cs_ref, ns_ref, ms_ref, carry_scr, *, tc, d_inner):
    c = pl.program_id(1)

    @pl.when(c == 0)
    def _():
        cs_ref[...] = c0_ref[...]
        ns_ref[...] = n0_ref[...]
        ms_ref[...] = m0_ref[...]
        _conv_carry_init(conv0_ref, cw_ref, carry_scr, MLSTM_CONV)

    xa = _silu(_causal_conv_chunk(xc_ref, carry_scr, cw_ref, cb_ref, tc, MLSTM_CONV)).astype(BF16)
    blk = wq_ref.shape[1]
    dh = d_inner // MLSTM_HEADS
    q = jnp.concatenate([jnp.dot(xa[:, j * blk:(j + 1) * blk], wq_ref[j], preferred_element_type=F32)
                         for j in range(d_inner // blk)], axis=1)
    k = jnp.concatenate([jnp.dot(xa[:, j * blk:(j + 1) * blk], wk_ref[j], preferred_element_type=F32)
                         for j in range(d_inner // blk)], axis=1) * dh ** -0.5
    causal = _tri(tc)
    gates = gates_ref[...] + gb_ref[...]
    log_f = jnp.minimum(gates, 0.0) - jnp.log1p(jnp.exp(-jnp.abs(gates)))
    f_cum = _cumsum_rows(log_f, causal)
    f_cum_t = _transpose_f32(f_cum)
    gates_t = _transpose_f32(gates)
    for hh in range(MLSTM_HEADS):
        cols = slice(hh * dh, (hh + 1) * dh)
        fh = MLSTM_HEADS + hh
        qh = q[:, cols].astype(BF16)
        kh = k[:, cols]
        vh = v_ref[:, cols]
        fc = f_cum[:, fh:fh + 1]
        i_col = gates[:, hh:hh + 1]
        m0 = ms_ref[0:1, hh:hh + 1]
        d_log = jnp.where(causal, fc - f_cum_t[fh:fh + 1, :] + gates_t[hh:hh + 1, :], -jnp.inf)
        inter = fc + m0
        m = jnp.maximum(jnp.max(d_log, axis=-1, keepdims=True), inter)
        s = lax.dot_general(qh, kh.astype(BF16), NT_DIMS, preferred_element_type=F32) * jnp.exp(d_log - m)
        w_inter = jnp.exp(inter - m)
        c0 = cs_ref[hh]
        n0 = ns_ref[hh:hh + 1, :]
        num = (jnp.dot(s.astype(BF16), vh, preferred_element_type=F32)
               + w_inter * jnp.dot(qh, c0.astype(BF16), preferred_element_type=F32))
        den = (jnp.sum(s, axis=-1, keepdims=True)
               + w_inter * jnp.sum(q[:, cols] * n0, axis=-1, keepdims=True))
        h = num / jnp.maximum(jnp.abs(den), jnp.exp(-m))
        m_end = m[tc - 1:tc, :]
        f_last = fc[tc - 1:tc, :]
        w_end = jnp.exp(f_last - fc + i_col - m_end)
        decay = jnp.exp(f_last + m0 - m_end)
        wk = w_end * kh
        cs_ref[hh] = decay * c0 + lax.dot_general(wk.astype(BF16), vh, TN_DIMS, preferred_element_type=F32)
        ns_ref[hh:hh + 1, :] = decay * n0 + jnp.sum(wk, axis=0, keepdims=True)
        ms_ref[0:1, hh:hh + 1] = m_end
        h = jax.nn.sigmoid(o_ref[:, cols]) * h
        h = h - jnp.mean(h, axis=-1, keepdims=True)
        h = h * lax.rsqrt(jnp.mean(h * h, axis=-1, keepdims=True) + NORM_EPS)
        h_ref[:, cols] = (h * ng_ref[:, cols]).astype(h_ref.dtype)


def mlstm_scan(xc, v, o, gates, conv0, c0, n0, m0, conv_w, conv_b, wq_bd, wk_bd, gate_b, norm_g, *, tc):
    bsz, length, d_inner = xc.shape
    assert length % tc == 0
    seq = lambda w: pl.BlockSpec((None, tc, w), lambda b, c: (b, c, 0))
    per_b = lambda s: pl.BlockSpec((None,) + s, lambda b, c: (b,) + (0,) * len(s))
    row = lambda w: _resident((1, w), lambda b, c: (0, 0))
    return pl.pallas_call(
        functools.partial(_mlstm_kernel, tc=tc, d_inner=d_inner),
        grid=(bsz, length // tc),
        in_specs=[seq(d_inner), seq(d_inner), seq(d_inner), seq(gates.shape[2]), per_b(conv0.shape[1:]),
                  per_b(c0.shape[1:]), per_b(n0.shape[1:]), per_b(m0.shape[1:]),
                  _resident(conv_w.shape, lambda b, c: (0, 0)), row(d_inner),
                  _resident(wq_bd.shape, lambda b, c: (0, 0, 0)), _resident(wk_bd.shape, lambda b, c: (0, 0, 0)),
                  row(gates.shape[2]), row(d_inner)],
        out_specs=[seq(d_inner), per_b(c0.shape[1:]), per_b(n0.shape[1:]), per_b(m0.shape[1:])],
        out_shape=[jax.ShapeDtypeStruct((bsz, length, d_inner), BF16), jax.ShapeDtypeStruct(c0.shape, F32),
                   jax.ShapeDtypeStruct(n0.shape, F32), jax.ShapeDtypeStruct(m0.shape, F32)],
        scratch_shapes=[pltpu.VMEM((CARRY_ROWS, d_inner), F32)],
        compiler_params=_params("parallel", "arbitrary"),
        name="mlstm_scan",
    )(xc, v, o, gates, conv0, c0, n0, m0, conv_w, conv_b, wq_bd, wk_bd, gate_b, norm_g)


def _block_diag(w_blk, blk):
    n, c, d = w_blk.shape
    per = blk // c
    eye = jnp.eye(per, dtype=w_blk.dtype)
    tiles = w_blk.reshape(n // per, per, c, d)
    return jnp.einsum("jpcd,pq->jpcqd", tiles, eye).reshape(n // per, per * c, per * d)


def _mixer_a(xp, xs, shapes, cache_k, cache_v, cache_ki, w_in, t5_table):
    (bp, lp_), (bs, ls) = shapes
    hd, kvd, idd = A_HEADS * A_HEAD_DIM, A_KV_HEADS * A_HEAD_DIM, IDX_HEADS * IDX_DIM
    w_pad, cols = _pad_cols(w_in, (hd, kvd, kvd, idd, IDX_DIM, IDX_HEADS))
    log2e = math.log2(math.e)
    outs = [(cols[0], BF16, A_HEAD_DIM ** -0.5 * log2e), (cols[1], F32, None, A_KV_HEADS), (cols[1], BF16),
            (cols[2], F32, None, A_KV_HEADS), (cols[2], BF16), (cols[3], BF16), (cols[4], F32), (cols[4], BF16),
            (cols[5], F32)]
    w_pad = w_pad.astype(BF16)
    bias = _t5_bias_tiles(t5_table) * log2e
    on_lanes = lambda a: jnp.swapaxes(a, 1, 2)

    def with_ones(vv):
        b_, l_, _ = vv.shape
        v4 = vv.reshape(b_, l_, A_KV_HEADS, A_HEAD_DIM)
        return jnp.concatenate([v4, jnp.ones_like(v4)], axis=-1).reshape(b_, l_, 2 * kvd)

    q, k, kb, v, vb, qi, ki, kib, wi = project(xp, w_pad, outs, name="project_a")
    r3 = lambda a, b_, l: a.reshape(b_, l, a.shape[-1])
    att_p = dsa_attention(r3(q, bp, lp_), on_lanes(r3(qi, bp, lp_)), on_lanes(r3(wi, bp, lp_)),
                          on_lanes(r3(kb, bp, lp_)), with_ones(r3(vb, bp, lp_)), r3(kib, bp, lp_), bias, tq=KEY_TILE,
                          q_off=0,
                          l_true=lp_,
                          n_sel=min(IDX_TOPK_MAX, lp_ // 4))
    outs_p = (k.reshape(bp, lp_, A_KV_HEADS, A_HEAD_DIM), v.reshape(bp, lp_, A_KV_HEADS, A_HEAD_DIM),
              ki.reshape(bp, lp_, IDX_DIM))

    q, k, kb, v, vb, qi, ki, kib, wi = project(xs, w_pad, outs, name="project_a")
    past = cache_k.shape[1]
    total = past + ls
    lpad = -(-total // (2 * KEY_TILE)) * (2 * KEY_TILE)

    def with_past(cache, new):
        parts = [cache.reshape(bs, past, -1).astype(BF16), r3(new, bs, ls)]
        if lpad != total:
            parts.append(jnp.zeros((bs, lpad - total, new.shape[-1]), BF16))
        return jnp.concatenate(parts, axis=1)

    att_s = dsa_attention(r3(q, bs, ls), on_lanes(r3(qi, bs, ls)), on_lanes(r3(wi, bs, ls)),
                          on_lanes(with_past(cache_k, kb)), with_ones(with_past(cache_v, vb)),
                          with_past(cache_ki, kib), bias,
                          tq=ls, q_off=past,
                          l_true=total, n_sel=min(IDX_TOPK_MAX, total // 4))
    outs_s = (k.reshape(bs, ls, A_KV_HEADS, A_HEAD_DIM), v.reshape(bs, ls, A_KV_HEADS, A_HEAD_DIM),
              ki.reshape(bs, ls, IDX_DIM))
    return att_p.reshape(bp * lp_, hd), att_s.reshape(bs * ls, hd), outs_p, outs_s


def _mixer_b(xp, xs, shapes, cache_k, cache_v, w_in, rel_table):
    (bp, lp_), (bs, ls) = shapes
    hd = BAND_HEADS * BAND_HEAD_DIM
    w_pad, cols = _pad_cols(w_in, (hd, hd, hd))
    attn_outs = [(cols[0], BF16, BAND_HEAD_DIM ** -0.5), (cols[1], BF16), (cols[2], BF16)]
    kv_outs = [(cols[1], F32, None, BAND_HEADS), (cols[2], F32, None, BAND_HEADS)]
    w_pad = w_pad.astype(BF16)

    q, kb, vb = project(xp, w_pad, attn_outs, name="project_b")
    r3 = lambda a, b_, l: a.reshape(b_, l, hd)
    att_p = band_attention_prompt(r3(q, bp, lp_), r3(kb, bp, lp_), r3(vb, bp, lp_),
                                  _band_bias(rel_table, CHUNK, BAND_ROWS), grp=BAND_ROWS)
    keep = min(BAND_WINDOW, lp_)
    x_keep = xp.reshape(bp, lp_, -1)[:, lp_ - keep:].reshape(bp * keep, -1)
    k, v = project(x_keep, w_pad, kv_outs, name="project_b_keep")
    heads = lambda a, b_, l: a.reshape(b_, l, BAND_HEADS, BAND_HEAD_DIM)
    outs_p = (heads(k, bp, keep), heads(v, bp, keep))

    q, kb, vb = project(xs, w_pad, attn_outs, name="project_b")
    k, v = project(xs, w_pad, kv_outs, name="project_b_keep")
    past = cache_k.shape[1]
    att_s = band_attention_sample(r3(q, bs, ls), cache_k.reshape(bs, past, hd).astype(BF16), r3(kb, bs, ls),
                                  cache_v.reshape(bs, past, hd).astype(BF16), r3(vb, bs, ls),
                                  _band_bias(rel_table, ls, ls))
    outs_s = (jnp.concatenate([cache_k, heads(k, bs, ls)], axis=1)[:, ls:],
              jnp.concatenate([cache_v, heads(v, bs, ls)], axis=1)[:, ls:])
    return att_p.reshape(bp * lp_, hd), att_s.reshape(bs * ls, hd), outs_p, outs_s


def _mixer_c(xp, xs, shapes, ssm0, conv0, w_in, conv_w, conv_b, dt_bias, a_log, d_skip, norm_g):
    n_heads = a_log.shape[0]
    d_inner = n_heads * SSD_HEAD_DIM
    conv_dim = conv_w.shape[1]
    w_pad, cols = _pad_cols(w_in, (d_inner, conv_dim, n_heads))
    outs = [(cols[0], F32), (cols[1], F32), (cols[2], F32)]
    w_pad = w_pad.astype(BF16)
    row = lambda a: a.reshape(1, -1)
    d_skip_cols = jnp.repeat(d_skip, SSD_HEAD_DIM).reshape(1, d_inner)
    res = []
    for x, (b_, l), h0, c0 in ((xp, shapes[0], None, None), (xs, shapes[1], ssm0, conv0)):
        if h0 is None:
            h0 = jnp.zeros((b_, n_heads, SSD_HEAD_DIM, SSD_D_STATE), F32)
            c0 = jnp.zeros((b_, SSD_CONV - 1, conv_dim), F32)
        z, xbc, dt = project(x, w_pad, outs, name="project_c")
        xbc3 = xbc.reshape(b_, l, conv_dim)
        y, h_new = ssd_scan(xbc3, z.reshape(b_, l, d_inner), dt.reshape(b_, l, n_heads), c0, h0, conv_w,
                            row(conv_b), row(dt_bias), row(a_log), d_skip_cols, row(norm_g), tc=min(SSD_ROWS, l))
        conv_new = jnp.concatenate([c0, xbc3], axis=1)[:, l:]
        res.append((y.reshape(b_ * l, d_inner), (h_new, conv_new)))
    return res[0][0], res[1][0], res[0][1], res[1][1]


def _mixer_d(xp, xs, shapes, c0s, n0s, m0s, conv0s, w_in, conv_w, conv_b, wq_blk, wk_blk, gate_b, norm_g):
    d_inner = conv_w.shape[1]
    dh = d_inner // MLSTM_HEADS
    w_pad, cols = _pad_cols(w_in, (d_inner, d_inner, d_inner, 2 * MLSTM_HEADS))
    gcol = (cols[3][0], cols[3][1], cols[3][1])
    outs = [(cols[0], F32), (cols[1], BF16), (cols[2], F32), (gcol, F32)]
    w_pad = w_pad.astype(BF16)
    blk = 2 * LANES
    wq_bd = _block_diag(wq_blk, blk).astype(BF16)
    wk_bd = _block_diag(wk_blk, blk).astype(BF16)
    gate_b_pad = jnp.zeros((1, gcol[1]), F32).at[0, :2 * MLSTM_HEADS].set(gate_b)
    row = lambda a: a.reshape(1, -1)
    res = []
    for x, (b_, l), st in ((xp, shapes[0], None), (xs, shapes[1], (c0s, n0s, m0s, conv0s))):
        if st is None:
            st = (jnp.zeros((b_, MLSTM_HEADS, dh, dh), F32), jnp.zeros((b_, MLSTM_HEADS, dh), F32),
                  jnp.zeros((b_, MLSTM_HEADS), F32), jnp.zeros((b_, MLSTM_CONV - 1, d_inner), F32))
        c0, n0, m0, conv0 = st
        xc, v, o, gates = project(x, w_pad, outs, name="project_d")
        r3 = lambda a: a.reshape(b_, l, a.shape[-1])
        h, c_new, n_new, m_new = mlstm_scan(r3(xc), r3(v), r3(o), r3(gates), conv0, c0, n0,
                                            m0.reshape(b_, 1, MLSTM_HEADS), conv_w, row(conv_b), wq_bd, wk_bd,
                                            gate_b_pad, row(norm_g), tc=min(MLSTM_ROWS, l))
        conv_new = jnp.concatenate([conv0, r3(xc)], axis=1)[:, l:]
        res.append((h.reshape(b_ * l, d_inner), (c_new, n_new, m_new.reshape(b_, MLSTM_HEADS), conv_new)))
    return res[0][0], res[1][0], res[0][1], res[1][1]


def kernel(x_prompt, x_sample, cache_a_k, cache_a_v, cache_a_kidx, cache_b_k, cache_b_v, state_c_ssm, state_c_conv, state_d_c, state_d_n, state_d_m, state_d_conv, a_w_in, a_w_out, t5_table, b_w_in, b_w_out, b_rel_table, c_w_in, c_conv_w, c_conv_b, c_dt_bias, c_a_log, c_d_skip, c_norm_g, c_w_out, d_w_in, d_conv_w, d_conv_b, d_wq_blk, d_wk_blk, d_gate_b, d_norm_g, d_w_out, ffn1_wg, ffn1_wu, ffn1_wd, ffn2_wg, ffn2_wu, ffn2_wd, ln_g, ln_b):
    bp, lp_, d = x_prompt.shape
    bs, ls, _ = x_sample.shape
    depth = ffn1_wg.shape[0]
    alpha = (2.0 * depth) ** 0.25
    shapes = ((bp, lp_), (bs, ls))
    xp = x_prompt.reshape(bp * lp_, d)
    xs = x_sample.reshape(bs * ls, d)
    ffn_w = [[w.astype(BF16) for w in ws] for ws in ((ffn1_wg, ffn1_wu, ffn1_wd), (ffn2_wg, ffn2_wu, ffn2_wd))]
    w_out = [w.astype(BF16) for w in (a_w_out, b_w_out, c_w_out, d_w_out)]
    row = lambda a: a.reshape(1, -1)
    extra = {}
    for i in range(depth):
        g, b = ln_g[i], ln_b[i]
        xp = ffn_postnorm(xp, *ffn_w[0], i, row(g[0]), row(b[0]), alpha)
        xs = ffn_postnorm(xs, *ffn_w[0], i, row(g[0]), row(b[0]), alpha)
        kind = i % 4
        if kind == 0:
            mp, ms, op, os_ = _mixer_a(xp, xs, shapes, cache_a_k, cache_a_v, cache_a_kidx, a_w_in, t5_table)
        elif kind == 1:
            mp, ms, op, os_ = _mixer_b(xp, xs, shapes, cache_b_k, cache_b_v, b_w_in, b_rel_table)
        elif kind == 2:
            mp, ms, op, os_ = _mixer_c(xp, xs, shapes, state_c_ssm, state_c_conv, c_w_in, c_conv_w, c_conv_b,
                                       c_dt_bias, c_a_log, c_d_skip, c_norm_g)
        else:
            mp, ms, op, os_ = _mixer_d(xp, xs, shapes, state_d_c, state_d_n, state_d_m, state_d_conv, d_w_in,
                                       d_conv_w, d_conv_b, d_wq_blk, d_wk_blk, d_gate_b, d_norm_g)
        extra[kind] = (op, os_)
        xp = mixout_ffn(xp, mp, w_out[kind], row(g[1]), row(b[1]), *ffn_w[1], i, row(g[2]), row(b[2]), alpha)
        xs = mixout_ffn(xs, ms, w_out[kind], row(g[1]), row(b[1]), *ffn_w[1], i, row(g[2]), row(b[2]), alpha)
    prompt_side = tuple(t for kind in range(4) for t in extra[kind][0])
    sample_side = tuple(t for kind in range(4) for t in extra[kind][1])
    return (xp.reshape(bp, lp_, d), xs.reshape(bs, ls, d)) + prompt_side + sample_side
```

```python
import functools
import math

import numpy as np
import jax
import jax.numpy as jnp
from jax import lax
from jax.experimental import pallas as pl
from jax.experimental.pallas import tpu as pltpu

F32 = jnp.float32
BF16 = jnp.bfloat16
I32 = jnp.int32

CHUNK = 64
NORM_EPS = 1e-5
A_HEADS, A_KV_HEADS, A_HEAD_DIM = 8, 2, 128
IDX_HEADS, IDX_DIM, IDX_TOPK_MAX = 8, 64, 256
T5_BUCKETS, T5_MAX_DIST = 32, 128
BAND_HEADS, BAND_HEAD_DIM, BAND_LEFT_CHUNKS, BAND_MAX_REL = 16, 64, 8, 128
BAND_WINDOW = BAND_LEFT_CHUNKS * CHUNK
SSD_HEAD_DIM, SSD_GROUPS, SSD_D_STATE, SSD_CONV = 64, 8, 128, 4
MLSTM_HEADS, MLSTM_CONV, MLSTM_QK_BLOCK = 4, 4, 4

LANES = 128
NEG_BIG = -1e30
VMEM_LIMIT = 56 * 1024 * 1024

NT_DIMS = (((1,), (1,)), ((), ()))
TN_DIMS = (((0,), (0,)), ((), ()))


def _params(*sem):
    return pltpu.CompilerParams(dimension_semantics=sem, vmem_limit_bytes=VMEM_LIMIT)


def _resident(shape, index_map):
    return pl.BlockSpec(shape, index_map, pipeline_mode=pl.Buffered(1))


def _layer_norm(y, g, b):
    mu = jnp.mean(y, axis=-1, keepdims=True)
    yc = y - mu
    var = jnp.mean(yc * yc, axis=-1, keepdims=True)
    return yc * lax.rsqrt(var + NORM_EPS) * g + b


def _silu(x):
    return x * jax.nn.sigmoid(x)


def _row_tile(n, want):
    t = min(n, want)
    assert n % t == 0, (n, t)
    return t


def _ffn_block(x, wg_ref, wu_ref, wd_ref, g_ref, b_ref, alpha, f_cuts):
    xb = x.astype(BF16)
    acc = jnp.zeros(x.shape, F32)
    for lo, hi in zip(f_cuts[:-1], f_cuts[1:]):
        sl = slice(lo, hi)
        gate = jnp.dot(xb, wg_ref[:, sl], preferred_element_type=F32)
        up = jnp.dot(xb, wu_ref[:, sl], preferred_element_type=F32)
        h = (_silu(gate) * up).astype(BF16)
        acc = acc + jnp.dot(h, wd_ref[sl, :], preferred_element_type=F32)
    return _layer_norm(alpha * x + 0.5 * acc, g_ref[...], b_ref[...])


def _ffn_kernel(x_ref, wg_ref, wu_ref, wd_ref, g_ref, b_ref, o_ref, *, alpha, f_cuts, row_parts):
    rows = x_ref.shape[0] // row_parts
    for r in range(row_parts):
        rs = slice(r * rows, (r + 1) * rows)
        o_ref[rs, :] = _ffn_block(x_ref[rs, :], wg_ref, wu_ref, wd_ref, g_ref, b_ref, alpha, f_cuts)


def _mixout_ffn_kernel(x_ref, m_ref, wo_ref, g1_ref, b1_ref, wg_ref, wu_ref, wd_ref, g2_ref, b2_ref, o_ref, *,
                       alpha, f_cuts, row_parts):
    rows = x_ref.shape[0] // row_parts
    for r in range(row_parts):
        rs = slice(r * rows, (r + 1) * rows)
        sub = jnp.dot(m_ref[rs, :], wo_ref[...], preferred_element_type=F32)
        x1 = _layer_norm(alpha * x_ref[rs, :] + sub, g1_ref[...], b1_ref[...])
        o_ref[rs, :] = _ffn_block(x1, wg_ref, wu_ref, wd_ref, g2_ref, b2_ref, alpha, f_cuts)


MXU_WIDTH = 256
FFN_ROWS = 1024
FFN_ROW_PARTS = 4


def _ffn_cuts(d_ff, n_chunks):
    tiles = -(-d_ff // MXU_WIDTH)
    cuts = [min(d_ff, MXU_WIDTH * (-(-tiles * c // n_chunks))) for c in range(n_chunks + 1)]
    return tuple(cuts)


def ffn_postnorm(x, wg, wu, wd, layer, g, b, alpha, tm=FFN_ROWS, n_chunks=2, row_parts=FFN_ROW_PARTS):
    n, d = x.shape
    d_ff = wg.shape[2]
    tm = _row_tile(n, tm)
    return pl.pallas_call(
        functools.partial(_ffn_kernel, alpha=alpha, f_cuts=_ffn_cuts(d_ff, n_chunks), row_parts=row_parts),
        grid=(n // tm,),
        in_specs=[
            pl.BlockSpec((tm, d), lambda i: (i, 0)),
            _resident((None, d, d_ff), lambda i: (layer, 0, 0)),
            _resident((None, d, d_ff), lambda i: (layer, 0, 0)),
            _resident((None, d_ff, d), lambda i: (layer, 0, 0)),
            _resident((1, d), lambda i: (0, 0)),
            _resident((1, d), lambda i: (0, 0)),
        ],
        out_specs=pl.BlockSpec((tm, d), lambda i: (i, 0)),
        out_shape=jax.ShapeDtypeStruct((n, d), F32),
        compiler_params=_params("parallel"),
        name="ffn_postnorm",
    )(x, wg, wu, wd, g, b)


def mixout_ffn(x, m, w_out, g1, b1, wg, wu, wd, layer, g2, b2, alpha, tm=FFN_ROWS, n_chunks=2,
               row_parts=FFN_ROW_PARTS):
    n, d = x.shape
    k = m.shape[1]
    d_ff = wg.shape[2]
    tm = _row_tile(n, tm)
    vec = lambda: _resident((1, d), lambda i: (0, 0))
    return pl.pallas_call(
        functools.partial(_mixout_ffn_kernel, alpha=alpha, f_cuts=_ffn_cuts(d_ff, n_chunks), row_parts=row_parts),
        grid=(n // tm,),
        in_specs=[
            pl.BlockSpec((tm, d), lambda i: (i, 0)),
            pl.BlockSpec((tm, k), lambda i: (i, 0)),
            _resident((k, d), lambda i: (0, 0)), vec(), vec(),
            _resident((None, d, d_ff), lambda i: (layer, 0, 0)),
            _resident((None, d, d_ff), lambda i: (layer, 0, 0)),
            _resident((None, d_ff, d), lambda i: (layer, 0, 0)), vec(), vec(),
        ],
        out_specs=pl.BlockSpec((tm, d), lambda i: (i, 0)),
        out_shape=jax.ShapeDtypeStruct((n, d), F32),
        compiler_params=_params("parallel"),
        name="mixout_ffn",
    )(x, m, w_out, g1, b1, wg, wu, wd, g2, b2)


def _proj_kernel(x_ref, w_ref, *o_refs, cols, scales):
    xb = x_ref[...].astype(BF16)
    done = {}
    for (off, pad_w, true_w), scale, o_ref in zip(cols, scales, o_refs):
        if (off, pad_w) not in done:
            done[(off, pad_w)] = jnp.dot(xb, w_ref[:, off:off + pad_w], preferred_element_type=F32)
        y = done[(off, pad_w)][:, :true_w]
        y = (y if scale is None else y * scale).astype(o_ref.dtype)
        if len(o_ref.shape) == 3:
            head_w = o_ref.shape[2]
            for p in range(o_ref.shape[1]):
                o_ref[:, p, :] = y[:, p * head_w:(p + 1) * head_w]
        else:
            o_ref[...] = y


def _ffn_project_kernel(x_ref, wg_ref, wu_ref, wd_ref, g_ref, b_ref, w_ref, x1_ref, *o_refs, alpha, f_cuts,
                        row_parts, cols, scales):
    rows = x_ref.shape[0] // row_parts
    for r in range(row_parts):
        rs = slice(r * rows, (r + 1) * rows)
        x1 = _ffn_block(x_ref[rs, :], wg_ref, wu_ref, wd_ref, g_ref, b_ref, alpha, f_cuts)
        x1_ref[rs, :] = x1
        xb = x1.astype(BF16)
        for (off, pad_w, true_w), scale, o_ref in zip(cols, scales, o_refs):
            y = jnp.dot(xb, w_ref[:, off:off + pad_w], preferred_element_type=F32)[:, :true_w]
            o_ref[rs, :] = (y if scale is None else y * scale).astype(o_ref.dtype)


def ffn_project(x, wg, wu, wd, layer, g, b, alpha, w_pad, outs, tm=512, n_chunks=2, row_parts=2):
    n, d = x.shape
    d_ff = wg.shape[2]
    tm = _row_tile(n, tm)
    cols = tuple(o[0] for o in outs)
    scales = tuple(o[2] if len(o) > 2 else None for o in outs)
    vec = lambda: _resident((1, d), lambda i: (0, 0))
    rows_of = lambda w: pl.BlockSpec((tm, w), lambda i: (i, 0))
    return pl.pallas_call(
        functools.partial(_ffn_project_kernel, alpha=alpha, f_cuts=_ffn_cuts(d_ff, n_chunks), row_parts=row_parts,
                          cols=cols, scales=scales),
        grid=(n // tm,),
        in_specs=[rows_of(d),
                  _resident((None, d, d_ff), lambda i: (layer, 0, 0)),
                  _resident((None, d, d_ff), lambda i: (layer, 0, 0)),
                  _resident((None, d_ff, d), lambda i: (layer, 0, 0)), vec(), vec(),
                  _resident(w_pad.shape, lambda i: (0, 0))],
        out_specs=[rows_of(d)] + [rows_of(c[2]) for c in cols],
        out_shape=[jax.ShapeDtypeStruct((n, d), F32)] + [jax.ShapeDtypeStruct((n, o[0][2]), o[1]) for o in outs],
        compiler_params=_params("parallel"),
        name="ffn_project",
    )(x, wg, wu, wd, g, b, w_pad)


def _pad_cols(w, widths):
    pieces, offs, off, src = [], [], 0, 0
    for wd in widths:
        pad_w = -(-wd // LANES) * LANES
        pieces.append(w[:, src:src + wd])
        if pad_w != wd:
            pieces.append(jnp.zeros((w.shape[0], pad_w - wd), w.dtype))
        offs.append((off, pad_w, wd))
        off += pad_w
        src += wd
    return jnp.concatenate(pieces, axis=1), offs


def project(x, w_pad, outs, tm=512, name="project"):
    n, d = x.shape
    tm = _row_tile(n, tm)
    cols = tuple(o[0] for o in outs)
    scales = tuple(o[2] if len(o) > 2 else None for o in outs)

    def tail(o):
        heads = o[3] if len(o) > 3 else None
        return (o[0][2],) if heads is None else (heads, o[0][2] // heads)

    return pl.pallas_call(
        functools.partial(_proj_kernel, cols=cols, scales=scales),
        grid=(n // tm,),
        in_specs=[pl.BlockSpec((tm, d), lambda i: (i, 0)),
                  _resident(w_pad.shape, lambda i: (0, 0))],
        out_specs=[pl.BlockSpec((tm,) + tail(o), lambda i, nd=len(tail(o)): (i,) + (0,) * nd) for o in outs],
        out_shape=[jax.ShapeDtypeStruct((n,) + tail(o), o[1]) for o in outs],
        compiler_params=_params("parallel"),
        name=name,
    )(x, w_pad)


def _t5_bucket_np(rel):
    half = T5_BUCKETS // 2
    max_exact = half // 2
    n = np.abs(rel)
    nf = np.maximum(n, 1).astype(np.float32)
    large = max_exact + (np.log(nf / np.float32(max_exact)) / np.float32(math.log(T5_MAX_DIST / max_exact))
                         * np.float32(half - max_exact)).astype(np.int32)
    large = np.minimum(large, half - 1)
    return np.where(rel > 0, half, 0) + np.where(n < max_exact, n, large)


KEY_TILE = 2 * LANES


def _toeplitz(w, t_rows, s_cols):
    n = t_rows + s_cols
    lead = w.shape[:-1]
    wp = jnp.concatenate([w, jnp.zeros(lead + (1,), w.dtype)], axis=-1)
    flat = jnp.broadcast_to(wp[..., None, :], lead + (t_rows, n)).reshape(lead + (t_rows * n,))
    skew = flat[..., :t_rows * (n - 1)].reshape(lead + (t_rows, n - 1))
    return skew[..., t_rows - 1:t_rows - 1 + s_cols]


def _t5_bias_tiles(t5_table):
    assert T5_MAX_DIST <= KEY_TILE
    rel = np.arange(3 * KEY_TILE - 1) - (KEY_TILE - 1) - KEY_TILE
    far = t5_table[int(_t5_bucket_np(np.array(-T5_MAX_DIST)))]
    by_rel = (t5_table[_t5_bucket_np(rel)] - far).T
    strip = _toeplitz(by_rel, KEY_TILE, 2 * KEY_TILE)
    return jnp.stack([strip[:, :, KEY_TILE:], strip[:, :, :KEY_TILE]])


def _dsa_kernel(q_ref, qit_ref, wit_ref, kt_ref, v_ref, ki_ref, bias_ref, o_ref,
                keyt_scr, mask_scr, lg_scr, m_scr, acc_scr, *, tq, q_off, l_true, n_sel, lp):
    j = pl.program_id(1)
    q0 = q_off + j * tq
    kend = jnp.minimum(((q0 + tq - 1) // CHUNK + 1) * CHUNK, l_true)
    n_tiles = (kend + KEY_TILE - 1) // KEY_TILE
    n_far = jnp.maximum(q0 // KEY_TILE - 1, 0)
    int_min = jnp.int32(-2 ** 31)
    wide = 2 * KEY_TILE

    def tile_off(kt):
        return pl.multiple_of(kt * KEY_TILE, KEY_TILE)

    def slabs(lo, hi, fn):
        def pair(i, carry):
            fn(tile_off(lo + 2 * i), wide)
            return carry
        lax.fori_loop(0, (hi - lo) // 2, pair, 0)

        @pl.when((hi - lo) % 2 == 1)
        def _():
            fn(tile_off(hi - 1), KEY_TILE)

    wi = wit_ref[...] * (IDX_HEADS ** -0.5) * (IDX_DIM ** -0.5)

    def positions(off, n_keys):
        kpos = off + lax.broadcasted_iota(I32, (n_keys, tq), 0)
        q_chunk = (q0 + lax.broadcasted_iota(I32, (n_keys, tq), 1)) // CHUNK
        return kpos, ((kpos // CHUNK) <= q_chunk) & (kpos < l_true)

    def to_key(x):
        bits = lax.bitcast_convert_type(x, I32)
        return jnp.where(bits < 0, (bits ^ jnp.int32(0x7FFFFFFF)) + 1, bits)

    last_off = tile_off(n_tiles - 1)

    def score_slab(off, width, check_admissible=False):
        ki_t = ki_ref[pl.ds(off, width), :]
        accs = [jnp.zeros((width, tq), F32)] * 2
        for h in range(IDX_HEADS):
            s = jnp.dot(ki_t, qit_ref[h * IDX_DIM:(h + 1) * IDX_DIM, :], preferred_element_type=F32)
            accs[h % 2] = accs[h % 2] + jnp.maximum(s, 0.0) * wi[h:h + 1, :]
        sc = accs[0] + accs[1]
        if check_admissible:
            sc = jnp.where(positions(off, width)[1], sc, -jnp.inf)
        keyt_scr[pl.ds(off, width), :] = to_key(sc)

    slabs(0, n_tiles - 1, score_slab)
    score_slab(last_off, KEY_TILE, check_admissible=True)

    @pl.when(n_tiles % 2 == 1)
    def _():
        keyt_scr[pl.ds(tile_off(n_tiles), KEY_TILE), :] = to_key(jnp.full((KEY_TILE, tq), -jnp.inf, F32))

    sublanes = 8

    def count(pred):
        def body(ct, acc):
            off = pl.multiple_of(ct * wide, wide)
            parts = []
            for r0 in range(0, wide, LANES):
                kpos = off + r0 + lax.broadcasted_iota(I32, (LANES, tq), 0)
                hit = pred(keyt_scr[pl.ds(pl.multiple_of(off + r0, LANES), LANES), :], kpos).astype(I32)
                parts.append(jnp.sum(hit.reshape(LANES // sublanes, sublanes, tq), axis=0))
            return acc + ((parts[0] + parts[1]) + (parts[2] + parts[3]))
        acc = lax.fori_loop(0, (n_tiles + 1) // 2, body, jnp.zeros((sublanes, tq), I32))
        return jnp.sum(acc.astype(F32), axis=0, keepdims=True)

    check_every = 4

    def search_body(state):
        i, thr_u, settled, _ = state

        def one_bit(b, st):
            thr_u, settled = st
            cand_u = thr_u | lax.shift_left(jnp.int32(1), 31 - (i + b))
            cand = cand_u ^ int_min
            cnt = count(lambda key, kpos: key >= cand)
            thr_u = jnp.where((cnt >= n_sel) & (settled == 0), cand_u, thr_u)
            return thr_u, jnp.where(cnt == n_sel, 1, settled)

        thr_u, settled = lax.fori_loop(0, check_every, one_bit, (thr_u, settled))
        return i + check_every, thr_u, settled, jnp.min(settled.astype(F32))

    zeros = jnp.zeros((1, tq), I32)
    _, thr_u, _, _ = lax.while_loop(lambda s: (s[0] < 32) & (s[3] == 0.0), search_body,
                                    (jnp.int32(0), zeros, zeros, jnp.float32(0.0)))
    thr = thr_u ^ int_min
    need = n_sel - count(lambda key, kpos: key > thr)
    n_eq = count(lambda key, kpos: key == thr)
    has_tie = jnp.max(n_eq - need) > 0.0

    nbits = int(lp).bit_length()

    def tie_cut():
        def cut_body(i, cut):
            cand = cut | lax.shift_left(jnp.int32(1), nbits - 1 - i)
            cnt = count(lambda key, kpos: (key == thr) & (kpos < cand))
            return jnp.where(cnt <= need, cand, cut)
        return lax.fori_loop(0, nbits, cut_body, jnp.zeros((1, tq), I32))

    cut = lax.cond(has_tie, tie_cut, lambda: jnp.full((1, tq), 2 ** 30, I32))

    eye = (lax.broadcasted_iota(I32, (tq, tq), 0) == lax.broadcasted_iota(I32, (tq, tq), 1)).astype(BF16)

    def mask_tile(off, tie, check_admissible=False):
        key = keyt_scr[pl.ds(off, KEY_TILE), :]
        kpos, adm = positions(off, KEY_TILE)
        sel = ((key > thr) | ((key == thr) & (kpos < cut))) if tie else (key >= thr)
        if check_admissible:
            sel = sel & adm
        sel_q = lax.dot_general(eye, jnp.where(sel, 1.0, 0.0).astype(BF16), NT_DIMS, preferred_element_type=F32)
        mask_scr[:, pl.ds(off, KEY_TILE)] = jnp.where(sel_q > 0.5, 0.0, NEG_BIG)

    def mask_all(tie):
        def slab(off, width):
            for u in range(width // KEY_TILE):
                mask_tile(pl.multiple_of(off + u * KEY_TILE, KEY_TILE), tie)

        def run():
            slabs(0, n_tiles - 1, slab)
            mask_tile(last_off, tie, check_admissible=True)
        return run

    lax.cond(has_tie, mask_all(True), mask_all(False))

    grp = A_HEADS // A_KV_HEADS
    rows = grp * tq
    head_cols = lambda h: slice(h * A_HEAD_DIM, (h + 1) * A_HEAD_DIM)
    for g in range(A_KV_HEADS):
        qg = jnp.concatenate([q_ref[:, head_cols(g * grp + i)] for i in range(grp)], axis=0)
        m_scr[...] = jnp.full(m_scr.shape, NEG_BIG, F32)

        def qk_slab(off, width, near=False, qg=qg, g=g):
            lg = jnp.dot(qg, kt_ref[head_cols(g), pl.ds(off, width)], preferred_element_type=F32)
            lg = lg.reshape(grp, tq, width) + mask_scr[:, pl.ds(off, width)][None]
            if near:
                back = jnp.clip((q0 - off) // KEY_TILE, 0, 1)
                lg = lg + bias_ref[back, g * grp:(g + 1) * grp, 0:tq, :]
            lg = lg.reshape(rows, width)
            lg_scr[:, pl.ds(off, width)] = lg
            mx = lg[:, 0:LANES]
            for u in range(1, width // LANES):
                mx = jnp.maximum(mx, lg[:, u * LANES:(u + 1) * LANES])
            m_scr[...] = jnp.maximum(m_scr[...], mx)

        def near_body(kt, carry, qk_slab=qk_slab):
            qk_slab(tile_off(kt), KEY_TILE, near=True)
            return carry

        slabs(0, n_far, qk_slab)
        lax.fori_loop(n_far, n_tiles, near_body, 0)
        m_scr[...] = jnp.broadcast_to(jnp.max(m_scr[...], axis=-1, keepdims=True), m_scr.shape)
        acc_scr[...] = jnp.zeros(acc_scr.shape, F32)

        def pv_slab(off, width, g=g):
            lg = lg_scr[:, pl.ds(off, width)]
            m_b = m_scr[...]
            ps = [jnp.exp2(lg[:, u * LANES:(u + 1) * LANES] - m_b) for u in range(width // LANES)]
            p = jnp.concatenate(ps, axis=1).astype(BF16)
            v_ones = v_ref[pl.ds(off, width), 2 * g * A_HEAD_DIM:2 * (g + 1) * A_HEAD_DIM]
            acc_scr[...] += jnp.dot(p, v_ones, preferred_element_type=F32)

        slabs(0, n_tiles, pv_slab)
        out = acc_scr[:, 0:A_HEAD_DIM] / acc_scr[:, A_HEAD_DIM:2 * A_HEAD_DIM]
        for i in range(grp):
            o_ref[:, head_cols(g * grp + i)] = out[i * tq:(i + 1) * tq].astype(o_ref.dtype)


def dsa_attention(q, qi_t, wi_t, k_t, v, ki, bias, *, tq, q_off, l_true, n_sel):
    bsz, t_len, _ = q.shape
    lp = v.shape[1]
    assert lp % (2 * KEY_TILE) == 0 and t_len % tq == 0 and q_off % KEY_TILE == 0
    assert tq == KEY_TILE or (t_len == tq and tq < KEY_TILE)
    grp = A_HEADS // A_KV_HEADS
    qspec = lambda w: pl.BlockSpec((None, tq, w), lambda b, j: (b, j, 0))
    qtspec = lambda h: pl.BlockSpec((None, h, tq), lambda b, j: (b, 0, j))
    whole = lambda a: pl.BlockSpec((None,) + a.shape[1:], lambda b, j: (b, 0, 0))
    return pl.pallas_call(
        functools.partial(_dsa_kernel, tq=tq, q_off=q_off, l_true=l_true, n_sel=n_sel, lp=lp),
        grid=(bsz, t_len // tq),
        in_specs=[qspec(q.shape[2]), qtspec(qi_t.shape[1]), qtspec(wi_t.shape[1]),
                  whole(k_t), whole(v), whole(ki),
                  _resident(bias.shape, lambda b, j: (0, 0, 0, 0))],
        out_specs=qspec(q.shape[2]),
        out_shape=jax.ShapeDtypeStruct(q.shape, BF16),
        scratch_shapes=[pltpu.VMEM((lp, tq), I32), pltpu.VMEM((tq, lp), F32), pltpu.VMEM((grp * tq, lp), F32),
                        pltpu.VMEM((grp * tq, LANES), F32), pltpu.VMEM((grp * tq, 2 * A_HEAD_DIM), F32)],
        compiler_params=_params("parallel", "arbitrary"),
        name="dsa_attention",
    )(q, qi_t, wi_t, k_t, v, ki, bias)


BAND_ROWS = 4 * CHUNK


def _band_bias(rel_table, chunk, grp):
    win = BAND_WINDOW + grp
    t = np.arange(grp)[:, None]
    s = np.arange(win)[None, :]
    k = np.arange(grp + win - 1)
    idx = np.clip(BAND_WINDOW + (grp - 1) - k, -BAND_MAX_REL, BAND_MAX_REL) + BAND_MAX_REL
    bias = _toeplitz(rel_table[:, idx], grp, win)
    lo = (t // chunk) * chunk
    allowed = (s >= lo) & (s < lo + BAND_WINDOW + chunk)
    return jnp.where(allowed[None], bias, NEG_BIG)


def _band_kernel(q_ref, kp_ref, kc_ref, vp_ref, vc_ref, bias_ref, o_ref, kcat, vcat, *, grp, first_has_no_past):
    i = pl.program_id(1)
    prev = kp_ref.shape[0]
    tq = q_ref.shape[0]
    win = prev + grp
    kcat[0:prev, :] = kp_ref[...]
    kcat[prev:prev + tq, :] = kc_ref[...]
    vcat[0:prev, :] = vp_ref[...]
    vcat[prev:prev + tq, :] = vc_ref[...]
    lane = lax.broadcasted_iota(I32, (grp, win), 1)
    pair_w = 2 * BAND_HEAD_DIM
    assert pair_w == LANES
    first_half = lax.broadcasted_iota(I32, (grp, pair_w), 1) < BAND_HEAD_DIM

    def attend(mask_missing_past):
        for r0 in range(0, tq, grp):
            if mask_missing_past:
                kpos_ok = (r0 - prev + lane) >= 0
            for hp in range(BAND_HEADS // 2):
                cols = slice(hp * pair_w, (hp + 1) * pair_w)
                q_pair = q_ref[r0:r0 + grp, cols]
                k_pair = kcat[r0:r0 + win, cols]
                v_pair = vcat[r0:r0 + win, cols]
                outs = []
                for side in range(2):
                    q_one = jnp.where(first_half == (side == 0), q_pair, jnp.zeros_like(q_pair))
                    lg = lax.dot_general(q_one, k_pair, NT_DIMS, preferred_element_type=F32)
                    lg = lg + bias_ref[2 * hp + side]
                    if mask_missing_past:
                        lg = jnp.where(kpos_ok, lg, NEG_BIG)
                    p = jnp.exp(lg - jnp.max(lg, axis=-1, keepdims=True))
                    den = jnp.sum(p, axis=-1, keepdims=True)
                    outs.append(jnp.dot(p.astype(BF16), v_pair, preferred_element_type=F32) / den)
                o_ref[r0:r0 + grp, cols] = jnp.where(first_half, outs[0], outs[1]).astype(o_ref.dtype)

    if first_has_no_past:
        lax.cond(i == 0, lambda: attend(True), lambda: attend(False))
    else:
        attend(False)


def band_attention_prompt(q, k, v, bias, *, tq=512, grp=128):
    bsz, length, w = q.shape
    assert tq == BAND_WINDOW and length % tq == 0
    cur = pl.BlockSpec((None, tq, w), lambda b, i: (b, i, 0))
    prv = pl.BlockSpec((None, tq, w), lambda b, i: (b, jnp.maximum(i - 1, 0), 0))
    return pl.pallas_call(
        functools.partial(_band_kernel, grp=grp, first_has_no_past=True),
        grid=(bsz, length // tq),
        in_specs=[cur, prv, cur, prv, cur, _resident(bias.shape, lambda b, i: (0, 0, 0))],
        out_specs=cur,
        out_shape=jax.ShapeDtypeStruct(q.shape, BF16),
        scratch_shapes=[pltpu.VMEM((2 * tq, w), BF16), pltpu.VMEM((2 * tq, w), BF16)],
        compiler_params=_params("parallel", "arbitrary"),
        name="band_attention",
    )(q, k, k, v, v, bias)


def band_attention_sample(q, k_past, k_new, v_past, v_new, bias):
    bsz, t_len, w = q.shape
    past = k_past.shape[1]
    assert past == BAND_WINDOW
    new = pl.BlockSpec((None, t_len, w), lambda b, i: (b, 0, 0))
    old = pl.BlockSpec((None, past, w), lambda b, i: (b, 0, 0))
    return pl.pallas_call(
        functools.partial(_band_kernel, grp=t_len, first_has_no_past=False),
        grid=(bsz, 1),
        in_specs=[new, old, new, old, new, _resident(bias.shape, lambda b, i: (0, 0, 0))],
        out_specs=new,
        out_shape=jax.ShapeDtypeStruct(q.shape, BF16),
        scratch_shapes=[pltpu.VMEM((past + t_len, w), BF16), pltpu.VMEM((past + t_len, w), BF16)],
        compiler_params=_params("parallel", "arbitrary"),
        name="band_attention_sample",
    )(q, k_past, k_new, v_past, v_new, bias)


CARRY_ROWS = 8
SSD_ROWS = 128
MLSTM_ROWS = 256


def _conv_carry_init(conv0_ref, w_ref, carry_scr, width):
    for k in range(1, width):
        z = None
        for i in range(k, width):
            term = w_ref[width - 1 - i:width - i, :] * conv0_ref[width - 2 - (i - k):width - 1 - (i - k), :]
            z = term if z is None else z + term
        carry_scr[k - 1:k, :] = z


def _causal_conv_chunk(x_ref, carry_scr, w_ref, b_ref, tc, width):
    x = x_ref[...]
    first_row = lax.broadcasted_iota(I32, x.shape, 0) == 0
    z = None
    for k in range(width - 1, 0, -1):
        y = x * w_ref[width - 1 - k:width - k, :]
        z = y if z is None else y + delayed
        delayed = jnp.where(first_row, carry_scr[k - 1:k, :], pltpu.roll(z, 1, 0))
        carry_scr[k - 1:k, :] = z[tc - 1:tc, :]
    return b_ref[...] + x * w_ref[width - 1:width, :] + delayed


def _tri(tc):
    r = lax.broadcasted_iota(I32, (tc, tc), 0)
    c = lax.broadcasted_iota(I32, (tc, tc), 1)
    return r >= c


def _eye(n):
    r = lax.broadcasted_iota(I32, (n, n), 0)
    c = lax.broadcasted_iota(I32, (n, n), 1)
    return (r == c).astype(F32)


def _cumsum_rows(x, causal):
    return jnp.dot(causal.astype(F32), x, precision=lax.Precision.HIGHEST, preferred_element_type=F32)


def _transpose_f32(x):
    return lax.dot_general(_eye(x.shape[1]), x, NT_DIMS, precision=lax.Precision.HIGHEST,
                           preferred_element_type=F32)


def _ssd_kernel(xbc_ref, z_ref, dt_ref, conv0_ref, h0_ref, cw_ref, cb_ref, dtb_ref, alog_ref, dskip_ref,
                ng_ref, y_ref, hs_ref, carry_scr, y_scr, *, tc, d_inner, n_heads):
    c = pl.program_id(1)

    @pl.when(c == 0)
    def _():
        hs_ref[...] = h0_ref[...]
        _conv_carry_init(conv0_ref, cw_ref, carry_scr, SSD_CONV)

    xs = _silu(_causal_conv_chunk(xbc_ref, carry_scr, cw_ref, cb_ref, tc, SSD_CONV))
    gn = SSD_GROUPS * SSD_D_STATE
    hg = n_heads // SSD_GROUPS
    causal = _tri(tc)
    x_dt = dt_ref[...] + dtb_ref[...]
    dt = jnp.maximum(x_dt, 0.0) + jnp.log1p(jnp.exp(-jnp.abs(x_dt)))
    a_head = -jnp.exp(alog_ref[...])
    cum = _cumsum_rows(dt * a_head, causal)
    cum_t = _transpose_f32(cum)
    cum_last = cum[tc - 1:tc, :]
    pair_w = 2 * SSD_HEAD_DIM
    assert pair_w == LANES and hg % 2 == 0
    first_lanes = lax.broadcasted_iota(I32, (tc, pair_w), 1) < SSD_HEAD_DIM
    first_rows = lax.broadcasted_iota(I32, (pair_w, SSD_D_STATE), 0) < SSD_HEAD_DIM
    pick = lambda a, b: jnp.where(first_lanes, a, b)
    for g in range(SSD_GROUPS):
        bm = xs[:, d_inner + g * SSD_D_STATE:d_inner + (g + 1) * SSD_D_STATE].astype(BF16)
        cm = xs[:, d_inner + gn + g * SSD_D_STATE:d_inner + gn + (g + 1) * SSD_D_STATE].astype(BF16)
        cb = lax.dot_general(cm, bm, NT_DIMS, preferred_element_type=F32)
        for pp in range(hg // 2):
            p = g * (hg // 2) + pp
            ha, hb = 2 * p, 2 * p + 1
            cols = slice(p * pair_w, (p + 1) * pair_w)
            col_a, col_b = cum[:, ha:ha + 1], cum[:, hb:hb + 1]
            xh = xs[:, cols]
            xdt = xh * pick(dt[:, ha:ha + 1], dt[:, hb:hb + 1])
            xdt_b = xdt.astype(BF16)
            ys = []
            for h, col in ((ha, col_a), (hb, col_b)):
                decay = jnp.exp(jnp.where(causal, col - cum_t[h:h + 1, :], -jnp.inf))
                ys.append(jnp.dot((cb * decay).astype(BF16), xdt_b, preferred_element_type=F32))
            h0 = hs_ref[p]
            inter = lax.dot_general(cm, h0.astype(BF16), NT_DIMS, preferred_element_type=F32)
            y = pick(ys[0], ys[1]) + pick(jnp.exp(col_a), jnp.exp(col_b)) * inter
            y_scr[:, cols] = y + dskip_ref[:, cols] * xh
            last_a, last_b = cum_last[:, ha:ha + 1], cum_last[:, hb:hb + 1]
            w = (pick(jnp.exp(last_a - col_a), jnp.exp(last_b - col_b)) * xdt).astype(BF16)
            keep = jnp.where(first_rows, jnp.exp(last_a), jnp.exp(last_b))
            hs_ref[p] = keep * h0 + lax.dot_general(w, bm, TN_DIMS, preferred_element_type=F32)
    yz = y_scr[...] * _silu(z_ref[...])
    gw = d_inner // SSD_GROUPS
    for g in range(SSD_GROUPS):
        cols = slice(g * gw, (g + 1) * gw)
        seg = yz[:, cols]
        ms = jnp.mean(seg * seg, axis=-1, keepdims=True)
        y_ref[:, cols] = (seg * lax.rsqrt(ms + NORM_EPS) * ng_ref[:, cols]).astype(y_ref.dtype)


def ssd_scan(xbc, z, dt, conv0, h0, conv_w, conv_b, dt_bias, a_log, d_skip_cols, norm_g, *, tc):
    bsz, length, cc = xbc.shape
    d_inner = z.shape[2]
    n_heads = dt.shape[2]
    assert length % tc == 0
    state_shape = h0.shape
    h0 = h0.reshape(bsz, n_heads // 2, 2 * state_shape[2], state_shape[3])
    seq = lambda w: pl.BlockSpec((None, tc, w), lambda b, c: (b, c, 0))
    per_b3 = lambda s: pl.BlockSpec((None,) + s, lambda b, c: (b,) + (0,) * len(s))
    row = lambda w: _resident((1, w), lambda b, c: (0, 0))
    y, h_new = pl.pallas_call(
        functools.partial(_ssd_kernel, tc=tc, d_inner=d_inner, n_heads=n_heads),
        grid=(bsz, length // tc),
        in_specs=[seq(cc), seq(d_inner), seq(n_heads), per_b3(conv0.shape[1:]), per_b3(h0.shape[1:]),
                  _resident(conv_w.shape, lambda b, c: (0, 0)), row(cc), row(n_heads), row(n_heads),
                  row(d_inner), row(d_inner)],
        out_specs=[seq(d_inner), per_b3(h0.shape[1:])],
        out_shape=[jax.ShapeDtypeStruct((bsz, length, d_inner), BF16), jax.ShapeDtypeStruct(h0.shape, F32)],
        scratch_shapes=[pltpu.VMEM((CARRY_ROWS, cc), F32), pltpu.VMEM((tc, d_inner), F32)],
        compiler_params=_params("parallel", "arbitrary"),
        name="ssd_scan",
    )(xbc, z, dt, conv0, h0, conv_w, conv_b, dt_bias, a_log, d_skip_cols, norm_g)
    return y, h_new.reshape(state_shape)


def _mlstm_kernel(xc_ref, v_ref, o_ref, gates_ref, conv0_ref, c0_ref, n0_ref, m0_ref, cw_ref, cb_ref, wq_ref,
                  wk_ref, gb_ref, ng_ref, h_ref, cs_ref, ns_ref, ms_ref, carry_scr, *, tc, d_inner):
    c = pl.program_id(1)

    @pl.when(c == 0)
    def _():
        cs_ref[...] = c0_ref[...]
        ns_ref[...] = n0_ref[...]
        ms_ref[...] = m0_ref[...]
        _conv_carry_init(conv0_ref, cw_ref, carry_scr, MLSTM_CONV)

    xa = _silu(_causal_conv_chunk(xc_ref, carry_scr, cw_ref, cb_ref, tc, MLSTM_CONV)).astype(BF16)
    blk = wq_ref.shape[1]
    dh = d_inner // MLSTM_HEADS
    q = jnp.concatenate([jnp.dot(xa[:, j * blk:(j + 1) * blk], wq_ref[j], preferred_element_type=F32)
                         for j in range(d_inner // blk)], axis=1)
    k = jnp.concatenate([jnp.dot(xa[:, j * blk:(j + 1) * blk], wk_ref[j], preferred_element_type=F32)
                         for j in range(d_inner // blk)], axis=1) * dh ** -0.5
    causal = _tri(tc)
    gates = gates_ref[...] + gb_ref[...]
    log_f = jnp.minimum(gates, 0.0) - jnp.log1p(jnp.exp(-jnp.abs(gates)))
    f_cum = _cumsum_rows(log_f, causal)
    f_cum_t = _transpose_f32(f_cum)
    gates_t = _transpose_f32(gates)
    for hh in range(MLSTM_HEADS):
        cols = slice(hh * dh, (hh + 1) * dh)
        fh = MLSTM_HEADS + hh
        qh = q[:, cols].astype(BF16)
        kh = k[:, cols]
        vh = v_ref[:, cols]
        fc = f_cum[:, fh:fh + 1]
        i_col = gates[:, hh:hh + 1]
        m0 = ms_ref[0:1, hh:hh + 1]
        d_log = jnp.where(causal, fc - f_cum_t[fh:fh + 1, :] + gates_t[hh:hh + 1, :], -jnp.inf)
        inter = fc + m0
        m = jnp.maximum(jnp.max(d_log, axis=-1, keepdims=True), inter)
        s = lax.dot_general(qh, kh.astype(BF16), NT_DIMS, preferred_element_type=F32) * jnp.exp(d_log - m)
        w_inter = jnp.exp(inter - m)
        c0 = cs_ref[hh]
        n0 = ns_ref[hh:hh + 1, :]
        num = (jnp.dot(s.astype(BF16), vh, preferred_element_type=F32)
               + w_inter * jnp.dot(qh, c0.astype(BF16), preferred_element_type=F32))
        den = (jnp.sum(s, axis=-1, keepdims=True)
               + w_inter * jnp.sum(q[:, cols] * n0, axis=-1, keepdims=True))
        h = num / jnp.maximum(jnp.abs(den), jnp.exp(-m))
        m_end = m[tc - 1:tc, :]
        f_last = fc[tc - 1:tc, :]
        w_end = jnp.exp(f_last - fc + i_col - m_end)
        decay = jnp.exp(f_last + m0 - m_end)
        wk = w_end * kh
        cs_ref[hh] = decay * c0 + lax.dot_general(wk.astype(BF16), vh, TN_DIMS, preferred_element_type=F32)
        ns_ref[hh:hh + 1, :] = decay * n0 + jnp.sum(wk, axis=0, keepdims=True)
        ms_ref[0:1, hh:hh + 1] = m_end
        h = jax.nn.sigmoid(o_ref[:, cols]) * h
        h = h - jnp.mean(h, axis=-1, keepdims=True)
        h = h * lax.rsqrt(jnp.mean(h * h, axis=-1, keepdims=True) + NORM_EPS)
        h_ref[:, cols] = (h * ng_ref[:, cols]).astype(h_ref.dtype)


def mlstm_scan(xc, v, o, gates, conv0, c0, n0, m0, conv_w, conv_b, wq_bd, wk_bd, gate_b, norm_g, *, tc):
    bsz, length, d_inner = xc.shape
    assert length % tc == 0
    seq = lambda w: pl.BlockSpec((None, tc, w), lambda b, c: (b, c, 0))
    per_b = lambda s: pl.BlockSpec((None,) + s, lambda b, c: (b,) + (0,) * len(s))
    row = lambda w: _resident((1, w), lambda b, c: (0, 0))
    return pl.pallas_call(
        functools.partial(_mlstm_kernel, tc=tc, d_inner=d_inner),
        grid=(bsz, length // tc),
        in_specs=[seq(d_inner), seq(d_inner), seq(d_inner), seq(gates.shape[2]), per_b(conv0.shape[1:]),
                  per_b(c0.shape[1:]), per_b(n0.shape[1:]), per_b(m0.shape[1:]),
                  _resident(conv_w.shape, lambda b, c: (0, 0)), row(d_inner),
                  _resident(wq_bd.shape, lambda b, c: (0, 0, 0)), _resident(wk_bd.shape, lambda b, c: (0, 0, 0)),
                  row(gates.shape[2]), row(d_inner)],
        out_specs=[seq(d_inner), per_b(c0.shape[1:]), per_b(n0.shape[1:]), per_b(m0.shape[1:])],
        out_shape=[jax.ShapeDtypeStruct((bsz, length, d_inner), BF16), jax.ShapeDtypeStruct(c0.shape, F32),
                   jax.ShapeDtypeStruct(n0.shape, F32), jax.ShapeDtypeStruct(m0.shape, F32)],
        scratch_shapes=[pltpu.VMEM((CARRY_ROWS, d_inner), F32)],
        compiler_params=_params("parallel", "arbitrary"),
        name="mlstm_scan",
    )(xc, v, o, gates, conv0, c0, n0, m0, conv_w, conv_b, wq_bd, wk_bd, gate_b, norm_g)


def _block_diag(w_blk, blk):
    n, c, d = w_blk.shape
    per = blk // c
    eye = jnp.eye(per, dtype=w_blk.dtype)
    tiles = w_blk.reshape(n // per, per, c, d)
    return jnp.einsum("jpcd,pq->jpcqd", tiles, eye).reshape(n // per, per * c, per * d)


def _mixer_a(xp, xs, shapes, cache_k, cache_v, cache_ki, w_in, t5_table):
    (bp, lp_), (bs, ls) = shapes
    hd, kvd, idd = A_HEADS * A_HEAD_DIM, A_KV_HEADS * A_HEAD_DIM, IDX_HEADS * IDX_DIM
    w_pad, cols = _pad_cols(w_in, (hd, kvd, kvd, idd, IDX_DIM, IDX_HEADS))
    log2e = math.log2(math.e)
    outs = [(cols[0], BF16, A_HEAD_DIM ** -0.5 * log2e), (cols[1], F32, None, A_KV_HEADS), (cols[1], BF16),
            (cols[2], F32, None, A_KV_HEADS), (cols[2], BF16), (cols[3], BF16), (cols[4], F32), (cols[4], BF16),
            (cols[5], F32)]
    w_pad = w_pad.astype(BF16)
    bias = _t5_bias_tiles(t5_table) * log2e
    on_lanes = lambda a: jnp.swapaxes(a, 1, 2)

    def with_ones(vv):
        b_, l_, _ = vv.shape
        v4 = vv.reshape(b_, l_, A_KV_HEADS, A_HEAD_DIM)
        return jnp.concatenate([v4, jnp.ones_like(v4)], axis=-1).reshape(b_, l_, 2 * kvd)

    q, k, kb, v, vb, qi, ki, kib, wi = project(xp, w_pad, outs, name="project_a")
    r3 = lambda a, b_, l: a.reshape(b_, l, a.shape[-1])
    att_p = dsa_attention(r3(q, bp, lp_), on_lanes(r3(qi, bp, lp_)), on_lanes(r3(wi, bp, lp_)),
                          on_lanes(r3(kb, bp, lp_)), with_ones(r3(vb, bp, lp_)), r3(kib, bp, lp_), bias, tq=KEY_TILE,
                          q_off=0,
                          l_true=lp_,
                          n_sel=min(IDX_TOPK_MAX, lp_ // 4))
    outs_p = (k.reshape(bp, lp_, A_KV_HEADS, A_HEAD_DIM), v.reshape(bp, lp_, A_KV_HEADS, A_HEAD_DIM),
              ki.reshape(bp, lp_, IDX_DIM))

    q, k, kb, v, vb, qi, ki, kib, wi = project(xs, w_pad, outs, name="project_a")
    past = cache_k.shape[1]
    total = past + ls
    lpad = -(-total // (2 * KEY_TILE)) * (2 * KEY_TILE)

    def with_past(cache, new):
        parts = [cache.reshape(bs, past, -1).astype(BF16), r3(new, bs, ls)]
        if lpad != total:
            parts.append(jnp.zeros((bs, lpad - total, new.shape[-1]), BF16))
        return jnp.concatenate(parts, axis=1)

    att_s = dsa_attention(r3(q, bs, ls), on_lanes(r3(qi, bs, ls)), on_lanes(r3(wi, bs, ls)),
                          on_lanes(with_past(cache_k, kb)), with_ones(with_past(cache_v, vb)),
                          with_past(cache_ki, kib), bias,
                          tq=ls, q_off=past,
                          l_true=total, n_sel=min(IDX_TOPK_MAX, total // 4))
    outs_s = (k.reshape(bs, ls, A_KV_HEADS, A_HEAD_DIM), v.reshape(bs, ls, A_KV_HEADS, A_HEAD_DIM),
              ki.reshape(bs, ls, IDX_DIM))
    return att_p.reshape(bp * lp_, hd), att_s.reshape(bs * ls, hd), outs_p, outs_s


def _mixer_b(xp, xs, shapes, cache_k, cache_v, w_in, rel_table, prompt_ffn):
    (bp, lp_), (bs, ls) = shapes
    hd = BAND_HEADS * BAND_HEAD_DIM
    w_pad, cols = _pad_cols(w_in, (hd, hd, hd))
    attn_outs = [(cols[0], BF16, BAND_HEAD_DIM ** -0.5), (cols[1], BF16), (cols[2], BF16)]
    kv_outs = [(cols[1], F32, None, BAND_HEADS), (cols[2], F32, None, BAND_HEADS)]
    w_pad = w_pad.astype(BF16)

    xp, q, kb, vb = ffn_project(xp, *prompt_ffn, w_pad, attn_outs)
    r3 = lambda a, b_, l: a.reshape(b_, l, hd)
    att_p = band_attention_prompt(r3(q, bp, lp_), r3(kb, bp, lp_), r3(vb, bp, lp_),
                                  _band_bias(rel_table, CHUNK, BAND_ROWS), grp=BAND_ROWS)
    keep = min(BAND_WINDOW, lp_)
    x_keep = xp.reshape(bp, lp_, -1)[:, lp_ - keep:].reshape(bp * keep, -1)
    k, v = project(x_keep, w_pad, kv_outs, name="project_b_keep")
    heads = lambda a, b_, l: a.reshape(b_, l, BAND_HEADS, BAND_HEAD_DIM)
    outs_p = (heads(k, bp, keep), heads(v, bp, keep))

    q, kb, vb = project(xs, w_pad, attn_outs, name="project_b")
    k, v = project(xs, w_pad, kv_outs, name="project_b_keep")
    past = cache_k.shape[1]
    att_s = band_attention_sample(r3(q, bs, ls), cache_k.reshape(bs, past, hd).astype(BF16), r3(kb, bs, ls),
                                  cache_v.reshape(bs, past, hd).astype(BF16), r3(vb, bs, ls),
                                  _band_bias(rel_table, ls, ls))
    outs_s = (jnp.concatenate([cache_k, heads(k, bs, ls)], axis=1)[:, ls:],
              jnp.concatenate([cache_v, heads(v, bs, ls)], axis=1)[:, ls:])
    return xp, att_p.reshape(bp * lp_, hd), att_s.reshape(bs * ls, hd), outs_p, outs_s


def _mixer_c(xp, xs, shapes, ssm0, conv0, w_in, conv_w, conv_b, dt_bias, a_log, d_skip, norm_g):
    n_heads = a_log.shape[0]
    d_inner = n_heads * SSD_HEAD_DIM
    conv_dim = conv_w.shape[1]
    w_pad, cols = _pad_cols(w_in, (d_inner, conv_dim, n_heads))
    outs = [(cols[0], F32), (cols[1], F32), (cols[2], F32)]
    w_pad = w_pad.astype(BF16)
    row = lambda a: a.reshape(1, -1)
    d_skip_cols = jnp.repeat(d_skip, SSD_HEAD_DIM).reshape(1, d_inner)
    res = []
    for x, (b_, l), h0, c0 in ((xp, shapes[0], None, None), (xs, shapes[1], ssm0, conv0)):
        if h0 is None:
            h0 = jnp.zeros((b_, n_heads, SSD_HEAD_DIM, SSD_D_STATE), F32)
            c0 = jnp.zeros((b_, SSD_CONV - 1, conv_dim), F32)
        z, xbc, dt = project(x, w_pad, outs, name="project_c")
        xbc3 = xbc.reshape(b_, l, conv_dim)
        y, h_new = ssd_scan(xbc3, z.reshape(b_, l, d_inner), dt.reshape(b_, l, n_heads), c0, h0, conv_w,
                            row(conv_b), row(dt_bias), row(a_log), d_skip_cols, row(norm_g), tc=min(SSD_ROWS, l))
        conv_new = jnp.concatenate([c0, xbc3], axis=1)[:, l:]
        res.append((y.reshape(b_ * l, d_inner), (h_new, conv_new)))
    return res[0][0], res[1][0], res[0][1], res[1][1]


def _mixer_d(xp, xs, shapes, c0s, n0s, m0s, conv0s, w_in, conv_w, conv_b, wq_blk, wk_blk, gate_b, norm_g):
    d_inner = conv_w.shape[1]
    dh = d_inner // MLSTM_HEADS
    w_pad, cols = _pad_cols(w_in, (d_inner, d_inner, d_inner, 2 * MLSTM_HEADS))
    gcol = (cols[3][0], cols[3][1], cols[3][1])
    outs = [(cols[0], F32), (cols[1], BF16), (cols[2], F32), (gcol, F32)]
    w_pad = w_pad.astype(BF16)
    blk = 2 * LANES
    wq_bd = _block_diag(wq_blk, blk).astype(BF16)
    wk_bd = _block_diag(wk_blk, blk).astype(BF16)
    gate_b_pad = jnp.zeros((1, gcol[1]), F32).at[0, :2 * MLSTM_HEADS].set(gate_b)
    row = lambda a: a.reshape(1, -1)
    res = []
    for x, (b_, l), st in ((xp, shapes[0], None), (xs, shapes[1], (c0s, n0s, m0s, conv0s))):
        if st is None:
            st = (jnp.zeros((b_, MLSTM_HEADS, dh, dh), F32), jnp.zeros((b_, MLSTM_HEADS, dh), F32),
                  jnp.zeros((b_, MLSTM_HEADS), F32), jnp.zeros((b_, MLSTM_CONV - 1, d_inner), F32))
        c0, n0, m0, conv0 = st
        xc, v, o, gates = project(x, w_pad, outs, name="project_d")
        r3 = lambda a: a.reshape(b_, l, a.shape[-1])
        h, c_new, n_new, m_new = mlstm_scan(r3(xc), r3(v), r3(o), r3(gates), conv0, c0, n0,
                                            m0.reshape(b_, 1, MLSTM_HEADS), conv_w, row(conv_b), wq_bd, wk_bd,
                                            gate_b_pad, row(norm_g), tc=min(MLSTM_ROWS, l))
        conv_new = jnp.concatenate([conv0, r3(xc)], axis=1)[:, l:]
        res.append((h.reshape(b_ * l, d_inner), (c_new, n_new, m_new.reshape(b_, MLSTM_HEADS), conv_new)))
    return res[0][0], res[1][0], res[0][1], res[1][1]


def kernel(x_prompt, x_sample, cache_a_k, cache_a_v, cache_a_kidx, cache_b_k, cache_b_v, state_c_ssm, state_c_conv, state_d_c, state_d_n, state_d_m, state_d_conv, a_w_in, a_w_out, t5_table, b_w_in, b_w_out, b_rel_table, c_w_in, c_conv_w, c_conv_b, c_dt_bias, c_a_log, c_d_skip, c_norm_g, c_w_out, d_w_in, d_conv_w, d_conv_b, d_wq_blk, d_wk_blk, d_gate_b, d_norm_g, d_w_out, ffn1_wg, ffn1_wu, ffn1_wd, ffn2_wg, ffn2_wu, ffn2_wd, ln_g, ln_b):
    bp, lp_, d = x_prompt.shape
    bs, ls, _ = x_sample.shape
    depth = ffn1_wg.shape[0]
    alpha = (2.0 * depth) ** 0.25
    shapes = ((bp, lp_), (bs, ls))
    xp = x_prompt.reshape(bp * lp_, d)
    xs = x_sample.reshape(bs * ls, d)
    ffn_w = [[w.astype(BF16) for w in ws] for ws in ((ffn1_wg, ffn1_wu, ffn1_wd), (ffn2_wg, ffn2_wu, ffn2_wd))]
    w_out = [w.astype(BF16) for w in (a_w_out, b_w_out, c_w_out, d_w_out)]
    row = lambda a: a.reshape(1, -1)
    extra = {}
    for i in range(depth):
        g, b = ln_g[i], ln_b[i]
        kind = i % 4
        ffn1 = (*ffn_w[0], i, row(g[0]), row(b[0]), alpha)
        if kind != 1:
            xp = ffn_postnorm(xp, *ffn1)
        xs = ffn_postnorm(xs, *ffn1)
        if kind == 0:
            mp, ms, op, os_ = _mixer_a(xp, xs, shapes, cache_a_k, cache_a_v, cache_a_kidx, a_w_in, t5_table)
        elif kind == 1:
            xp, mp, ms, op, os_ = _mixer_b(xp, xs, shapes, cache_b_k, cache_b_v, b_w_in, b_rel_table, ffn1)
        elif kind == 2:
            mp, ms, op, os_ = _mixer_c(xp, xs, shapes, state_c_ssm, state_c_conv, c_w_in, c_conv_w, c_conv_b,
                                       c_dt_bias, c_a_log, c_d_skip, c_norm_g)
        else:
            mp, ms, op, os_ = _mixer_d(xp, xs, shapes, state_d_c, state_d_n, state_d_m, state_d_conv, d_w_in,
                                       d_conv_w, d_conv_b, d_wq_blk, d_wk_blk, d_gate_b, d_norm_g)
        extra[kind] = (op, os_)
        xp = mixout_ffn(xp, mp, w_out[kind], row(g[1]), row(b[1]), *ffn_w[1], i, row(g[2]), row(b[2]), alpha)
        xs = mixout_ffn(xs, ms, w_out[kind], row(g[1]), row(b[1]), *ffn_w[1], i, row(g[2]), row(b[2]), alpha)
    prompt_side = tuple(t for kind in range(4) for t in extra[kind][0])
    sample_side = tuple(t for kind in range(4) for t in extra[kind][1])
    return (xp.reshape(bp, lp_, d), xs.reshape(bs, ls, d)) + prompt_side + sample_side
```
